```python
import math
import jax, jax.numpy as jnp
from jax import lax
import numpy as np


D_MODEL = 1024
BATCH = 4
SEQ = 4096
DEPTH = 2
DEC_BATCH = 16
DEC_SEQ = 64
PAST_LEN = 2048

CHUNK = 64
PLE_DIM = 256
SSM_WIDTH = 256
SSM_GROUP = 16
SSM_GROUPS = SSM_WIDTH // SSM_GROUP
SSM_STATE = 64
ATT_HEADS = 8
ATT_HEAD_DIM = 64
ATT_WIDTH = ATT_HEADS * ATT_HEAD_DIM
ATT_PAST_CHUNKS = 8
REL_CLIP = 256
RET_HEADS = 4
RET_KEY_DIM = 64
RET_VAL_DIM = 64
RET_QK = RET_HEADS * RET_KEY_DIM
RET_V = RET_HEADS * RET_VAL_DIM
ROPE_BASE = 10000.0
PEER_HEADS = 8
PEER_KEY_DIM = 256
N_KEYS = 128
N_EXPERTS = N_KEYS * N_KEYS
PEER_TOPK = 16
PEER_BLOCK = 128
DN_ALPHA = (2 * DEPTH) ** 0.25
DN_BETA = (8 * DEPTH) ** -0.25
LN_EPS = 1e-5
IN_SPLITS = (SSM_WIDTH, ATT_WIDTH, ATT_WIDTH, ATT_WIDTH, RET_QK, RET_QK, RET_V, RET_V, D_MODEL, D_MODEL, D_MODEL)
IN_COLS = sum(IN_SPLITS)
IN_OFFSETS = tuple(np.cumsum(IN_SPLITS)[:-1].tolist())

kernel_name = 'hybrid_s5_bandattn_retention_peer_stream_step'

F32 = jnp.float32


def layer_norm(x, g, b):
    xf = x.astype(F32)
    mu = xf.mean(-1, keepdims=True)
    var = jnp.square(xf - mu).mean(-1, keepdims=True)
    return ((xf - mu) * lax.rsqrt(var + LN_EPS) * g.astype(F32) + b.astype(F32)).astype(x.dtype)


def s5_mixer(u, h0_re, h0_im, lam_re, lam_im, b_re, b_im, c_re, c_im, log_dt, d_skip, w_glu, b_glu):
    bsz, t, _ = u.shape
    uf = u.astype(F32)
    ug = uf.reshape(bsz, t, SSM_GROUPS, SSM_GROUP)
    lam = lax.complex(lam_re.astype(F32), lam_im.astype(F32))
    dt = jnp.exp(log_dt.astype(F32))[:, None]
    lam_dt = lam * dt
    a_bar = jnp.exp(lam_dt)
    b = lax.complex(b_re.astype(F32), b_im.astype(F32))
    b_bar = ((a_bar - 1.0) / lam)[..., None] * b
    bu = jnp.einsum('gnp,btgp->btgn', b_bar, ug.astype(jnp.complex64))
    a_full = jnp.broadcast_to(a_bar, bu.shape)

    def combine(left, right):
        return (left[0] * right[0], right[0] * left[1] + right[1])

    _, h = lax.associative_scan(combine, (a_full, bu), axis=1)
    steps = jnp.arange(1, t + 1, dtype=F32)
    carry = jnp.exp(lam_dt[None] * steps[:, None, None])
    h0 = lax.complex(h0_re.astype(F32), h0_im.astype(F32))
    h = h + carry[None] * h0[:, None]
    c = lax.complex(c_re.astype(F32), c_im.astype(F32))
    y = jnp.real(jnp.einsum('gpn,btgn->btgp', c, h)).reshape(bsz, t, SSM_WIDTH) + d_skip.astype(F32) * uf
    z = jax.nn.gelu(y)
    out = z * jax.nn.sigmoid(z @ w_glu.astype(F32) + b_glu.astype(F32))
    h_last = h[:, -1]
    return out.astype(u.dtype), jnp.real(h_last), jnp.imag(h_last)


def rel_bias_lookup(dist, table):
    idx = jnp.clip(dist, -REL_CLIP, REL_CLIP) + REL_CLIP
    return jnp.moveaxis(table[idx], -1, 0).astype(F32)


def band_attention_prompt(q, k, v, table, n_rows):
    bsz, t, h, hd = q.shape
    n_chunks = t // CHUNK
    pad = ATT_PAST_CHUNKS * CHUNK
    band = pad + CHUNK
    kp = jnp.pad(k, ((0, 0), (pad, 0), (0, 0), (0, 0)))
    vp = jnp.pad(v, ((0, 0), (pad, 0), (0, 0), (0, 0)))
    i = jnp.arange(CHUNK)
    j = jnp.arange(band)
    bias = rel_bias_lookup(i[:, None] - j[None, :] + pad, table)
    qc = jnp.moveaxis(q.reshape(bsz, n_chunks, CHUNK, h, hd), 1, 0)
    scale = hd ** -0.5

    def one_chunk(args):
        c, qb = args
        start = c * CHUNK
        kb = lax.dynamic_slice_in_dim(kp, start, band, axis=1)
        vb = lax.dynamic_slice_in_dim(vp, start, band, axis=1)
        s = jnp.einsum('bihd,bjhd->bhij', qb, kb).astype(F32) * scale + bias
        valid = (j + start) >= pad
        s = jnp.where(valid, s, jnp.finfo(F32).min)
        p = jax.nn.softmax(s, axis=-1).astype(vb.dtype)
        return jnp.einsum('bhij,bjhd->bihd', p, vb)

    o = lax.map(one_chunk, (jnp.arange(n_chunks), qc))
    o = jnp.moveaxis(o, 0, 1).reshape(bsz, t, h * hd)
    return o, kp[:, -n_rows:], vp[:, -n_rows:]


def band_attention_sample(q, k, v, cache_k, cache_v, table):
    bsz, l, h, hd = q.shape
    r = cache_k.shape[1]
    kk = jnp.concatenate([cache_k, k], axis=1)
    vv = jnp.concatenate([cache_v, v], axis=1)
    q_pos = PAST_LEN + jnp.arange(l)
    k_pos = PAST_LEN - r + jnp.arange(r + l)
    bias = rel_bias_lookup(q_pos[:, None] - k_pos[None, :], table)
    s = jnp.einsum('bihd,bjhd->bhij', q, kk).astype(F32) * (hd ** -0.5) + bias
    p = jax.nn.softmax(s, axis=-1).astype(vv.dtype)
    o = jnp.einsum('bhij,bjhd->bihd', p, vv).reshape(bsz, l, h * hd)
    return o, k, v


def rope(x, pos):
    half = x.shape[-1] // 2
    freqs = ROPE_BASE ** (-jnp.arange(half, dtype=F32) / half)
    ang = pos.astype(F32)[:, None] * freqs[None]
    cos = jnp.cos(ang)[None, :, None, :]
    sin = jnp.sin(ang)[None, :, None, :]
    x1 = x[..., :half].astype(F32)
    x2 = x[..., half:].astype(F32)
    return jnp.concatenate([x1 * cos - x2 * sin, x1 * sin + x2 * cos], axis=-1).astype(x.dtype)


def retention(q, k, v, s0, chunk):
    bsz, t, h, dk = q.shape
    dv = v.shape[-1]
    nc = t // chunk
    lg = jnp.log1p(-(2.0 ** (-5.0 - jnp.arange(h, dtype=F32))))
    qc = q.reshape(bsz, nc, chunk, h, dk).astype(F32) * (dk ** -0.5)
    kc = k.reshape(bsz, nc, chunk, h, dk).astype(F32)
    vc = v.reshape(bsz, nc, chunk, h, dv).astype(F32)
    i = jnp.arange(chunk, dtype=F32)
    intra = jnp.exp(lg[:, None, None] * jnp.abs(i[:, None] - i[None, :]))
    scores = jnp.einsum('bnihd,bnjhd->bnhij', qc, kc) * intra
    o = jnp.einsum('bnhij,bnjhe->bnihe', scores, vc)
    k_w = jnp.exp(lg[None, :] * (chunk - 1.0 - i)[:, None])
    contrib = jnp.einsum('bnjhd,jh,bnjhe->bnhde', kc, k_w, vc)
    n = jnp.arange(nc, dtype=F32)
    lag = n[:, None] - 1.0 - n[None, :]
    carry_w = jnp.where(lag >= 0, jnp.exp(lg[:, None, None] * chunk * jnp.maximum(lag, 0.0)), 0.0)
    s0f = s0.astype(F32)
    s_prev = (jnp.einsum('hnm,bmhde->bnhde', carry_w, contrib)
              + jnp.exp(lg[None, :] * chunk * n[:, None])[None, :, :, None, None] * s0f[:, None])
    q_w = jnp.exp(lg[None, :] * (i[:, None] + 1.0))
    o = o + jnp.einsum('bnihd,ih,bnhde->bnihe', qc, q_w, s_prev)
    fin_w = jnp.exp(lg[:, None] * chunk * (nc - 1.0 - n)[None, :])
    s_fin = (jnp.einsum('hm,bmhde->bhde', fin_w, contrib)
             + jnp.exp(lg * chunk * nc)[None, :, None, None] * s0f)
    return o.reshape(bsz, t, h, dv), s_fin


def peer_ffn(x, w_q, sub_keys, u_tab, v_tab):
    bsz, t, d = x.shape
    xf = x.reshape(-1, d)
    n_tok = xf.shape[0]
    q = (xf @ w_q).reshape(n_tok, PEER_HEADS, 2, PEER_KEY_DIM // 2).astype(F32)
    s = jnp.einsum('thsd,hskd->thsk', q, sub_keys.astype(F32))
    sc, ix = lax.top_k(s, PEER_TOPK)
    cand = (sc[:, :, 0, :, None] + sc[:, :, 1, None, :]).reshape(n_tok, PEER_HEADS, -1)
    cand_ix = (ix[:, :, 0, :, None] * N_KEYS + ix[:, :, 1, None, :]).reshape(n_tok, PEER_HEADS, -1)
    top_s, sel = lax.top_k(cand, PEER_TOPK)
    expert = jnp.take_along_axis(cand_ix, sel, axis=-1)
    gate = jax.nn.softmax(top_s, axis=-1)
    n_blk = -(-n_tok // PEER_BLOCK)
    pad = n_blk * PEER_BLOCK - n_tok
    xb = jnp.pad(xf, ((0, pad), (0, 0))).reshape(n_blk, PEER_BLOCK, d)
    eb = jnp.pad(expert, ((0, pad), (0, 0), (0, 0))).reshape(n_blk, PEER_BLOCK, PEER_HEADS, PEER_TOPK)
    gb = jnp.pad(gate, ((0, pad), (0, 0), (0, 0))).reshape(n_blk, PEER_BLOCK, PEER_HEADS, PEER_TOPK)

    def block(args):
        xk, ek, gk = args
        u = u_tab[ek]
        act = jax.nn.gelu(jnp.einsum('bhkd,bd->bhk', u, xk).astype(F32))
        w = (gk * act).astype(xk.dtype)
        return jnp.einsum('bhk,bhkd->bd', w, v_tab[ek])

    out = lax.map(block, (xb, eb, gb)).reshape(-1, d)[:n_tok]
    return out.reshape(bsz, t, d)


def trunk_layer(x, pe, h0_re, h0_im, ck, cv, s0, pos, prompt, n_rows, lp):
    bsz, t, _ = x.shape
    h = x @ lp['w_in']
    (u_ssm, q_att, k_att, v_att, q_ret, k_ret, v_ret, g_ret,
     gate_ssm, gate_att, gate_ret) = jnp.split(h, IN_OFFSETS, axis=-1)
    y_ssm, h_re, h_im = s5_mixer(u_ssm, h0_re, h0_im, lp['ssm_lam_re'], lp['ssm_lam_im'],
                                 lp['ssm_b_re'], lp['ssm_b_im'], lp['ssm_c_re'], lp['ssm_c_im'],
                                 lp['ssm_log_dt'], lp['ssm_d'], lp['ssm_w_glu'], lp['ssm_b_glu'])
    q_att = q_att.reshape(bsz, t, ATT_HEADS, ATT_HEAD_DIM)
    k_att = k_att.reshape(bsz, t, ATT_HEADS, ATT_HEAD_DIM)
    v_att = v_att.reshape(bsz, t, ATT_HEADS, ATT_HEAD_DIM)
    if prompt:
        y_att, new_k, new_v = band_attention_prompt(q_att, k_att, v_att, lp['att_rel_bias'], n_rows)
    else:
        y_att, new_k, new_v = band_attention_sample(q_att, k_att, v_att, ck, cv, lp['att_rel_bias'])
    q_ret = rope(q_ret.reshape(bsz, t, RET_HEADS, RET_KEY_DIM), pos)
    k_ret = rope(k_ret.reshape(bsz, t, RET_HEADS, RET_KEY_DIM), pos)
    o_ret, s_fin = retention(q_ret, k_ret, v_ret.reshape(bsz, t, RET_HEADS, RET_VAL_DIM), s0,
                             CHUNK if prompt else t)
    mu = o_ret.mean(-1, keepdims=True)
    var = jnp.square(o_ret - mu).mean(-1, keepdims=True)
    o_ret = ((o_ret - mu) * lax.rsqrt(var + LN_EPS)).reshape(bsz, t, RET_V)
    y_ret = (o_ret * lp['ret_gn_g'].astype(F32) * jax.nn.silu(g_ret.astype(F32))).astype(x.dtype)
    merged = (jax.nn.sigmoid(gate_ssm) * (y_ssm @ lp['w_br_ssm'])
              + jax.nn.sigmoid(gate_att) * (y_att @ lp['w_br_att'])
              + jax.nn.sigmoid(gate_ret) * (y_ret @ lp['w_br_ret']))
    x = layer_norm(DN_ALPHA * x + merged @ lp['w_o'], lp['ln1_g'], lp['ln1_b'])
    x = layer_norm(DN_ALPHA * x + peer_ffn(x, lp['peer_w_q'], lp['peer_sub_keys'], lp['peer_u'], lp['peer_v']),
                   lp['ln2_g'], lp['ln2_b'])
    ple = jax.nn.sigmoid(x @ lp['ple_w_g']) * (pe @ lp['ple_w_p'])
    x = layer_norm(DN_ALPHA * x + ple, lp['ln3_g'], lp['ln3_b'])
    return x, (h_re, h_im, new_k, new_v, s_fin)


def setup_inputs(seed: int = 0) -> dict:
    key = jax.random.key(seed)
    ks = iter(jax.random.split(key, 48))

    def nrm(shape, scale):
        return jax.random.normal(next(ks), shape, F32) * scale

    att_rows = min(ATT_PAST_CHUNKS * CHUNK, PAST_LEN)
    lam_im = jnp.tile(math.pi * jnp.arange(SSM_STATE, dtype=F32), (DEPTH, SSM_GROUPS, 1))
    return {
        'x_prompt': nrm((BATCH, SEQ, D_MODEL), 1.0),
        'x_sample': nrm((DEC_BATCH, DEC_SEQ, D_MODEL), 1.0),
        'p_prompt': nrm((DEPTH, BATCH, SEQ, PLE_DIM), 1.0),
        'p_sample': nrm((DEPTH, DEC_BATCH, DEC_SEQ, PLE_DIM), 1.0),
        'state_ssm_re': nrm((DEPTH, DEC_BATCH, SSM_GROUPS, SSM_STATE), 0.1),
        'state_ssm_im': nrm((DEPTH, DEC_BATCH, SSM_GROUPS, SSM_STATE), 0.1),
        'cache_attn_k': nrm((DEPTH, DEC_BATCH, att_rows, ATT_HEADS, ATT_HEAD_DIM), 1.0),
        'cache_attn_v': nrm((DEPTH, DEC_BATCH, att_rows, ATT_HEADS, ATT_HEAD_DIM), 1.0),
        'state_ret': nrm((DEPTH, DEC_BATCH, RET_HEADS, RET_KEY_DIM, RET_VAL_DIM), 4.0),
        'w_in': nrm((DEPTH, D_MODEL, IN_COLS), D_MODEL ** -0.5),
        'ssm_lam_re': -0.5 + nrm((DEPTH, SSM_GROUPS, SSM_STATE), 0.01),
        'ssm_lam_im': lam_im,
        'ssm_b_re': nrm((DEPTH, SSM_GROUPS, SSM_STATE, SSM_GROUP), (2 * SSM_GROUP) ** -0.5),
        'ssm_b_im': nrm((DEPTH, SSM_GROUPS, SSM_STATE, SSM_GROUP), (2 * SSM_GROUP) ** -0.5),
        'ssm_c_re': nrm((DEPTH, SSM_GROUPS, SSM_GROUP, SSM_STATE), SSM_STATE ** -0.5),
        'ssm_c_im': nrm((DEPTH, SSM_GROUPS, SSM_GROUP, SSM_STATE), SSM_STATE ** -0.5),
        'ssm_log_dt': jax.random.uniform(next(ks), (DEPTH, SSM_GROUPS), F32, math.log(1e-3), math.log(1e-1)),
        'ssm_d': nrm((DEPTH, SSM_WIDTH), 1.0),
        'ssm_w_glu': nrm((DEPTH, SSM_WIDTH, SSM_WIDTH), SSM_WIDTH ** -0.5),
        'ssm_b_glu': nrm((DEPTH, SSM_WIDTH), 0.01),
        'att_rel_bias': nrm((DEPTH, 2 * REL_CLIP + 1, ATT_HEADS), 0.1),
        'ret_gn_g': 1.0 + nrm((DEPTH, RET_V), 0.02),
        'w_br_ssm': nrm((DEPTH, SSM_WIDTH, D_MODEL), SSM_WIDTH ** -0.5),
        'w_br_att': nrm((DEPTH, ATT_WIDTH, D_MODEL), ATT_WIDTH ** -0.5),
        'w_br_ret': nrm((DEPTH, RET_V, D_MODEL), RET_V ** -0.5),
        'w_o': nrm((DEPTH, D_MODEL, D_MODEL), DN_BETA * D_MODEL ** -0.5),
        'ln1_g': 1.0 + nrm((DEPTH, D_MODEL), 0.02),
        'ln1_b': nrm((DEPTH, D_MODEL), 0.01),
        'peer_w_q': nrm((DEPTH, D_MODEL, PEER_HEADS * PEER_KEY_DIM), D_MODEL ** -0.5),
        'peer_sub_keys': nrm((DEPTH, PEER_HEADS, 2, N_KEYS, PEER_KEY_DIM // 2), (PEER_KEY_DIM // 2) ** -0.5),
        'peer_u': nrm((DEPTH, N_EXPERTS, D_MODEL), D_MODEL ** -0.5),
        'peer_v': nrm((DEPTH, N_EXPERTS, D_MODEL), DN_BETA * PEER_HEADS ** -0.5),
        'ln2_g': 1.0 + nrm((DEPTH, D_MODEL), 0.02),
        'ln2_b': nrm((DEPTH, D_MODEL), 0.01),
        'ple_w_g': nrm((DEPTH, D_MODEL, D_MODEL), D_MODEL ** -0.5),
        'ple_w_p': nrm((DEPTH, PLE_DIM, D_MODEL), DN_BETA * PLE_DIM ** -0.5),
        'ln3_g': 1.0 + nrm((DEPTH, D_MODEL), 0.02),
        'ln3_b': nrm((DEPTH, D_MODEL), 0.01),
    }


def reference(x_prompt, x_sample, p_prompt, p_sample, state_ssm_re, state_ssm_im, cache_attn_k, cache_attn_v,
              state_ret, w_in, ssm_lam_re, ssm_lam_im, ssm_b_re, ssm_b_im, ssm_c_re, ssm_c_im, ssm_log_dt,
              ssm_d, ssm_w_glu, ssm_b_glu, att_rel_bias, ret_gn_g, w_br_ssm, w_br_att, w_br_ret, w_o,
              ln1_g, ln1_b, peer_w_q, peer_sub_keys, peer_u, peer_v, ln2_g, ln2_b, ple_w_g, ple_w_p,
              ln3_g, ln3_b):
    n_rows = cache_attn_k.shape[2]
    bsz_p = x_prompt.shape[0]
    pos_p = jnp.arange(x_prompt.shape[1])
    pos_s = PAST_LEN + jnp.arange(x_sample.shape[1])
    y_p, y_s = x_prompt, x_sample
    new_p = ([], [], [], [], [])
    new_s = ([], [], [], [], [])
    for li in range(DEPTH):
        lp = {
            'w_in': w_in[li], 'ssm_lam_re': ssm_lam_re[li], 'ssm_lam_im': ssm_lam_im[li],
            'ssm_b_re': ssm_b_re[li], 'ssm_b_im': ssm_b_im[li], 'ssm_c_re': ssm_c_re[li],
            'ssm_c_im': ssm_c_im[li], 'ssm_log_dt': ssm_log_dt[li], 'ssm_d': ssm_d[li],
            'ssm_w_glu': ssm_w_glu[li], 'ssm_b_glu': ssm_b_glu[li], 'att_rel_bias': att_rel_bias[li],
            'ret_gn_g': ret_gn_g[li], 'w_br_ssm': w_br_ssm[li], 'w_br_att': w_br_att[li],
            'w_br_ret': w_br_ret[li], 'w_o': w_o[li], 'ln1_g': ln1_g[li], 'ln1_b': ln1_b[li],
            'peer_w_q': peer_w_q[li], 'peer_sub_keys': peer_sub_keys[li], 'peer_u': peer_u[li],
            'peer_v': peer_v[li], 'ln2_g': ln2_g[li], 'ln2_b': ln2_b[li], 'ple_w_g': ple_w_g[li],
            'ple_w_p': ple_w_p[li], 'ln3_g': ln3_g[li], 'ln3_b': ln3_b[li],
        }
        zero_ssm = jnp.zeros((bsz_p, SSM_GROUPS, SSM_STATE), F32)
        zero_ret = jnp.zeros((bsz_p, RET_HEADS, RET_KEY_DIM, RET_VAL_DIM), F32)
        y_p, st = trunk_layer(y_p, p_prompt[li], zero_ssm, zero_ssm, None, None, zero_ret, pos_p,
                              True, n_rows, lp)
        for lst, a in zip(new_p, st):
            lst.append(a)
        y_s, st = trunk_layer(y_s, p_sample[li], state_ssm_re[li], state_ssm_im[li], cache_attn_k[li],
                              cache_attn_v[li], state_ret[li], pos_s, False, n_rows, lp)
        for lst, a in zip(new_s, st):
            lst.append(a)
    ssm_re_p = jnp.stack(new_p[0])
    ssm_im_p = jnp.stack(new_p[1])
    att_k_p = jnp.stack(new_p[2])
    att_v_p = jnp.stack(new_p[3])
    ret_p = jnp.stack(new_p[4])
    ssm_re_s = jnp.stack(new_s[0])
    ssm_im_s = jnp.stack(new_s[1])
    att_k_s = jnp.stack(new_s[2])
    att_v_s = jnp.stack(new_s[3])
    ret_s = jnp.stack(new_s[4])
    return (y_p, y_s, ssm_re_p, ssm_im_p, att_k_p, att_v_p, ret_p, ssm_re_s, ssm_im_s, att_k_s, att_v_s, ret_s)
```

```python
import functools

import jax
import jax.numpy as jnp
import numpy as np
from jax import lax
from jax.experimental import pallas as pl
from jax.experimental.pallas import tpu as pltpu

F32 = jnp.float32
BF = jnp.bfloat16

D_MODEL = 1024
DEPTH = 2
PAST_LEN = 2048
CHUNK = 64
PLE_DIM = 256
SSM_WIDTH = 256
SSM_GROUP = 16
SSM_GROUPS = 16
SSM_STATE = 64
SSM_FLAT = SSM_GROUPS * SSM_STATE
ATT_HEADS = 8
ATT_HEAD_DIM = 64
ATT_WIDTH = 512
ATT_PAST = 8 * CHUNK
ATT_BAND = ATT_PAST + CHUNK
REL_CLIP = 256
RET_HEADS = 4
RET_DIM = 64
ROPE_BASE = 10000.0
PEER_HEADS = 8
PEER_HALF = 128
N_KEYS = 128
N_EXPERTS = N_KEYS * N_KEYS
PEER_TOPK = 16
DN_ALPHA = (2 * DEPTH) ** 0.25
LN_EPS = 1e-5
LANES = 128
NEG_MASK = float(jnp.finfo(jnp.float32).min)

IN_COLS = 5888
OFF_GATES = 0
OFF_U_SSM = 3072
OFF_Q_ATT = 3328
OFF_K_ATT = 3840
OFF_V_ATT = 4352
OFF_Q_RET = 4864
OFF_K_RET = 5120
OFF_V_RET = 5376
OFF_G_RET = 5632
OLD_GATE_START = 2816

VMEM_LIMIT = 48 * 1024 * 1024


def _params(sem):
    return pltpu.CompilerParams(dimension_semantics=sem, vmem_limit_bytes=VMEM_LIMIT)


def _dot(a, b):
    return jnp.dot(a, b, preferred_element_type=F32)


def _dot_nt(a, b):
    return lax.dot_general(a, b, (((1,), (1,)), ((), ())), preferred_element_type=F32)


def _dot_tn(a, b):
    return lax.dot_general(a, b, (((0,), (0,)), ((), ())), preferred_element_type=F32)


def _layer_norm(r, g, b):
    mu = jnp.mean(r, axis=-1, keepdims=True)
    d = r - mu
    var = jnp.mean(d * d, axis=-1, keepdims=True)
    return d * lax.rsqrt(var + LN_EPS) * g + b


def _matmul_kernel(x_ref, w_ref, o_ref):
    o_ref[...] = _dot(x_ref[...].astype(BF), w_ref[...])


def _in_projection(x, w, tm, tn):
    m, k = x.shape
    n = w.shape[1]
    return pl.pallas_call(
        _matmul_kernel,
        out_shape=jax.ShapeDtypeStruct((m, n), F32),
        grid=(n // tn, m // tm),
        in_specs=[pl.BlockSpec((tm, k), lambda j, i: (i, 0)),
                  pl.BlockSpec((k, tn), lambda j, i: (0, j))],
        out_specs=pl.BlockSpec((tm, tn), lambda j, i: (i, j)),
        compiler_params=_params(("arbitrary", "arbitrary")),
        name="in_projection",
    )(x, w)


def _s5_kernel(u_ref, bm_ref, a_ref, cm_ref, d_ref, wg_ref, bg_ref, h0_ref, y_ref, hl_ref, hbuf, hst, *, steps, bp):
    @pl.when(pl.program_id(0) == 0)
    def _():
        hst[...] = h0_ref[...]

    u = u_ref[...]
    hbuf[...] = _dot(u.astype(BF), bm_ref[...])
    a_re = jnp.broadcast_to(a_ref[0:1, :], (bp, SSM_FLAT))
    a_im = jnp.broadcast_to(a_ref[1:2, :], (bp, SSM_FLAT))

    def step(t, carry):
        h_re, h_im = carry
        r0 = pl.multiple_of(t * bp, bp)
        n_re = a_re * h_re - a_im * h_im + hbuf[pl.ds(r0, bp), 0:SSM_FLAT]
        n_im = a_re * h_im + a_im * h_re + hbuf[pl.ds(r0, bp), SSM_FLAT:2 * SSM_FLAT]
        hbuf[pl.ds(r0, bp), 0:SSM_FLAT] = n_re
        hbuf[pl.ds(r0, bp), SSM_FLAT:2 * SSM_FLAT] = n_im
        return n_re, n_im

    h_re, h_im = lax.fori_loop(0, steps, step, (hst[:, 0:SSM_FLAT], hst[:, SSM_FLAT:2 * SSM_FLAT]))
    hst[:, 0:SSM_FLAT] = h_re
    hst[:, SSM_FLAT:2 * SSM_FLAT] = h_im
    hl_ref[...] = hst[...]
    y = _dot(hbuf[...].astype(BF), cm_ref[...]) + d_ref[...] * u
    z = jax.nn.gelu(y)
    y_ref[...] = z * jax.nn.sigmoid(_dot(z.astype(BF), wg_ref[...]) + bg_ref[...])


def _s5_mixer(u_tb, h0, sp, n_steps, bp, steps):
    rows = steps * bp
    kern = functools.partial(_s5_kernel, steps=steps, bp=bp)
    const = lambda c: (0, 0)
    return pl.pallas_call(
        kern,
        out_shape=(jax.ShapeDtypeStruct((n_steps * bp, SSM_WIDTH), F32),
                   jax.ShapeDtypeStruct((bp, 2 * SSM_FLAT), F32)),
        grid=(n_steps // steps,),
        in_specs=[pl.BlockSpec((rows, SSM_WIDTH), lambda c: (c, 0)),
                  pl.BlockSpec((SSM_WIDTH, 2 * SSM_FLAT), const),
                  pl.BlockSpec((2, SSM_FLAT), const),
                  pl.BlockSpec((2 * SSM_FLAT, SSM_WIDTH), const),
                  pl.BlockSpec((1, SSM_WIDTH), const),
                  pl.BlockSpec((SSM_WIDTH, SSM_WIDTH), const),
                  pl.BlockSpec((1, SSM_WIDTH), const),
                  pl.BlockSpec((bp, 2 * SSM_FLAT), const)],
        out_specs=(pl.BlockSpec((rows, SSM_WIDTH), lambda c: (c, 0)),
                   pl.BlockSpec((bp, 2 * SSM_FLAT), const)),
        scratch_shapes=[pltpu.VMEM((rows, 2 * SSM_FLAT), F32), pltpu.VMEM((bp, 2 * SSM_FLAT), F32)],
        compiler_params=_params(("arbitrary",)),
        name="s5_mixer",
    )(u_tb, sp["bmat"], sp["a"], sp["cmat"], sp["d"], sp["w_glu"], sp["b_glu"], h0)


def _s5_tables(lam_re, lam_im, b_re, b_im, c_re, c_im, log_dt, d_skip, w_glu, b_glu):
    lam = lax.complex(lam_re.astype(F32), lam_im.astype(F32))
    dt = jnp.exp(log_dt.astype(F32))[:, None]
    a_bar = jnp.exp(lam * dt)
    b_bar = ((a_bar - 1.0) / lam)[..., None] * lax.complex(b_re.astype(F32), b_im.astype(F32))
    eye = jnp.eye(SSM_GROUPS, dtype=F32)

    def in_blockdiag(m):
        m = jnp.transpose(m, (0, 2, 1))
        return (m[:, :, None, :] * eye[:, None, :, None]).reshape(SSM_WIDTH, SSM_FLAT)

    def out_blockdiag(m):
        m = jnp.transpose(m, (0, 2, 1))
        return (m[:, :, None, :] * eye[:, None, :, None]).reshape(SSM_FLAT, SSM_WIDTH)

    bmat = jnp.concatenate([in_blockdiag(jnp.real(b_bar)), in_blockdiag(jnp.imag(b_bar))], axis=1)
    cmat = jnp.concatenate([out_blockdiag(c_re.astype(F32)), -out_blockdiag(c_im.astype(F32))], axis=0)
    a = jnp.stack([jnp.real(a_bar).reshape(SSM_FLAT), jnp.imag(a_bar).reshape(SSM_FLAT)])
    return {"bmat": bmat.astype(BF), "cmat": cmat.astype(BF), "a": a,
            "d": d_skip.astype(F32).reshape(1, SSM_WIDTH), "w_glu": w_glu.astype(BF),
            "b_glu": b_glu.astype(F32).reshape(1, SSM_WIDTH)}


def _attend_chunk(q, kc, vc, bias_ref, valid):
    first = lax.broadcasted_iota(jnp.int32, (1, LANES), 1) < ATT_HEAD_DIM
    outs = []
    for hh in range(2):
        qm = jnp.where(first if hh == 0 else jnp.logical_not(first), q, 0.0).astype(BF)
        s = _dot_nt(qm, kc) + bias_ref[hh]
        if valid is not None:
            s = jnp.where(valid, s, NEG_MASK)
        m = jnp.max(s, axis=-1, keepdims=True)
        p = jnp.exp(s - m)
        l = jnp.sum(p, axis=-1, keepdims=True)
        outs.append(_dot(p.astype(BF), vc) / l)
    return jnp.where(first, outs[0], outs[1])


def _attn_prompt_kernel(q_ref, k0_ref, k1_ref, k2_ref, v0_ref, v1_ref, v2_ref, bias_ref, o_ref, *, cpb):
    tb = pl.program_id(2)
    rows = cpb * CHUNK
    kwin = jnp.concatenate([k0_ref[...], k1_ref[...], k2_ref[...]], axis=0).astype(BF)
    vwin = jnp.concatenate([v0_ref[...], v1_ref[...], v2_ref[...]], axis=0).astype(BF)
    col = lax.broadcasted_iota(jnp.int32, (1, ATT_BAND), 1)
    for c in range(cpb):
        lo = 2 * rows - ATT_PAST + c * CHUNK
        q = q_ref[c * CHUNK:(c + 1) * CHUNK, :] * (ATT_HEAD_DIM ** -0.5)
        valid = (col + (tb * rows - ATT_PAST + c * CHUNK)) >= 0
        o_ref[c * CHUNK:(c + 1) * CHUNK, :] = _attend_chunk(
            q, kwin[lo:lo + ATT_BAND], vwin[lo:lo + ATT_BAND], bias_ref, valid)


def _attn_prompt(h, bias, bsz, t):
    cpb = 4
    rows = cpb * CHUNK
    tbn = t // rows
    qc, kc, vc = OFF_Q_ATT // LANES, OFF_K_ATT // LANES, OFF_V_ATT // LANES

    def past(col0, back):
        return pl.BlockSpec((rows, LANES), lambda b, hp, tb: (b * tbn + jnp.maximum(tb - back, 0), col0 + hp))

    return pl.pallas_call(
        functools.partial(_attn_prompt_kernel, cpb=cpb),
        out_shape=jax.ShapeDtypeStruct((bsz * t, ATT_WIDTH), F32),
        grid=(bsz, ATT_HEADS // 2, tbn),
        in_specs=[past(qc, 0), past(kc, 2), past(kc, 1), past(kc, 0), past(vc, 2), past(vc, 1), past(vc, 0),
                  pl.BlockSpec((2, CHUNK, ATT_BAND), lambda b, hp, tb: (hp, 0, 0))],
        out_specs=pl.BlockSpec((rows, LANES), lambda b, hp, tb: (b * tbn + tb, hp)),
        compiler_params=_params(("arbitrary", "arbitrary", "arbitrary")),
        name="attn_prompt",
    )(h, h, h, h, h, h, h, bias)


def _attn_sample_kernel(q_ref, k_ref, v_ref, ck_ref, cv_ref, bias_ref, o_ref):
    kwin = jnp.concatenate([ck_ref[0], k_ref[...]], axis=0).astype(BF)
    vwin = jnp.concatenate([cv_ref[0], v_ref[...]], axis=0).astype(BF)
    q = q_ref[...] * (ATT_HEAD_DIM ** -0.5)
    o_ref[...] = _attend_chunk(q, kwin, vwin, bias_ref, None)


def _attn_sample(h, cache_k, cache_v, bias, bsz):
    qc, kc, vc = OFF_Q_ATT // LANES, OFF_K_ATT // LANES, OFF_V_ATT // LANES
    new = lambda col0: pl.BlockSpec((CHUNK, LANES), lambda b, hp: (b, col0 + hp))
    old = pl.BlockSpec((1, ATT_PAST, LANES), lambda b, hp: (b, 0, hp))
    return pl.pallas_call(
        _attn_sample_kernel,
        out_shape=jax.ShapeDtypeStruct((bsz * CHUNK, ATT_WIDTH), F32),
        grid=(bsz, ATT_HEADS // 2),
        in_specs=[new(qc), new(kc), new(vc), old, old,
                  pl.BlockSpec((2, CHUNK, ATT_BAND), lambda b, hp: (hp, 0, 0))],
        out_specs=pl.BlockSpec((CHUNK, LANES), lambda b, hp: (b, hp)),
        compiler_params=_params(("arbitrary", "arbitrary")),
        name="attn_sample",
    )(h, h, h, cache_k, cache_v, bias)


def _rel_bias_tile(table):
    i = jnp.arange(CHUNK)
    j = jnp.arange(ATT_BAND)
    idx = jnp.clip(i[:, None] - j[None, :] + ATT_PAST, -REL_CLIP, REL_CLIP) + REL_CLIP
    return jnp.moveaxis(table[idx], -1, 0).astype(F32)


def _ret_kernel(q_ref, k_ref, v_ref, g_ref, cos_ref, sin_ref, intra_ref, qw_ref, kw_ref, dec_ref, bd_ref, gn_ref,
                s0_ref, y_ref, sf_ref, state, *, cpb):
    @pl.when(pl.program_id(1) == 0)
    def _():
        state[...] = s0_ref[0]

    lane = lax.broadcasted_iota(jnp.int32, (1, LANES), 1)
    first = lane < RET_DIM
    low_half = (lane % RET_DIM) < (RET_DIM // 2)

    for c in range(cpb):
        rows = slice(c * CHUNK, (c + 1) * CHUNK)
        cos = cos_ref[rows, :]
        sin = sin_ref[rows, :]

        def rope(x):
            swapped = jnp.where(low_half, pltpu.roll(x, LANES - RET_DIM // 2, 1), pltpu.roll(x, RET_DIM // 2, 1))
            return x * cos + swapped * sin

        for hp in range(RET_HEADS // 2):
            cols = slice(hp * LANES, (hp + 1) * LANES)
            q = rope(q_ref[rows, cols]) * (RET_DIM ** -0.5)
            k = rope(k_ref[rows, cols])
            kb = k.astype(BF)
            vb = v_ref[rows, cols].astype(BF)
            intra = []
            for hh in range(2):
                qm = jnp.where(first if hh == 0 else jnp.logical_not(first), q, 0.0).astype(BF)
                s = _dot_nt(qm, kb) * intra_ref[2 * hp + hh]
                intra.append(_dot(s.astype(BF), vb))
            s_prev = state[hp]
            o = jnp.where(first, intra[0], intra[1]) + _dot((q * qw_ref[hp]).astype(BF), s_prev.astype(BF))
            kv = _dot_tn((k * kw_ref[hp]).astype(BF), vb)
            state[hp] = dec_ref[hp] * s_prev + bd_ref[...] * kv
            sum_a = jnp.sum(jnp.where(first, o, 0.0), axis=-1, keepdims=True)
            sum_b = jnp.sum(o, axis=-1, keepdims=True) - sum_a
            d = o - jnp.where(first, sum_a, sum_b) * (1.0 / RET_DIM)
            sq = d * d
            var_a = jnp.sum(jnp.where(first, sq, 0.0), axis=-1, keepdims=True)
            var_b = jnp.sum(sq, axis=-1, keepdims=True) - var_a
            var = jnp.where(first, var_a, var_b) * (1.0 / RET_DIM)
            g = g_ref[rows, cols]
            y_ref[rows, cols] = d * lax.rsqrt(var + LN_EPS) * gn_ref[:, cols] * (g * jax.nn.sigmoid(g))
    sf_ref[0] = state[...]


def _retention(h, s0, rt, gn, bsz, t, cpb):
    rows = cpb * CHUNK
    tbn = t // rows
    width = RET_HEADS * RET_DIM
    col = lambda off: pl.BlockSpec((rows, width), lambda b, tb: (b * tbn + tb, off // width))
    tab = pl.BlockSpec((rows, LANES), lambda b, tb: (tb, 0))
    const3 = lambda shape: pl.BlockSpec(shape, lambda b, tb: (0, 0, 0))
    st = pl.BlockSpec((1, 2, LANES, LANES), lambda b, tb: (b, 0, 0, 0))
    return pl.pallas_call(
        functools.partial(_ret_kernel, cpb=cpb),
        out_shape=(jax.ShapeDtypeStruct((bsz * t, width), F32),
                   jax.ShapeDtypeStruct((bsz, 2, LANES, LANES), F32)),
        grid=(bsz, tbn),
        in_specs=[col(OFF_Q_RET), col(OFF_K_RET), col(OFF_V_RET), col(OFF_G_RET), tab, tab,
                  const3((RET_HEADS, CHUNK, CHUNK)), const3((2, CHUNK, LANES)), const3((2, CHUNK, LANES)),
                  const3((2, LANES, LANES)), pl.BlockSpec((LANES, LANES), lambda b, tb: (0, 0)),
                  pl.BlockSpec((1, width), lambda b, tb: (0, 0)), st],
        out_specs=(pl.BlockSpec((rows, width), lambda b, tb: (b * tbn + tb, 0)), st),
        scratch_shapes=[pltpu.VMEM((2, LANES, LANES), F32)],
        compiler_params=_params(("arbitrary", "arbitrary")),
        name="retention",
    )(h, h, h, h, rt["cos"], rt["sin"], rt["intra"], rt["qw"], rt["kw"], rt["dec"], rt["bd"], gn, s0)


def _retention_tables(pos):
    lg = jnp.log1p(-(2.0 ** (-5.0 - jnp.arange(RET_HEADS, dtype=F32))))
    i = jnp.arange(CHUNK, dtype=F32)
    intra = jnp.exp(lg[:, None, None] * jnp.abs(i[:, None] - i[None, :]))
    lane_head = jnp.arange(2 * LANES) // RET_DIM
    lg_lane = lg[lane_head]
    qw = jnp.exp(lg_lane[None, :] * (i[:, None] + 1.0))
    kw = jnp.exp(lg_lane[None, :] * (CHUNK - 1.0 - i)[:, None])
    pair = lambda m: jnp.stack([m[:, :LANES], m[:, LANES:]])
    same_head = (jnp.arange(LANES)[:, None] // RET_DIM) == (jnp.arange(LANES)[None, :] // RET_DIM)
    bd = same_head.astype(F32)
    dec_lane = jnp.exp(lg_lane * CHUNK)
    dec = jnp.stack([bd * dec_lane[:LANES][:, None], bd * dec_lane[LANES:][:, None]])
    half = RET_DIM // 2
    freqs = ROPE_BASE ** (-jnp.arange(half, dtype=F32) / half)
    ang = pos.astype(F32)[:, None] * freqs[None]
    cos = jnp.tile(jnp.cos(ang), (1, LANES // half))
    sin = jnp.tile(jnp.concatenate([-jnp.sin(ang), jnp.sin(ang)], axis=1), (1, LANES // RET_DIM))
    return {"intra": intra, "qw": pair(qw), "kw": pair(kw), "dec": dec, "bd": bd, "cos": cos, "sin": sin}


def _pair_states(s):
    b = s.shape[0]
    s = s.astype(F32).reshape(b, 2, 2, RET_DIM, RET_DIM)
    out = jnp.zeros((b, 2, 2, RET_DIM, 2, RET_DIM), F32)
    out = out.at[:, :, 0, :, 0, :].set(s[:, :, 0]).at[:, :, 1, :, 1, :].set(s[:, :, 1])
    return out.reshape(b, 2, LANES, LANES)


def _unpair_states(s):
    b = s.shape[0]
    s = s.reshape(b, 2, 2, RET_DIM, 2, RET_DIM)
    return jnp.stack([s[:, :, 0, :, 0, :], s[:, :, 1, :, 1, :]], axis=2).reshape(b, RET_HEADS, RET_DIM, RET_DIM)


def _merge_kernel(x_ref, g1_ref, g2_ref, g3_ref, ys_ref, ya_ref, yr_ref, ws_ref, wa_ref, wr_ref, wo_ref,
                  lg_ref, lb_ref, o_ref):
    merged = (jax.nn.sigmoid(g1_ref[...]) * _dot(ys_ref[...].astype(BF), ws_ref[...])
              + jax.nn.sigmoid(g2_ref[...]) * _dot(ya_ref[...].astype(BF), wa_ref[...])
              + jax.nn.sigmoid(g3_ref[...]) * _dot(yr_ref[...].astype(BF), wr_ref[...]))
    r = DN_ALPHA * x_ref[...] + _dot(merged.astype(BF), wo_ref[...])
    o_ref[...] = _layer_norm(r, lg_ref[...], lb_ref[...])


def _merge(x, h, y_ssm, y_att, y_ret, lw, tm):
    n = x.shape[0]
    row = lambda w: pl.BlockSpec((tm, w), lambda i: (i, 0))
    gate = lambda j: pl.BlockSpec((tm, D_MODEL), lambda i: (i, j))
    full = lambda a: pl.BlockSpec(a.shape, lambda i: (0, 0))
    ws = (lw["w_br_ssm"], lw["w_br_att"], lw["w_br_ret"], lw["w_o"], lw["ln1_g"], lw["ln1_b"])
    return pl.pallas_call(
        _merge_kernel,
        out_shape=jax.ShapeDtypeStruct((n, D_MODEL), F32),
        grid=(n // tm,),
        in_specs=[row(D_MODEL), gate(0), gate(1), gate(2), row(SSM_WIDTH), row(ATT_WIDTH),
                  row(RET_HEADS * RET_DIM)] + [full(a) for a in ws],
        out_specs=row(D_MODEL),
        compiler_params=_params(("arbitrary",)),
        name="merge",
    )(x, h, h, h, y_ssm, y_att, y_ret, *ws)


CAND_ROWS = 80


def _cand_flat_index():
    f = [b for b in range(16)]
    for a in range(1, 8):
        f += [a * 16 + b for b in range(8)]
    f += [a * 16 for a in range(8, 16)]
    return np.asarray(f, np.float32)


def _split_bf16(x):
    hi = x.astype(BF)
    return hi, (x - hi.astype(F32)).astype(BF)


def _top16_rows(s, key_idx):
    t = s.shape[1]
    row16 = lax.broadcasted_iota(jnp.int32, (PEER_TOPK, t), 0)
    work = s
    rank = jnp.full(s.shape, float(PEER_TOPK), F32)
    vals = jnp.zeros((PEER_TOPK, t), F32)
    for r in range(PEER_TOPK):
        m = jnp.max(work, axis=0, keepdims=True)
        idx = jnp.min(jnp.where(work == m, key_idx, float(N_KEYS)), axis=0, keepdims=True)
        sel = key_idx == idx
        rank = jnp.where(sel, float(r), rank)
        work = jnp.where(sel, -jnp.inf, work)
        vals = jnp.where(row16 == r, m, vals)
    return rank, vals


def _route_kernel(x_ref, wh_ref, wl_ref, kh_ref, kl_ref, fidx_ref, r2_ref, e2_ref, c1_ref, e1_ref):
    x = x_ref[...]
    xh, xl = _split_bf16(x)
    q = _dot(xh, wh_ref[...]) + (_dot(xl, wh_ref[...]) + _dot(xh, wl_ref[...]))
    t = x.shape[0]
    key_idx = lax.broadcasted_iota(jnp.int32, (N_KEYS, t), 0).astype(F32)
    fidx = fidx_ref[...]

    def scores(hs):
        qh, ql = _split_bf16(q[:, hs * PEER_HALF:(hs + 1) * PEER_HALF])
        return _dot_nt(kh_ref[hs], qh) + (_dot_nt(kl_ref[hs], qh) + _dot_nt(kh_ref[hs], ql))

    for h in range(PEER_HEADS):
        s1 = scores(2 * h)
        s2 = scores(2 * h + 1)
        rank1, v1 = _top16_rows(s1, key_idx)
        rank2, v2 = _top16_rows(s2, key_idx)
        blocks = [v1[0:1] + v2]
        for a in range(1, 8):
            blocks.append(v1[a:a + 1] + v2[0:8])
        blocks.append(v1[8:16] + v2[0:1])
        cand = jnp.concatenate(blocks, axis=0)
        work = cand
        chosen = jnp.zeros(cand.shape, F32)
        for _ in range(PEER_TOPK):
            m = jnp.max(work, axis=0, keepdims=True)
            idx = jnp.min(jnp.where(work == m, fidx, 1e9), axis=0, keepdims=True)
            sel = fidx == idx
            chosen = jnp.where(sel, 1.0, chosen)
            work = jnp.where(sel, -jnp.inf, work)
        counts = [jnp.sum(chosen[0:16], axis=0, keepdims=True)]
        for a in range(1, 8):
            counts.append(jnp.sum(chosen[8 + 8 * a:16 + 8 * a], axis=0, keepdims=True))
        for a in range(8, 16):
            counts.append(chosen[64 + a:65 + a])
        c1 = jnp.zeros(s1.shape, F32)
        for a in range(PEER_TOPK):
            c1 = jnp.where(rank1 == float(a), counts[a], c1)
        z = jnp.sum(chosen * jnp.exp(cand - cand[0:1]), axis=0, keepdims=True)
        r2_ref[h] = rank2
        e2_ref[h] = jnp.exp(s2 - v2[0:1])
        c1_ref[h] = c1
        e1_ref[h] = jnp.exp(s1 - v1[0:1]) / z


def _route(x1, lw, tb):
    n = x1.shape[0]
    fidx = jnp.broadcast_to(jnp.asarray(_cand_flat_index())[:, None], (CAND_ROWS, tb))
    out = jax.ShapeDtypeStruct((PEER_HEADS, N_KEYS, n), F32)
    ospec = pl.BlockSpec((PEER_HEADS, N_KEYS, tb), lambda i: (0, 0, i))
    full2 = lambda a: pl.BlockSpec(a.shape, lambda i: (0, 0))
    full3 = lambda a: pl.BlockSpec(a.shape, lambda i: (0, 0, 0))
    return pl.pallas_call(
        _route_kernel,
        out_shape=(out, out, out, out),
        grid=(n // tb,),
        in_specs=[pl.BlockSpec((tb, D_MODEL), lambda i: (i, 0)), full2(lw["wq_hi"]), full2(lw["wq_lo"]),
                  full3(lw["keys_hi"]), full3(lw["keys_lo"]), full2(fidx)],
        out_specs=(ospec, ospec, ospec, ospec),
        compiler_params=_params(("arbitrary",)),
        name="peer_route",
    )(x1, lw["wq_hi"], lw["wq_lo"], lw["keys_hi"], lw["keys_lo"], fidx)


PEER_EXPERT_BLOCK = 1024


def _peer_kernel(x_ref, u_ref, vt_ref, r2_ref, e2_ref, c1_ref, e1_ref, lg_ref, lb_ref, o_ref, xb, acc, wt):
    e = pl.program_id(1)
    tb = x_ref.shape[0]

    @pl.when(e == 0)
    def _():
        xb[...] = x_ref[...].astype(BF)
        acc[...] = jnp.zeros_like(acc)

    act = jax.nn.gelu(_dot_nt(u_ref[...], xb[...]))
    keys_per_step = PEER_EXPERT_BLOCK // N_KEYS
    first_key = pl.multiple_of(e * keys_per_step, keys_per_step)
    for j in range(keys_per_step):
        for tc in range(tb // LANES):
            cols = slice(tc * LANES, (tc + 1) * LANES)
            w = jnp.zeros((N_KEYS, LANES), F32)
            for h in range(PEER_HEADS):
                c1 = c1_ref[h, pl.ds(first_key, keys_per_step), cols][j:j + 1]
                e1 = e1_ref[h, pl.ds(first_key, keys_per_step), cols][j:j + 1]
                w = w + jnp.where(r2_ref[h, :, cols] < c1, e2_ref[h, :, cols] * e1, 0.0)
            wt[j * N_KEYS:(j + 1) * N_KEYS, cols] = (w * act[j * N_KEYS:(j + 1) * N_KEYS, cols]).astype(BF)
    acc[...] += _dot(vt_ref[...], wt[...])

    @pl.when(e == pl.num_programs(1) - 1)
    def _():
        r = DN_ALPHA * x_ref[...] + acc[...].T
        o_ref[...] = _layer_norm(r, lg_ref[...], lb_ref[...])


def _peer(x1, route, lw, tb):
    n = x1.shape[0]
    eb = PEER_EXPERT_BLOCK
    rspec = pl.BlockSpec((PEER_HEADS, N_KEYS, tb), lambda i, e: (0, 0, i))
    vec = pl.BlockSpec((1, D_MODEL), lambda i, e: (0, 0))
    return pl.pallas_call(
        _peer_kernel,
        out_shape=jax.ShapeDtypeStruct((n, D_MODEL), F32),
        grid=(n // tb, N_EXPERTS // eb),
        in_specs=[pl.BlockSpec((tb, D_MODEL), lambda i, e: (i, 0)),
                  pl.BlockSpec((eb, D_MODEL), lambda i, e: (e, 0)),
                  pl.BlockSpec((D_MODEL, eb), lambda i, e: (0, e)),
                  rspec, rspec, rspec, rspec, vec, vec],
        out_specs=pl.BlockSpec((tb, D_MODEL), lambda i, e: (i, 0)),
        scratch_shapes=[pltpu.VMEM((tb, D_MODEL), BF), pltpu.VMEM((D_MODEL, tb), F32), pltpu.VMEM((eb, tb), BF)],
        compiler_params=_params(("arbitrary", "arbitrary")),
        name="peer_experts",
    )(x1, lw["peer_u"], lw["peer_vt"], *route, lw["ln2_g"], lw["ln2_b"])


def _ple_kernel(x_ref, pe_ref, wg_ref, wp_ref, lg_ref, lb_ref, o_ref):
    x = x_ref[...]
    ple = jax.nn.sigmoid(_dot(x.astype(BF), wg_ref[...])) * _dot(pe_ref[...].astype(BF), wp_ref[...])
    o_ref[...] = _layer_norm(DN_ALPHA * x + ple, lg_ref[...], lb_ref[...])


def _ple(x2, pe, lw, tm):
    n = x2.shape[0]
    full = lambda a: pl.BlockSpec(a.shape, lambda i: (0, 0))
    ws = (lw["ple_w_g"], lw["ple_w_p"], lw["ln3_g"], lw["ln3_b"])
    return pl.pallas_call(
        _ple_kernel,
        out_shape=jax.ShapeDtypeStruct((n, D_MODEL), F32),
        grid=(n // tm,),
        in_specs=[pl.BlockSpec((tm, D_MODEL), lambda i: (i, 0)), pl.BlockSpec((tm, PLE_DIM), lambda i: (i, 0))]
        + [full(a) for a in ws],
        out_specs=pl.BlockSpec((tm, D_MODEL), lambda i: (i, 0)),
        compiler_params=_params(("arbitrary",)),
        name="ple",
    )(x2, pe, *ws)


def _time_major(u, bsz, t, bp):
    u = jnp.transpose(u.reshape(bsz, t, -1), (1, 0, 2))
    return jnp.pad(u, ((0, 0), (0, bp - bsz), (0, 0))).reshape(t * bp, -1)


def _batch_major(y, bsz, t, bp):
    return jnp.transpose(y.reshape(t, bp, -1)[:, :bsz], (1, 0, 2)).reshape(bsz * t, -1)


def _layer(x, pe, h0_re, h0_im, cache_k, cache_v, s0, pos, lw, bsz, t, prompt, n_rows):
    n = bsz * t
    h = _in_projection(x, lw["w_in"], 512, IN_COLS // 2)

    bp = max(8, bsz)
    u_tb = _time_major(h[:, OFF_U_SSM:OFF_U_SSM + SSM_WIDTH], bsz, t, bp)
    h0 = jnp.concatenate([h0_re.reshape(bsz, SSM_FLAT), h0_im.reshape(bsz, SSM_FLAT)], axis=1).astype(F32)
    h0 = jnp.pad(h0, ((0, bp - bsz), (0, 0)))
    y_tb, h_last = _s5_mixer(u_tb, h0, lw["s5"], t, bp, CHUNK)
    y_ssm = _batch_major(y_tb, bsz, t, bp)
    new_re = h_last[:bsz, :SSM_FLAT].reshape(bsz, SSM_GROUPS, SSM_STATE)
    new_im = h_last[:bsz, SSM_FLAT:].reshape(bsz, SSM_GROUPS, SSM_STATE)

    k_att = h[:, OFF_K_ATT:OFF_K_ATT + ATT_WIDTH].reshape(bsz, t, ATT_HEADS, ATT_HEAD_DIM)
    v_att = h[:, OFF_V_ATT:OFF_V_ATT + ATT_WIDTH].reshape(bsz, t, ATT_HEADS, ATT_HEAD_DIM)
    if prompt:
        y_att = _attn_prompt(h, lw["rel_bias"], bsz, t)
        new_k, new_v = k_att[:, -n_rows:], v_att[:, -n_rows:]
    else:
        r = cache_k.shape[1]
        y_att = _attn_sample(h, cache_k.reshape(bsz, r, ATT_WIDTH), cache_v.reshape(bsz, r, ATT_WIDTH),
                             lw["rel_bias"], bsz)
        new_k, new_v = k_att, v_att

    y_ret, s_pair = _retention(h, _pair_states(s0), _retention_tables(pos), lw["ret_gn_g"], bsz, t,
                               4 if prompt else 1)
    s_fin = _unpair_states(s_pair)

    tok = 512 if n % 512 == 0 else 256
    x1 = _merge(x, h, y_ssm, y_att, y_ret, lw, 256)
    route = _route(x1, lw, 256)
    x2 = _peer(x1, route, lw, tok)
    x3 = _ple(x2, pe, lw, 256)
    return x3, (new_re, new_im, new_k, new_v, s_fin)


def _layer_weights(li, w_in, ssm_lam_re, ssm_lam_im, ssm_b_re, ssm_b_im, ssm_c_re, ssm_c_im, ssm_log_dt, ssm_d,
                   ssm_w_glu, ssm_b_glu, att_rel_bias, ret_gn_g, w_br_ssm, w_br_att, w_br_ret, w_o, ln1_g, ln1_b,
                   peer_w_q, peer_sub_keys, peer_u, peer_v, ln2_g, ln2_b, ple_w_g, ple_w_p, ln3_g, ln3_b):
    vec = lambda a: a[li].astype(F32).reshape(1, -1)
    w = w_in[li]
    wq_hi, wq_lo = _split_bf16(peer_w_q[li].astype(F32))
    keys_hi, keys_lo = _split_bf16(peer_sub_keys[li].astype(F32).reshape(2 * PEER_HEADS, N_KEYS, PEER_HALF))
    return {
        "w_in": jnp.concatenate([w[:, OLD_GATE_START:], w[:, :OLD_GATE_START]], axis=1).astype(BF),
        "s5": _s5_tables(ssm_lam_re[li], ssm_lam_im[li], ssm_b_re[li], ssm_b_im[li], ssm_c_re[li], ssm_c_im[li],
                         ssm_log_dt[li], ssm_d[li], ssm_w_glu[li], ssm_b_glu[li]),
        "rel_bias": _rel_bias_tile(att_rel_bias[li]),
        "ret_gn_g": vec(ret_gn_g),
        "w_br_ssm": w_br_ssm[li].astype(BF), "w_br_att": w_br_att[li].astype(BF), "w_br_ret": w_br_ret[li].astype(BF),
        "w_o": w_o[li].astype(BF), "ln1_g": vec(ln1_g), "ln1_b": vec(ln1_b),
        "wq_hi": wq_hi, "wq_lo": wq_lo, "keys_hi": keys_hi, "keys_lo": keys_lo,
        "peer_u": peer_u[li].astype(BF), "peer_vt": peer_v[li].astype(BF).T,
        "ln2_g": vec(ln2_g), "ln2_b": vec(ln2_b),
        "ple_w_g": ple_w_g[li].astype(BF), "ple_w_p": ple_w_p[li].astype(BF), "ln3_g": vec(ln3_g), "ln3_b": vec(ln3_b),
    }


def kernel(x_prompt, x_sample, p_prompt, p_sample, state_ssm_re, state_ssm_im, cache_attn_k, cache_attn_v, state_ret, w_in, ssm_lam_re, ssm_lam_im, ssm_b_re, ssm_b_im, ssm_c_re, ssm_c_im, ssm_log_dt, ssm_d, ssm_w_glu, ssm_b_glu, att_rel_bias, ret_gn_g, w_br_ssm, w_br_att, w_br_ret, w_o, ln1_g, ln1_b, peer_w_q, peer_sub_keys, peer_u, peer_v, ln2_g, ln2_b, ple_w_g, ple_w_p, ln3_g, ln3_b):
    bp, tp, _ = x_prompt.shape
    bs, ts, _ = x_sample.shape
    n_rows = cache_attn_k.shape[2]
    pos_p = jnp.arange(tp)
    pos_s = PAST_LEN + jnp.arange(ts)
    y_p = x_prompt.reshape(bp * tp, D_MODEL).astype(F32)
    y_s = x_sample.reshape(bs * ts, D_MODEL).astype(F32)
    new_p, new_s = [], []
    for li in range(DEPTH):
        lw = _layer_weights(li, w_in, ssm_lam_re, ssm_lam_im, ssm_b_re, ssm_b_im, ssm_c_re, ssm_c_im, ssm_log_dt,
                            ssm_d, ssm_w_glu, ssm_b_glu, att_rel_bias, ret_gn_g, w_br_ssm, w_br_att, w_br_ret, w_o,
                            ln1_g, ln1_b, peer_w_q, peer_sub_keys, peer_u, peer_v, ln2_g, ln2_b, ple_w_g, ple_w_p,
                            ln3_g, ln3_b)
        zero_ssm = jnp.zeros((bp, SSM_GROUPS, SSM_STATE), F32)
        zero_ret = jnp.zeros((bp, RET_HEADS, RET_DIM, RET_DIM), F32)
        y_p, st = _layer(y_p, p_prompt[li].reshape(bp * tp, PLE_DIM), zero_ssm, zero_ssm, None, None, zero_ret,
                         pos_p, lw, bp, tp, True, n_rows)
        new_p.append(st)
        y_s, st = _layer(y_s, p_sample[li].reshape(bs * ts, PLE_DIM), state_ssm_re[li], state_ssm_im[li],
                         cache_attn_k[li], cache_attn_v[li], state_ret[li], pos_s, lw, bs, ts, False, n_rows)
        new_s.append(st)
    stack = lambda states, k: jnp.stack([s[k] for s in states])
    return (y_p.reshape(bp, tp, D_MODEL), y_s.reshape(bs, ts, D_MODEL),
            stack(new_p, 0), stack(new_p, 1), stack(new_p, 2), stack(new_p, 3), stack(new_p, 4),
            stack(new_s, 0), stack(new_s, 1), stack(new_s, 2), stack(new_s, 3), stack(new_s, 4))
```

```python
import functools

import jax
import jax.numpy as jnp
import numpy as np
from jax import lax
from jax.experimental import pallas as pl
from jax.experimental.pallas import tpu as pltpu

F32 = jnp.float32
BF = jnp.bfloat16

D_MODEL = 1024
DEPTH = 2
PAST_LEN = 2048
CHUNK = 64
PLE_DIM = 256
SSM_WIDTH = 256
SSM_GROUP = 16
SSM_GROUPS = 16
SSM_STATE = 64
SSM_FLAT = SSM_GROUPS * SSM_STATE
ATT_HEADS = 8
ATT_HEAD_DIM = 64
ATT_WIDTH = 512
ATT_PAST = 8 * CHUNK
ATT_BAND = ATT_PAST + CHUNK
REL_CLIP = 256
RET_HEADS = 4
RET_DIM = 64
ROPE_BASE = 10000.0
PEER_HEADS = 8
PEER_HALF = 128
N_KEYS = 128
N_EXPERTS = N_KEYS * N_KEYS
PEER_TOPK = 16
SUBLANES = 8
BF_CHUNK = 256
DN_ALPHA = (2 * DEPTH) ** 0.25
LN_EPS = 1e-5
LANES = 128
NEG_MASK = float(jnp.finfo(jnp.float32).min)

IN_COLS = 5888
OFF_GATES = 0
OFF_U_SSM = 3072
OFF_Q_ATT = 3328
OFF_K_ATT = 3840
OFF_V_ATT = 4352
OFF_Q_RET = 4864
OFF_K_RET = 5120
OFF_V_RET = 5376
OFF_G_RET = 5632
OLD_GATE_START = 2816

VMEM_LIMIT = 48 * 1024 * 1024


def _params(sem):
    return pltpu.CompilerParams(dimension_semantics=sem, vmem_limit_bytes=VMEM_LIMIT)


def _dot(a, b):
    return jnp.dot(a, b, preferred_element_type=F32)


def _dot_nt(a, b):
    return lax.dot_general(a, b, (((1,), (1,)), ((), ())), preferred_element_type=F32)


def _dot_tn(a, b):
    return lax.dot_general(a, b, (((0,), (0,)), ((), ())), preferred_element_type=F32)


def _layer_norm(r, g, b):
    mu = jnp.mean(r, axis=-1, keepdims=True)
    d = r - mu
    var = jnp.mean(d * d, axis=-1, keepdims=True)
    return d * lax.rsqrt(var + LN_EPS) * g + b


def _matmul_kernel(x_ref, w_ref, o_ref):
    o_ref[...] = _dot(x_ref[...].astype(BF), w_ref[...])


def _in_projection(x, w, tm, tn):
    m, k = x.shape
    n = w.shape[1]
    return pl.pallas_call(
        _matmul_kernel,
        out_shape=jax.ShapeDtypeStruct((m, n), F32),
        grid=(n // tn, m // tm),
        in_specs=[pl.BlockSpec((tm, k), lambda j, i: (i, 0)),
                  pl.BlockSpec((k, tn), lambda j, i: (0, j))],
        out_specs=pl.BlockSpec((tm, tn), lambda j, i: (i, j)),
        compiler_params=_params(("arbitrary", "arbitrary")),
        name="in_projection",
    )(x, w)


def _s5_kernel(u_ref, bm_ref, a_ref, cm_ref, d_ref, wg_ref, bg_ref, h0_ref, y_ref, hl_ref, hbuf, hst, *, steps, bp):
    @pl.when(pl.program_id(0) == 0)
    def _():
        hst[...] = h0_ref[...]

    u = u_ref[...]
    hbuf[...] = _dot(u.astype(BF), bm_ref[...])
    a_re = jnp.broadcast_to(a_ref[0:1, :], (bp, SSM_FLAT))
    a_im = jnp.broadcast_to(a_ref[1:2, :], (bp, SSM_FLAT))

    def step(t, carry):
        h_re, h_im = carry
        r0 = pl.multiple_of(t * bp, bp)
        n_re = a_re * h_re - a_im * h_im + hbuf[pl.ds(r0, bp), 0:SSM_FLAT]
        n_im = a_re * h_im + a_im * h_re + hbuf[pl.ds(r0, bp), SSM_FLAT:2 * SSM_FLAT]
        hbuf[pl.ds(r0, bp), 0:SSM_FLAT] = n_re
        hbuf[pl.ds(r0, bp), SSM_FLAT:2 * SSM_FLAT] = n_im
        return n_re, n_im

    h_re, h_im = lax.fori_loop(0, steps, step, (hst[:, 0:SSM_FLAT], hst[:, SSM_FLAT:2 * SSM_FLAT]))
    hst[:, 0:SSM_FLAT] = h_re
    hst[:, SSM_FLAT:2 * SSM_FLAT] = h_im
    hl_ref[...] = hst[...]
    y = _dot(hbuf[...].astype(BF), cm_ref[...]) + d_ref[...] * u
    z = jax.nn.gelu(y)
    y_ref[...] = z * jax.nn.sigmoid(_dot(z.astype(BF), wg_ref[...]) + bg_ref[...])


def _s5_mixer(u_tb, h0, sp, n_steps, bp, steps):
    rows = steps * bp
    kern = functools.partial(_s5_kernel, steps=steps, bp=bp)
    const = lambda c: (0, 0)
    return pl.pallas_call(
        kern,
        out_shape=(jax.ShapeDtypeStruct((n_steps * bp, SSM_WIDTH), F32),
                   jax.ShapeDtypeStruct((bp, 2 * SSM_FLAT), F32)),
        grid=(n_steps // steps,),
        in_specs=[pl.BlockSpec((rows, SSM_WIDTH), lambda c: (c, 0)),
                  pl.BlockSpec((SSM_WIDTH, 2 * SSM_FLAT), const),
                  pl.BlockSpec((2, SSM_FLAT), const),
                  pl.BlockSpec((2 * SSM_FLAT, SSM_WIDTH), const),
                  pl.BlockSpec((1, SSM_WIDTH), const),
                  pl.BlockSpec((SSM_WIDTH, SSM_WIDTH), const),
                  pl.BlockSpec((1, SSM_WIDTH), const),
                  pl.BlockSpec((bp, 2 * SSM_FLAT), const)],
        out_specs=(pl.BlockSpec((rows, SSM_WIDTH), lambda c: (c, 0)),
                   pl.BlockSpec((bp, 2 * SSM_FLAT), const)),
        scratch_shapes=[pltpu.VMEM((rows, 2 * SSM_FLAT), F32), pltpu.VMEM((bp, 2 * SSM_FLAT), F32)],
        compiler_params=_params(("arbitrary",)),
        name="s5_mixer",
    )(u_tb, sp["bmat"], sp["a"], sp["cmat"], sp["d"], sp["w_glu"], sp["b_glu"], h0)


def _s5_tables(lam_re, lam_im, b_re, b_im, c_re, c_im, log_dt, d_skip, w_glu, b_glu):
    lr, li = lam_re.astype(F32), lam_im.astype(F32)
    dt = jnp.exp(log_dt.astype(F32))[:, None]
    mag = jnp.exp(lr * dt)
    a_re, a_im = mag * jnp.cos(li * dt), mag * jnp.sin(li * dt)
    den = lr * lr + li * li
    k_re = ((a_re - 1.0) * lr + a_im * li) / den
    k_im = (a_im * lr - (a_re - 1.0) * li) / den
    br, bi = b_re.astype(F32), b_im.astype(F32)
    bbar_re = k_re[..., None] * br - k_im[..., None] * bi
    bbar_im = k_re[..., None] * bi + k_im[..., None] * br
    eye = jnp.eye(SSM_GROUPS, dtype=F32)

    def in_blockdiag(m):
        m = jnp.transpose(m, (0, 2, 1))
        return (m[:, :, None, :] * eye[:, None, :, None]).reshape(SSM_WIDTH, SSM_FLAT)

    def out_blockdiag(m):
        m = jnp.transpose(m, (0, 2, 1))
        return (m[:, :, None, :] * eye[:, None, :, None]).reshape(SSM_FLAT, SSM_WIDTH)

    bmat = jnp.concatenate([in_blockdiag(bbar_re), in_blockdiag(bbar_im)], axis=1)
    cmat = jnp.concatenate([out_blockdiag(c_re.astype(F32)), -out_blockdiag(c_im.astype(F32))], axis=0)
    a = jnp.stack([a_re.reshape(SSM_FLAT), a_im.reshape(SSM_FLAT)])
    return {"bmat": bmat.astype(BF), "cmat": cmat.astype(BF), "a": a,
            "d": d_skip.astype(F32).reshape(1, SSM_WIDTH), "w_glu": w_glu.astype(BF),
            "b_glu": b_glu.astype(F32).reshape(1, SSM_WIDTH)}


def _attend_chunk(q, kc, vc, bias_ref, valid):
    first = lax.broadcasted_iota(jnp.int32, (1, LANES), 1) < ATT_HEAD_DIM
    outs = []
    for hh in range(2):
        qm = jnp.where(first if hh == 0 else jnp.logical_not(first), q, 0.0).astype(BF)
        s = _dot_nt(qm, kc) + bias_ref[hh]
        if valid is not None:
            s = jnp.where(valid, s, NEG_MASK)
        m = jnp.max(s, axis=-1, keepdims=True)
        p = jnp.exp(s - m)
        l = jnp.sum(p, axis=-1, keepdims=True)
        outs.append(_dot(p.astype(BF), vc) / l)
    return jnp.where(first, outs[0], outs[1])


def _attn_prompt_kernel(q_ref, k0_ref, k1_ref, k2_ref, v0_ref, v1_ref, v2_ref, bias_ref, o_ref, *, cpb):
    tb = pl.program_id(2)
    rows = cpb * CHUNK
    kwin = jnp.concatenate([k0_ref[...], k1_ref[...], k2_ref[...]], axis=0).astype(BF)
    vwin = jnp.concatenate([v0_ref[...], v1_ref[...], v2_ref[...]], axis=0).astype(BF)
    col = lax.broadcasted_iota(jnp.int32, (1, ATT_BAND), 1)
    for c in range(cpb):
        lo = 2 * rows - ATT_PAST + c * CHUNK
        q = q_ref[c * CHUNK:(c + 1) * CHUNK, :] * (ATT_HEAD_DIM ** -0.5)
        valid = (col + (tb * rows - ATT_PAST + c * CHUNK)) >= 0
        o_ref[c * CHUNK:(c + 1) * CHUNK, :] = _attend_chunk(
            q, kwin[lo:lo + ATT_BAND], vwin[lo:lo + ATT_BAND], bias_ref, valid)


def _attn_prompt(h, bias, bsz, t):
    cpb = 4
    rows = cpb * CHUNK
    tbn = t // rows
    qc, kc, vc = OFF_Q_ATT // LANES, OFF_K_ATT // LANES, OFF_V_ATT // LANES

    def past(col0, back):
        return pl.BlockSpec((rows, LANES), lambda b, hp, tb: (b * tbn + jnp.maximum(tb - back, 0), col0 + hp))

    return pl.pallas_call(
        functools.partial(_attn_prompt_kernel, cpb=cpb),
        out_shape=jax.ShapeDtypeStruct((bsz * t, ATT_WIDTH), F32),
        grid=(bsz, ATT_HEADS // 2, tbn),
        in_specs=[past(qc, 0), past(kc, 2), past(kc, 1), past(kc, 0), past(vc, 2), past(vc, 1), past(vc, 0),
                  pl.BlockSpec((2, CHUNK, ATT_BAND), lambda b, hp, tb: (hp, 0, 0))],
        out_specs=pl.BlockSpec((rows, LANES), lambda b, hp, tb: (b * tbn + tb, hp)),
        compiler_params=_params(("arbitrary", "arbitrary", "arbitrary")),
        name="attn_prompt",
    )(h, h, h, h, h, h, h, bias)


def _attn_sample_kernel(q_ref, k_ref, v_ref, ck_ref, cv_ref, bias_ref, o_ref):
    kwin = jnp.concatenate([ck_ref[0], k_ref[...]], axis=0).astype(BF)
    vwin = jnp.concatenate([cv_ref[0], v_ref[...]], axis=0).astype(BF)
    q = q_ref[...] * (ATT_HEAD_DIM ** -0.5)
    o_ref[...] = _attend_chunk(q, kwin, vwin, bias_ref, None)


def _attn_sample(h, cache_k, cache_v, bias, bsz):
    qc, kc, vc = OFF_Q_ATT // LANES, OFF_K_ATT // LANES, OFF_V_ATT // LANES
    new = lambda col0: pl.BlockSpec((CHUNK, LANES), lambda b, hp: (b, col0 + hp))
    old = pl.BlockSpec((1, ATT_PAST, LANES), lambda b, hp: (b, 0, hp))
    return pl.pallas_call(
        _attn_sample_kernel,
        out_shape=jax.ShapeDtypeStruct((bsz * CHUNK, ATT_WIDTH), F32),
        grid=(bsz, ATT_HEADS // 2),
        in_specs=[new(qc), new(kc), new(vc), old, old,
                  pl.BlockSpec((2, CHUNK, ATT_BAND), lambda b, hp: (hp, 0, 0))],
        out_specs=pl.BlockSpec((CHUNK, LANES), lambda b, hp: (b, hp)),
        compiler_params=_params(("arbitrary", "arbitrary")),
        name="attn_sample",
    )(h, h, h, cache_k, cache_v, bias)


def _rel_bias_tile(table):
    i = jnp.arange(CHUNK)
    j = jnp.arange(ATT_BAND)
    idx = jnp.clip(i[:, None] - j[None, :] + ATT_PAST, -REL_CLIP, REL_CLIP) + REL_CLIP
    return jnp.moveaxis(table[idx], -1, 0).astype(F32)


def _ret_kernel(q_ref, k_ref, v_ref, g_ref, cos_ref, sin_ref, intra_ref, qw_ref, kw_ref, dec_ref, bd_ref, gn_ref,
                s0_ref, y_ref, sf_ref, state, *, cpb):
    @pl.when(pl.program_id(1) == 0)
    def _():
        state[...] = s0_ref[0]

    lane = lax.broadcasted_iota(jnp.int32, (1, LANES), 1)
    first = lane < RET_DIM
    low_half = (lane % RET_DIM) < (RET_DIM // 2)

    for c in range(cpb):
        rows = slice(c * CHUNK, (c + 1) * CHUNK)
        cos = cos_ref[rows, :]
        sin = sin_ref[rows, :]

        def rope(x):
            swapped = jnp.where(low_half, pltpu.roll(x, LANES - RET_DIM // 2, 1), pltpu.roll(x, RET_DIM // 2, 1))
            return x * cos + swapped * sin

        for hp in range(RET_HEADS // 2):
            cols = slice(hp * LANES, (hp + 1) * LANES)
            q = rope(q_ref[rows, cols]) * (RET_DIM ** -0.5)
            k = rope(k_ref[rows, cols])
            kb = k.astype(BF)
            vb = v_ref[rows, cols].astype(BF)
            intra = []
            for hh in range(2):
                qm = jnp.where(first if hh == 0 else jnp.logical_not(first), q, 0.0).astype(BF)
                s = _dot_nt(qm, kb) * intra_ref[2 * hp + hh]
                intra.append(_dot(s.astype(BF), vb))
            s_prev = state[hp]
            o = jnp.where(first, intra[0], intra[1]) + _dot((q * qw_ref[hp]).astype(BF), s_prev.astype(BF))
            kv = _dot_tn((k * kw_ref[hp]).astype(BF), vb)
            state[hp] = dec_ref[hp] * s_prev + bd_ref[...] * kv
            sum_a = jnp.sum(jnp.where(first, o, 0.0), axis=-1, keepdims=True)
            sum_b = jnp.sum(o, axis=-1, keepdims=True) - sum_a
            d = o - jnp.where(first, sum_a, sum_b) * (1.0 / RET_DIM)
            sq = d * d
            var_a = jnp.sum(jnp.where(first, sq, 0.0), axis=-1, keepdims=True)
            var_b = jnp.sum(sq, axis=-1, keepdims=True) - var_a
            var = jnp.where(first, var_a, var_b) * (1.0 / RET_DIM)
            g = g_ref[rows, cols]
            y_ref[rows, cols] = d * lax.rsqrt(var + LN_EPS) * gn_ref[:, cols] * (g * jax.nn.sigmoid(g))
    sf_ref[0] = state[...]


def _retention(h, s0, rt, gn, bsz, t, cpb):
    rows = cpb * CHUNK
    tbn = t // rows
    width = RET_HEADS * RET_DIM
    col = lambda off: pl.BlockSpec((rows, width), lambda b, tb: (b * tbn + tb, off // width))
    tab = pl.BlockSpec((rows, LANES), lambda b, tb: (tb, 0))
    const3 = lambda shape: pl.BlockSpec(shape, lambda b, tb: (0, 0, 0))
    st = pl.BlockSpec((1, 2, LANES, LANES), lambda b, tb: (b, 0, 0, 0))
    return pl.pallas_call(
        functools.partial(_ret_kernel, cpb=cpb),
        out_shape=(jax.ShapeDtypeStruct((bsz * t, width), F32),
                   jax.ShapeDtypeStruct((bsz, 2, LANES, LANES), F32)),
        grid=(bsz, tbn),
        in_specs=[col(OFF_Q_RET), col(OFF_K_RET), col(OFF_V_RET), col(OFF_G_RET), tab, tab,
                  const3((RET_HEADS, CHUNK, CHUNK)), const3((2, CHUNK, LANES)), const3((2, CHUNK, LANES)),
                  const3((2, LANES, LANES)), pl.BlockSpec((LANES, LANES), lambda b, tb: (0, 0)),
                  pl.BlockSpec((1, width), lambda b, tb: (0, 0)), st],
        out_specs=(pl.BlockSpec((rows, width), lambda b, tb: (b * tbn + tb, 0)), st),
        scratch_shapes=[pltpu.VMEM((2, LANES, LANES), F32)],
        compiler_params=_params(("arbitrary", "arbitrary")),
        name="retention",
    )(h, h, h, h, rt["cos"], rt["sin"], rt["intra"], rt["qw"], rt["kw"], rt["dec"], rt["bd"], gn, s0)


def _retention_tables(pos):
    lg = jnp.log1p(-(2.0 ** (-5.0 - jnp.arange(RET_HEADS, dtype=F32))))
    i = jnp.arange(CHUNK, dtype=F32)
    intra = jnp.exp(lg[:, None, None] * jnp.abs(i[:, None] - i[None, :]))
    lane_head = jnp.arange(2 * LANES) // RET_DIM
    lg_lane = lg[lane_head]
    qw = jnp.exp(lg_lane[None, :] * (i[:, None] + 1.0))
    kw = jnp.exp(lg_lane[None, :] * (CHUNK - 1.0 - i)[:, None])
    pair = lambda m: jnp.stack([m[:, :LANES], m[:, LANES:]])
    same_head = (jnp.arange(LANES)[:, None] // RET_DIM) == (jnp.arange(LANES)[None, :] // RET_DIM)
    bd = same_head.astype(F32)
    dec_lane = jnp.exp(lg_lane * CHUNK)
    dec = jnp.stack([bd * dec_lane[:LANES][:, None], bd * dec_lane[LANES:][:, None]])
    half = RET_DIM // 2
    freqs = ROPE_BASE ** (-jnp.arange(half, dtype=F32) / half)
    ang = pos.astype(F32)[:, None] * freqs[None]
    cos = jnp.tile(jnp.cos(ang), (1, LANES // half))
    sin = jnp.tile(jnp.concatenate([-jnp.sin(ang), jnp.sin(ang)], axis=1), (1, LANES // RET_DIM))
    return {"intra": intra, "qw": pair(qw), "kw": pair(kw), "dec": dec, "bd": bd, "cos": cos, "sin": sin}


def _pair_states(s):
    b = s.shape[0]
    s = s.astype(F32).reshape(b, 2, 2, RET_DIM, RET_DIM)
    out = jnp.zeros((b, 2, 2, RET_DIM, 2, RET_DIM), F32)
    out = out.at[:, :, 0, :, 0, :].set(s[:, :, 0]).at[:, :, 1, :, 1, :].set(s[:, :, 1])
    return out.reshape(b, 2, LANES, LANES)


def _unpair_states(s):
    b = s.shape[0]
    s = s.reshape(b, 2, 2, RET_DIM, 2, RET_DIM)
    return jnp.stack([s[:, :, 0, :, 0, :], s[:, :, 1, :, 1, :]], axis=2).reshape(b, RET_HEADS, RET_DIM, RET_DIM)


def _merge_kernel(x_ref, g1_ref, g2_ref, g3_ref, ys_ref, ya_ref, yr_ref, ws_ref, wa_ref, wr_ref, wo_ref,
                  lg_ref, lb_ref, o_ref):
    merged = (jax.nn.sigmoid(g1_ref[...]) * _dot(ys_ref[...].astype(BF), ws_ref[...])
              + jax.nn.sigmoid(g2_ref[...]) * _dot(ya_ref[...].astype(BF), wa_ref[...])
              + jax.nn.sigmoid(g3_ref[...]) * _dot(yr_ref[...].astype(BF), wr_ref[...]))
    r = DN_ALPHA * x_ref[...] + _dot(merged.astype(BF), wo_ref[...])
    o_ref[...] = _layer_norm(r, lg_ref[...], lb_ref[...])


def _merge(x, h, y_ssm, y_att, y_ret, lw, tm):
    n = x.shape[0]
    row = lambda w: pl.BlockSpec((tm, w), lambda i: (i, 0))
    gate = lambda j: pl.BlockSpec((tm, D_MODEL), lambda i: (i, j))
    full = lambda a: pl.BlockSpec(a.shape, lambda i: (0, 0))
    ws = (lw["w_br_ssm"], lw["w_br_att"], lw["w_br_ret"], lw["w_o"], lw["ln1_g"], lw["ln1_b"])
    return pl.pallas_call(
        _merge_kernel,
        out_shape=jax.ShapeDtypeStruct((n, D_MODEL), F32),
        grid=(n // tm,),
        in_specs=[row(D_MODEL), gate(0), gate(1), gate(2), row(SSM_WIDTH), row(ATT_WIDTH),
                  row(RET_HEADS * RET_DIM)] + [full(a) for a in ws],
        out_specs=row(D_MODEL),
        compiler_params=_params(("arbitrary",)),
        name="merge",
    )(x, h, h, h, y_ssm, y_att, y_ret, *ws)


CAND_ROWS = 80


def _cand_flat_index():
    f = [b for b in range(16)]
    for a in range(1, 8):
        f += [a * 16 + b for b in range(8)]
    f += [a * 16 for a in range(8, 16)]
    return np.asarray(f, np.float32)


def _split_bf16(x):
    hi = x.astype(BF)
    return hi, (x - hi.astype(F32)).astype(BF)


def _top16_rows(s, key_idx):
    t = s.shape[1]
    row16 = lax.broadcasted_iota(jnp.int32, (PEER_TOPK, t), 0)
    work = s
    rank = jnp.full(s.shape, float(PEER_TOPK), F32)
    vals = jnp.zeros((PEER_TOPK, t), F32)
    for r in range(PEER_TOPK):
        m = jnp.max(work, axis=0, keepdims=True)
        idx = jnp.min(jnp.where(work == m, key_idx, float(N_KEYS)), axis=0, keepdims=True)
        sel = key_idx == idx
        rank = jnp.where(sel, float(r), rank)
        work = jnp.where(sel, -jnp.inf, work)
        vals = jnp.where(row16 == r, m, vals)
    return rank, vals


def _route_kernel(x_ref, wh_ref, wl_ref, kh_ref, kl_ref, fidx_ref, r2_ref, e2_ref, c1_ref, e1_ref):
    x = x_ref[...]
    xh, xl = _split_bf16(x)
    q = _dot(xh, wh_ref[...]) + (_dot(xl, wh_ref[...]) + _dot(xh, wl_ref[...]))
    t = x.shape[0]
    key_idx = lax.broadcasted_iota(jnp.int32, (N_KEYS, t), 0).astype(F32)
    fidx = fidx_ref[...]

    def scores(hs):
        qh, ql = _split_bf16(q[:, hs * PEER_HALF:(hs + 1) * PEER_HALF])
        return _dot_nt(kh_ref[hs], qh) + (_dot_nt(kl_ref[hs], qh) + _dot_nt(kh_ref[hs], ql))

    for h in range(PEER_HEADS):
        s1 = scores(2 * h)
        s2 = scores(2 * h + 1)
        rank1, v1 = _top16_rows(s1, key_idx)
        rank2, v2 = _top16_rows(s2, key_idx)
        blocks = [v1[0:1] + v2]
        for a in range(1, 8):
            blocks.append(v1[a:a + 1] + v2[0:8])
        blocks.append(v1[8:16] + v2[0:1])
        cand = jnp.concatenate(blocks, axis=0)
        work = cand
        chosen = jnp.zeros(cand.shape, F32)
        for _ in range(PEER_TOPK):
            m = jnp.max(work, axis=0, keepdims=True)
            idx = jnp.min(jnp.where(work == m, fidx, 1e9), axis=0, keepdims=True)
            sel = fidx == idx
            chosen = jnp.where(sel, 1.0, chosen)
            work = jnp.where(sel, -jnp.inf, work)
        counts = [jnp.sum(chosen[0:16], axis=0, keepdims=True)]
        for a in range(1, 8):
            counts.append(jnp.sum(chosen[8 + 8 * a:16 + 8 * a], axis=0, keepdims=True))
        for a in range(8, 16):
            counts.append(chosen[64 + a:65 + a])
        c1 = jnp.zeros(s1.shape, F32)
        for a in range(PEER_TOPK):
            c1 = jnp.where(rank1 == float(a), counts[a], c1)
        z = jnp.sum(chosen * jnp.exp(cand - cand[0:1]), axis=0, keepdims=True)
        r2_ref[h] = rank2.astype(BF)
        e2_ref[h] = jnp.exp(s2 - v2[0:1]).astype(BF)
        c1_ref[h] = c1
        e1_ref[h] = jnp.exp(s1 - v1[0:1]) / z


def _route(x1, lw, tb):
    n = x1.shape[0]
    fidx = jnp.broadcast_to(jnp.asarray(_cand_flat_index())[:, None], (CAND_ROWS, tb))
    out = jax.ShapeDtypeStruct((PEER_HEADS, N_KEYS, n), F32)
    ospec = pl.BlockSpec((PEER_HEADS, N_KEYS, tb), lambda i: (0, 0, i))
    out_bf = jax.ShapeDtypeStruct((PEER_HEADS, N_KEYS, n), BF)
    ospec_bf = ospec
    full2 = lambda a: pl.BlockSpec(a.shape, lambda i: (0, 0))
    full3 = lambda a: pl.BlockSpec(a.shape, lambda i: (0, 0, 0))
    return pl.pallas_call(
        _route_kernel,
        out_shape=(out_bf, out_bf, out, out),
        grid=(n // tb,),
        in_specs=[pl.BlockSpec((tb, D_MODEL), lambda i: (i, 0)), full2(lw["wq_hi"]), full2(lw["wq_lo"]),
                  full3(lw["keys_hi"]), full3(lw["keys_lo"]), full2(fidx)],
        out_specs=(ospec_bf, ospec_bf, ospec, ospec),
        compiler_params=_params(("arbitrary",)),
        name="peer_route",
    )(x1, lw["wq_hi"], lw["wq_lo"], lw["keys_hi"], lw["keys_lo"], fidx)


PEER_EXPERT_BLOCK = 1024


def _peer_kernel(x_ref, u_ref, vt_ref, r2_ref, e2_ref, c1_ref, e1_ref, lg_ref, lb_ref, o_ref, xb, acc, wt, rowbuf):
    e = pl.program_id(1)
    tb = x_ref.shape[0]

    @pl.when(e == 0)
    def _():
        xb[...] = x_ref[...].astype(BF)
        acc[...] = jnp.zeros_like(acc)

    wt[...] = jax.nn.gelu(_dot_nt(u_ref[...], xb[...])).astype(BF)
    keys_per_step = PEER_EXPERT_BLOCK // N_KEYS
    first_key = pl.multiple_of(e * keys_per_step, keys_per_step)

    for h in range(PEER_HEADS):
        for src, ref in enumerate((c1_ref, e1_ref)):
            blk = ref[h, pl.ds(first_key, keys_per_step), :]
            for j in range(keys_per_step):
                rowbuf[src, h, j] = jnp.broadcast_to(blk[j:j + 1], (SUBLANES, tb)).astype(BF)

    half = N_KEYS // 2
    group = keys_per_step // 2
    tiles = half // SUBLANES

    def spread(row):
        return jnp.broadcast_to(row[None], (tiles, SUBLANES, BF_CHUNK)).reshape(half, BF_CHUNK)

    for tc in range(tb // BF_CHUNK):
        cols = slice(tc * BF_CHUNK, (tc + 1) * BF_CHUNK)
        for jg in range(keys_per_step // group):
            for kh in range(2):
                keys = slice(kh * half, (kh + 1) * half)
                w = [jnp.zeros((half, BF_CHUNK), BF) for _ in range(group)]
                for h in range(PEER_HEADS):
                    r2 = r2_ref[h, keys, cols]
                    e2 = e2_ref[h, keys, cols]
                    for jj in range(group):
                        j = jg * group + jj
                        c1 = spread(rowbuf[0, h, j, :, cols])
                        e1 = spread(rowbuf[1, h, j, :, cols])
                        w[jj] = w[jj] + jnp.where(r2 < c1, e2 * e1, 0)
                for jj in range(group):
                    rows = slice((jg * group + jj) * N_KEYS + kh * half, (jg * group + jj) * N_KEYS + (kh + 1) * half)
                    wt[rows, cols] = w[jj] * wt[rows, cols]
    acc[...] += _dot(vt_ref[...], wt[...])

    @pl.when(e == pl.num_programs(1) - 1)
    def _():
        r = DN_ALPHA * x_ref[...] + acc[...].T
        o_ref[...] = _layer_norm(r, lg_ref[...], lb_ref[...])


def _peer(x1, route, lw, tb):
    n = x1.shape[0]
    eb = PEER_EXPERT_BLOCK
    rspec = pl.BlockSpec((PEER_HEADS, N_KEYS, tb), lambda i, e: (0, 0, i))
    rspec_bf = rspec
    vec = pl.BlockSpec((1, D_MODEL), lambda i, e: (0, 0))
    return pl.pallas_call(
        _peer_kernel,
        out_shape=jax.ShapeDtypeStruct((n, D_MODEL), F32),
        grid=(n // tb, N_EXPERTS // eb),
        in_specs=[pl.BlockSpec((tb, D_MODEL), lambda i, e: (i, 0)),
                  pl.BlockSpec((eb, D_MODEL), lambda i, e: (e, 0)),
                  pl.BlockSpec((D_MODEL, eb), lambda i, e: (0, e)),
                  rspec_bf, rspec_bf, rspec, rspec, vec, vec],
        out_specs=pl.BlockSpec((tb, D_MODEL), lambda i, e: (i, 0)),
        scratch_shapes=[pltpu.VMEM((tb, D_MODEL), BF), pltpu.VMEM((D_MODEL, tb), F32), pltpu.VMEM((eb, tb), BF),
                        pltpu.VMEM((2, PEER_HEADS, eb // N_KEYS, SUBLANES, tb), BF)],
        compiler_params=_params(("arbitrary", "arbitrary")),
        name="peer_experts",
    )(x1, lw["peer_u"], lw["peer_vt"], *route, lw["ln2_g"], lw["ln2_b"])


def _ple_kernel(x_ref, pe_ref, wg_ref, wp_ref, lg_ref, lb_ref, o_ref):
    x = x_ref[...]
    ple = jax.nn.sigmoid(_dot(x.astype(BF), wg_ref[...])) * _dot(pe_ref[...].astype(BF), wp_ref[...])
    o_ref[...] = _layer_norm(DN_ALPHA * x + ple, lg_ref[...], lb_ref[...])


def _ple(x2, pe, lw, tm):
    n = x2.shape[0]
    full = lambda a: pl.BlockSpec(a.shape, lambda i: (0, 0))
    ws = (lw["ple_w_g"], lw["ple_w_p"], lw["ln3_g"], lw["ln3_b"])
    return pl.pallas_call(
        _ple_kernel,
        out_shape=jax.ShapeDtypeStruct((n, D_MODEL), F32),
        grid=(n // tm,),
        in_specs=[pl.BlockSpec((tm, D_MODEL), lambda i: (i, 0)), pl.BlockSpec((tm, PLE_DIM), lambda i: (i, 0))]
        + [full(a) for a in ws],
        out_specs=pl.BlockSpec((tm, D_MODEL), lambda i: (i, 0)),
        compiler_params=_params(("arbitrary",)),
        name="ple",
    )(x2, pe, *ws)


def _time_major(u, bsz, t, bp):
    u = jnp.transpose(u.reshape(bsz, t, -1), (1, 0, 2))
    return jnp.pad(u, ((0, 0), (0, bp - bsz), (0, 0))).reshape(t * bp, -1)


def _batch_major(y, bsz, t, bp):
    return jnp.transpose(y.reshape(t, bp, -1)[:, :bsz], (1, 0, 2)).reshape(bsz * t, -1)


def _layer(x, pe, h0_re, h0_im, cache_k, cache_v, s0, pos, lw, bsz, t, prompt, n_rows):
    n = bsz * t
    h = _in_projection(x, lw["w_in"], 512, IN_COLS // 2)

    bp = max(8, bsz)
    u_tb = _time_major(h[:, OFF_U_SSM:OFF_U_SSM + SSM_WIDTH], bsz, t, bp)
    h0 = jnp.concatenate([h0_re.reshape(bsz, SSM_FLAT), h0_im.reshape(bsz, SSM_FLAT)], axis=1).astype(F32)
    h0 = jnp.pad(h0, ((0, bp - bsz), (0, 0)))
    y_tb, h_last = _s5_mixer(u_tb, h0, lw["s5"], t, bp, CHUNK)
    y_ssm = _batch_major(y_tb, bsz, t, bp)
    new_re = h_last[:bsz, :SSM_FLAT].reshape(bsz, SSM_GROUPS, SSM_STATE)
    new_im = h_last[:bsz, SSM_FLAT:].reshape(bsz, SSM_GROUPS, SSM_STATE)

    k_att = h[:, OFF_K_ATT:OFF_K_ATT + ATT_WIDTH].reshape(bsz, t, ATT_HEADS, ATT_HEAD_DIM)
    v_att = h[:, OFF_V_ATT:OFF_V_ATT + ATT_WIDTH].reshape(bsz, t, ATT_HEADS, ATT_HEAD_DIM)
    if prompt:
        y_att = _attn_prompt(h, lw["rel_bias"], bsz, t)
        new_k, new_v = k_att[:, -n_rows:], v_att[:, -n_rows:]
    else:
        r = cache_k.shape[1]
        y_att = _attn_sample(h, cache_k.reshape(bsz, r, ATT_WIDTH), cache_v.reshape(bsz, r, ATT_WIDTH),
                             lw["rel_bias"], bsz)
        new_k, new_v = k_att, v_att

    y_ret, s_pair = _retention(h, _pair_states(s0), _retention_tables(pos), lw["ret_gn_g"], bsz, t,
                               4 if prompt else 1)
    s_fin = _unpair_states(s_pair)

    tok = 512 if n % 512 == 0 else 256
    x1 = _merge(x, h, y_ssm, y_att, y_ret, lw, 256)
    route = _route(x1, lw, 256)
    x2 = _peer(x1, route, lw, tok)
    x3 = _ple(x2, pe, lw, 256)
    return x3, (new_re, new_im, new_k, new_v, s_fin)


def _layer_weights(li, w_in, ssm_lam_re, ssm_lam_im, ssm_b_re, ssm_b_im, ssm_c_re, ssm_c_im, ssm_log_dt, ssm_d,
                   ssm_w_glu, ssm_b_glu, att_rel_bias, ret_gn_g, w_br_ssm, w_br_att, w_br_ret, w_o, ln1_g, ln1_b,
                   peer_w_q, peer_sub_keys, peer_u, peer_v, ln2_g, ln2_b, ple_w_g, ple_w_p, ln3_g, ln3_b):
    vec = lambda a: a[li].astype(F32).reshape(1, -1)
    w = w_in[li]
    wq_hi, wq_lo = _split_bf16(peer_w_q[li].astype(F32))
    keys_hi, keys_lo = _split_bf16(peer_sub_keys[li].astype(F32).reshape(2 * PEER_HEADS, N_KEYS, PEER_HALF))
    return {
        "w_in": jnp.concatenate([w[:, OLD_GATE_START:], w[:, :OLD_GATE_START]], axis=1).astype(BF),
        "s5": _s5_tables(ssm_lam_re[li], ssm_lam_im[li], ssm_b_re[li], ssm_b_im[li], ssm_c_re[li], ssm_c_im[li],
                         ssm_log_dt[li], ssm_d[li], ssm_w_glu[li], ssm_b_glu[li]),
        "rel_bias": _rel_bias_tile(att_rel_bias[li]),
        "ret_gn_g": vec(ret_gn_g),
        "w_br_ssm": w_br_ssm[li].astype(BF), "w_br_att": w_br_att[li].astype(BF), "w_br_ret": w_br_ret[li].astype(BF),
        "w_o": w_o[li].astype(BF), "ln1_g": vec(ln1_g), "ln1_b": vec(ln1_b),
        "wq_hi": wq_hi, "wq_lo": wq_lo, "keys_hi": keys_hi, "keys_lo": keys_lo,
        "peer_u": peer_u[li].astype(BF), "peer_vt": peer_v[li].astype(BF).T,
        "ln2_g": vec(ln2_g), "ln2_b": vec(ln2_b),
        "ple_w_g": ple_w_g[li].astype(BF), "ple_w_p": ple_w_p[li].astype(BF), "ln3_g": vec(ln3_g), "ln3_b": vec(ln3_b),
    }


def kernel(x_prompt, x_sample, p_prompt, p_sample, state_ssm_re, state_ssm_im, cache_attn_k, cache_attn_v, state_ret, w_in, ssm_lam_re, ssm_lam_im, ssm_b_re, ssm_b_im, ssm_c_re, ssm_c_im, ssm_log_dt, ssm_d, ssm_w_glu, ssm_b_glu, att_rel_bias, ret_gn_g, w_br_ssm, w_br_att, w_br_ret, w_o, ln1_g, ln1_b, peer_w_q, peer_sub_keys, peer_u, peer_v, ln2_g, ln2_b, ple_w_g, ple_w_p, ln3_g, ln3_b):
    bp, tp, _ = x_prompt.shape
    bs, ts, _ = x_sample.shape
    n_rows = cache_attn_k.shape[2]
    pos_p = jnp.arange(tp)
    pos_s = PAST_LEN + jnp.arange(ts)
    y_p = x_prompt.reshape(bp * tp, D_MODEL).astype(F32)
    y_s = x_sample.reshape(bs * ts, D_MODEL).astype(F32)
    new_p, new_s = [], []
    for li in range(DEPTH):
        lw = _layer_weights(li, w_in, ssm_lam_re, ssm_lam_im, ssm_b_re, ssm_b_im, ssm_c_re, ssm_c_im, ssm_log_dt,
                            ssm_d, ssm_w_glu, ssm_b_glu, att_rel_bias, ret_gn_g, w_br_ssm, w_br_att, w_br_ret, w_o,
                            ln1_g, ln1_b, peer_w_q, peer_sub_keys, peer_u, peer_v, ln2_g, ln2_b, ple_w_g, ple_w_p,
                            ln3_g, ln3_b)
        zero_ssm = jnp.zeros((bp, SSM_GROUPS, SSM_STATE), F32)
        zero_ret = jnp.zeros((bp, RET_HEADS, RET_DIM, RET_DIM), F32)
        y_p, st = _layer(y_p, p_prompt[li].reshape(bp * tp, PLE_DIM), zero_ssm, zero_ssm, None, None, zero_ret,
                         pos_p, lw, bp, tp, True, n_rows)
        new_p.append(st)
        y_s, st = _layer(y_s, p_sample[li].reshape(bs * ts, PLE_DIM), state_ssm_re[li], state_ssm_im[li],
                         cache_attn_k[li], cache_attn_v[li], state_ret[li], pos_s, lw, bs, ts, False, n_rows)
        new_s.append(st)
    stack = lambda states, k: jnp.stack([s[k] for s in states])
    return (y_p.reshape(bp, tp, D_MODEL), y_s.reshape(bs, ts, D_MODEL),
            stack(new_p, 0), stack(new_p, 1), stack(new_p, 2), stack(new_p, 3), stack(new_p, 4),
            stack(new_s, 0), stack(new_s, 1), stack(new_s, 2), stack(new_s, 3), stack(new_s, 4))
```

```python
import functools

import jax
import jax.numpy as jnp
import numpy as np
from jax import lax
from jax.experimental import pallas as pl
from jax.experimental.pallas import tpu as pltpu

F32 = jnp.float32
BF = jnp.bfloat16

D_MODEL = 1024
DEPTH = 2
PAST_LEN = 2048
CHUNK = 64
PLE_DIM = 256
SSM_WIDTH = 256
SSM_GROUP = 16
SSM_GROUPS = 16
SSM_STATE = 64
SSM_FLAT = SSM_GROUPS * SSM_STATE
ATT_HEADS = 8
ATT_HEAD_DIM = 64
ATT_WIDTH = 512
ATT_PAST = 8 * CHUNK
ATT_BAND = ATT_PAST + CHUNK
REL_CLIP = 256
RET_HEADS = 4
RET_DIM = 64
ROPE_BASE = 10000.0
PEER_HEADS = 8
PEER_HALF = 128
N_KEYS = 128
N_EXPERTS = N_KEYS * N_KEYS
PEER_TOPK = 16
SUBLANES = 8
BF_CHUNK = 256
DN_ALPHA = (2 * DEPTH) ** 0.25
LN_EPS = 1e-5
LANES = 128
NEG_MASK = float(jnp.finfo(jnp.float32).min)

IN_COLS = 5888
OFF_GATES = 0
OFF_U_SSM = 3072
OFF_Q_ATT = 3328
OFF_K_ATT = 3840
OFF_V_ATT = 4352
OFF_Q_RET = 4864
OFF_K_RET = 5120
OFF_V_RET = 5376
OFF_G_RET = 5632
OLD_GATE_START = 2816

VMEM_LIMIT = 48 * 1024 * 1024


def _params(sem):
    return pltpu.CompilerParams(dimension_semantics=sem, vmem_limit_bytes=VMEM_LIMIT)


def _dot(a, b):
    return jnp.dot(a, b, preferred_element_type=F32)


def _dot_nt(a, b):
    return lax.dot_general(a, b, (((1,), (1,)), ((), ())), preferred_element_type=F32)


def _dot_tn(a, b):
    return lax.dot_general(a, b, (((0,), (0,)), ((), ())), preferred_element_type=F32)


def _layer_norm(r, g, b):
    mu = jnp.mean(r, axis=-1, keepdims=True)
    d = r - mu
    var = jnp.mean(d * d, axis=-1, keepdims=True)
    return d * lax.rsqrt(var + LN_EPS) * g + b


def _matmul_kernel(x_ref, w_ref, o_ref):
    o_ref[...] = _dot(x_ref[...].astype(BF), w_ref[...])


def _in_projection(x, w, tm, tn):
    m, k = x.shape
    n = w.shape[1]
    return pl.pallas_call(
        _matmul_kernel,
        out_shape=jax.ShapeDtypeStruct((m, n), F32),
        grid=(n // tn, m // tm),
        in_specs=[pl.BlockSpec((tm, k), lambda j, i: (i, 0)),
                  pl.BlockSpec((k, tn), lambda j, i: (0, j))],
        out_specs=pl.BlockSpec((tm, tn), lambda j, i: (i, j)),
        compiler_params=_params(("arbitrary", "arbitrary")),
        name="in_projection",
    )(x, w)


def _s5_kernel(u_ref, bm_ref, a_ref, cm_ref, d_ref, wg_ref, bg_ref, h0_ref, y_ref, hl_ref, hbuf, hst, *, steps, bp):
    @pl.when(pl.program_id(0) == 0)
    def _():
        hst[...] = h0_ref[...]

    u = u_ref[...]
    hbuf[...] = _dot(u.astype(BF), bm_ref[...])
    a_re = jnp.broadcast_to(a_ref[0:1, :], (bp, SSM_FLAT))
    a_im = jnp.broadcast_to(a_ref[1:2, :], (bp, SSM_FLAT))

    def step(t, carry):
        h_re, h_im = carry
        r0 = pl.multiple_of(t * bp, bp)
        n_re = a_re * h_re - a_im * h_im + hbuf[pl.ds(r0, bp), 0:SSM_FLAT]
        n_im = a_re * h_im + a_im * h_re + hbuf[pl.ds(r0, bp), SSM_FLAT:2 * SSM_FLAT]
        hbuf[pl.ds(r0, bp), 0:SSM_FLAT] = n_re
        hbuf[pl.ds(r0, bp), SSM_FLAT:2 * SSM_FLAT] = n_im
        return n_re, n_im

    h_re, h_im = lax.fori_loop(0, steps, step, (hst[:, 0:SSM_FLAT], hst[:, SSM_FLAT:2 * SSM_FLAT]))
    hst[:, 0:SSM_FLAT] = h_re
    hst[:, SSM_FLAT:2 * SSM_FLAT] = h_im
    hl_ref[...] = hst[...]
    y = _dot(hbuf[...].astype(BF), cm_ref[...]) + d_ref[...] * u
    z = jax.nn.gelu(y)
    y_ref[...] = z * jax.nn.sigmoid(_dot(z.astype(BF), wg_ref[...]) + bg_ref[...])


def _s5_mixer(u_tb, h0, sp, n_steps, bp, steps):
    rows = steps * bp
    kern = functools.partial(_s5_kernel, steps=steps, bp=bp)
    const = lambda c: (0, 0)
    return pl.pallas_call(
        kern,
        out_shape=(jax.ShapeDtypeStruct((n_steps * bp, SSM_WIDTH), F32),
                   jax.ShapeDtypeStruct((bp, 2 * SSM_FLAT), F32)),
        grid=(n_steps // steps,),
        in_specs=[pl.BlockSpec((rows, SSM_WIDTH), lambda c: (c, 0)),
                  pl.BlockSpec((SSM_WIDTH, 2 * SSM_FLAT), const),
                  pl.BlockSpec((2, SSM_FLAT), const),
                  pl.BlockSpec((2 * SSM_FLAT, SSM_WIDTH), const),
                  pl.BlockSpec((1, SSM_WIDTH), const),
                  pl.BlockSpec((SSM_WIDTH, SSM_WIDTH), const),
                  pl.BlockSpec((1, SSM_WIDTH), const),
                  pl.BlockSpec((bp, 2 * SSM_FLAT), const)],
        out_specs=(pl.BlockSpec((rows, SSM_WIDTH), lambda c: (c, 0)),
                   pl.BlockSpec((bp, 2 * SSM_FLAT), const)),
        scratch_shapes=[pltpu.VMEM((rows, 2 * SSM_FLAT), F32), pltpu.VMEM((bp, 2 * SSM_FLAT), F32)],
        compiler_params=_params(("arbitrary",)),
        name="s5_mixer",
    )(u_tb, sp["bmat"], sp["a"], sp["cmat"], sp["d"], sp["w_glu"], sp["b_glu"], h0)


def _s5_tables(lam_re, lam_im, b_re, b_im, c_re, c_im, log_dt, d_skip, w_glu, b_glu):
    lr, li = lam_re.astype(F32), lam_im.astype(F32)
    dt = jnp.exp(log_dt.astype(F32))[:, None]
    mag = jnp.exp(lr * dt)
    a_re, a_im = mag * jnp.cos(li * dt), mag * jnp.sin(li * dt)
    den = lr * lr + li * li
    k_re = ((a_re - 1.0) * lr + a_im * li) / den
    k_im = (a_im * lr - (a_re - 1.0) * li) / den
    br, bi = b_re.astype(F32), b_im.astype(F32)
    bbar_re = k_re[..., None] * br - k_im[..., None] * bi
    bbar_im = k_re[..., None] * bi + k_im[..., None] * br
    eye = jnp.eye(SSM_GROUPS, dtype=F32)

    def in_blockdiag(m):
        m = jnp.transpose(m, (0, 2, 1))
        return (m[:, :, None, :] * eye[:, None, :, None]).reshape(SSM_WIDTH, SSM_FLAT)

    def out_blockdiag(m):
        m = jnp.transpose(m, (0, 2, 1))
        return (m[:, :, None, :] * eye[:, None, :, None]).reshape(SSM_FLAT, SSM_WIDTH)

    bmat = jnp.concatenate([in_blockdiag(bbar_re), in_blockdiag(bbar_im)], axis=1)
    cmat = jnp.concatenate([out_blockdiag(c_re.astype(F32)), -out_blockdiag(c_im.astype(F32))], axis=0)
    a = jnp.stack([a_re.reshape(SSM_FLAT), a_im.reshape(SSM_FLAT)])
    return {"bmat": bmat.astype(BF), "cmat": cmat.astype(BF), "a": a,
            "d": d_skip.astype(F32).reshape(1, SSM_WIDTH), "w_glu": w_glu.astype(BF),
            "b_glu": b_glu.astype(F32).reshape(1, SSM_WIDTH)}


def _attend_chunk(q, kc, vc, bias_ref, valid):
    first = lax.broadcasted_iota(jnp.int32, (1, LANES), 1) < ATT_HEAD_DIM
    outs = []
    for hh in range(2):
        qm = jnp.where(first if hh == 0 else jnp.logical_not(first), q, 0.0).astype(BF)
        s = _dot_nt(qm, kc) + bias_ref[hh]
        if valid is not None:
            s = jnp.where(valid, s, NEG_MASK)
        m = jnp.max(s, axis=-1, keepdims=True)
        p = jnp.exp(s - m)
        l = jnp.sum(p, axis=-1, keepdims=True)
        outs.append(_dot(p.astype(BF), vc) / l)
    return jnp.where(first, outs[0], outs[1])


def _attn_prompt_kernel(q_ref, k0_ref, k1_ref, k2_ref, v0_ref, v1_ref, v2_ref, bias_ref, o_ref, *, cpb):
    tb = pl.program_id(2)
    rows = cpb * CHUNK
    kwin = jnp.concatenate([k0_ref[...], k1_ref[...], k2_ref[...]], axis=0).astype(BF)
    vwin = jnp.concatenate([v0_ref[...], v1_ref[...], v2_ref[...]], axis=0).astype(BF)
    col = lax.broadcasted_iota(jnp.int32, (1, ATT_BAND), 1)
    for c in range(cpb):
        lo = 2 * rows - ATT_PAST + c * CHUNK
        q = q_ref[c * CHUNK:(c + 1) * CHUNK, :] * (ATT_HEAD_DIM ** -0.5)
        valid = (col + (tb * rows - ATT_PAST + c * CHUNK)) >= 0
        o_ref[c * CHUNK:(c + 1) * CHUNK, :] = _attend_chunk(
            q, kwin[lo:lo + ATT_BAND], vwin[lo:lo + ATT_BAND], bias_ref, valid)


def _attn_prompt(h, bias, bsz, t):
    cpb = 4
    rows = cpb * CHUNK
    tbn = t // rows
    qc, kc, vc = OFF_Q_ATT // LANES, OFF_K_ATT // LANES, OFF_V_ATT // LANES

    def past(col0, back):
        return pl.BlockSpec((rows, LANES), lambda b, hp, tb: (b * tbn + jnp.maximum(tb - back, 0), col0 + hp))

    return pl.pallas_call(
        functools.partial(_attn_prompt_kernel, cpb=cpb),
        out_shape=jax.ShapeDtypeStruct((bsz * t, ATT_WIDTH), F32),
        grid=(bsz, ATT_HEADS // 2, tbn),
        in_specs=[past(qc, 0), past(kc, 2), past(kc, 1), past(kc, 0), past(vc, 2), past(vc, 1), past(vc, 0),
                  pl.BlockSpec((2, CHUNK, ATT_BAND), lambda b, hp, tb: (hp, 0, 0))],
        out_specs=pl.BlockSpec((rows, LANES), lambda b, hp, tb: (b * tbn + tb, hp)),
        compiler_params=_params(("arbitrary", "arbitrary", "arbitrary")),
        name="attn_prompt",
    )(h, h, h, h, h, h, h, bias)


def _attn_sample_kernel(q_ref, k_ref, v_ref, ck_ref, cv_ref, bias_ref, o_ref):
    kwin = jnp.concatenate([ck_ref[0], k_ref[...]], axis=0).astype(BF)
    vwin = jnp.concatenate([cv_ref[0], v_ref[...]], axis=0).astype(BF)
    q = q_ref[...] * (ATT_HEAD_DIM ** -0.5)
    o_ref[...] = _attend_chunk(q, kwin, vwin, bias_ref, None)


def _attn_sample(h, cache_k, cache_v, bias, bsz):
    qc, kc, vc = OFF_Q_ATT // LANES, OFF_K_ATT // LANES, OFF_V_ATT // LANES
    new = lambda col0: pl.BlockSpec((CHUNK, LANES), lambda b, hp: (b, col0 + hp))
    old = pl.BlockSpec((1, ATT_PAST, LANES), lambda b, hp: (b, 0, hp))
    return pl.pallas_call(
        _attn_sample_kernel,
        out_shape=jax.ShapeDtypeStruct((bsz * CHUNK, ATT_WIDTH), F32),
        grid=(bsz, ATT_HEADS // 2),
        in_specs=[new(qc), new(kc), new(vc), old, old,
                  pl.BlockSpec((2, CHUNK, ATT_BAND), lambda b, hp: (hp, 0, 0))],
        out_specs=pl.BlockSpec((CHUNK, LANES), lambda b, hp: (b, hp)),
        compiler_params=_params(("arbitrary", "arbitrary")),
        name="attn_sample",
    )(h, h, h, cache_k, cache_v, bias)


def _rel_bias_tile(table):
    i = jnp.arange(CHUNK)
    j = jnp.arange(ATT_BAND)
    idx = jnp.clip(i[:, None] - j[None, :] + ATT_PAST, -REL_CLIP, REL_CLIP) + REL_CLIP
    return jnp.moveaxis(table[idx], -1, 0).astype(F32)


def _ret_kernel(q_ref, k_ref, v_ref, g_ref, cos_ref, sin_ref, intra_ref, qw_ref, kw_ref, dec_ref, bd_ref, gn_ref,
                s0_ref, y_ref, sf_ref, state, *, cpb):
    @pl.when(pl.program_id(1) == 0)
    def _():
        state[...] = s0_ref[0]

    lane = lax.broadcasted_iota(jnp.int32, (1, LANES), 1)
    first = lane < RET_DIM
    low_half = (lane % RET_DIM) < (RET_DIM // 2)

    for c in range(cpb):
        rows = slice(c * CHUNK, (c + 1) * CHUNK)
        cos = cos_ref[rows, :]
        sin = sin_ref[rows, :]

        def rope(x):
            swapped = jnp.where(low_half, pltpu.roll(x, LANES - RET_DIM // 2, 1), pltpu.roll(x, RET_DIM // 2, 1))
            return x * cos + swapped * sin

        for hp in range(RET_HEADS // 2):
            cols = slice(hp * LANES, (hp + 1) * LANES)
            q = rope(q_ref[rows, cols]) * (RET_DIM ** -0.5)
            k = rope(k_ref[rows, cols])
            kb = k.astype(BF)
            vb = v_ref[rows, cols].astype(BF)
            intra = []
            for hh in range(2):
                qm = jnp.where(first if hh == 0 else jnp.logical_not(first), q, 0.0).astype(BF)
                s = _dot_nt(qm, kb) * intra_ref[2 * hp + hh]
                intra.append(_dot(s.astype(BF), vb))
            s_prev = state[hp]
            o = jnp.where(first, intra[0], intra[1]) + _dot((q * qw_ref[hp]).astype(BF), s_prev.astype(BF))
            kv = _dot_tn((k * kw_ref[hp]).astype(BF), vb)
            state[hp] = dec_ref[hp] * s_prev + bd_ref[...] * kv
            sum_a = jnp.sum(jnp.where(first, o, 0.0), axis=-1, keepdims=True)
            sum_b = jnp.sum(o, axis=-1, keepdims=True) - sum_a
            d = o - jnp.where(first, sum_a, sum_b) * (1.0 / RET_DIM)
            sq = d * d
            var_a = jnp.sum(jnp.where(first, sq, 0.0), axis=-1, keepdims=True)
            var_b = jnp.sum(sq, axis=-1, keepdims=True) - var_a
            var = jnp.where(first, var_a, var_b) * (1.0 / RET_DIM)
            g = g_ref[rows, cols]
            y_ref[rows, cols] = d * lax.rsqrt(var + LN_EPS) * gn_ref[:, cols] * (g * jax.nn.sigmoid(g))
    sf_ref[0] = state[...]


def _retention(h, s0, rt, gn, bsz, t, cpb):
    rows = cpb * CHUNK
    tbn = t // rows
    width = RET_HEADS * RET_DIM
    col = lambda off: pl.BlockSpec((rows, width), lambda b, tb: (b * tbn + tb, off // width))
    tab = pl.BlockSpec((rows, LANES), lambda b, tb: (tb, 0))
    const3 = lambda shape: pl.BlockSpec(shape, lambda b, tb: (0, 0, 0))
    st = pl.BlockSpec((1, 2, LANES, LANES), lambda b, tb: (b, 0, 0, 0))
    return pl.pallas_call(
        functools.partial(_ret_kernel, cpb=cpb),
        out_shape=(jax.ShapeDtypeStruct((bsz * t, width), F32),
                   jax.ShapeDtypeStruct((bsz, 2, LANES, LANES), F32)),
        grid=(bsz, tbn),
        in_specs=[col(OFF_Q_RET), col(OFF_K_RET), col(OFF_V_RET), col(OFF_G_RET), tab, tab,
                  const3((RET_HEADS, CHUNK, CHUNK)), const3((2, CHUNK, LANES)), const3((2, CHUNK, LANES)),
                  const3((2, LANES, LANES)), pl.BlockSpec((LANES, LANES), lambda b, tb: (0, 0)),
                  pl.BlockSpec((1, width), lambda b, tb: (0, 0)), st],
        out_specs=(pl.BlockSpec((rows, width), lambda b, tb: (b * tbn + tb, 0)), st),
        scratch_shapes=[pltpu.VMEM((2, LANES, LANES), F32)],
        compiler_params=_params(("arbitrary", "arbitrary")),
        name="retention",
    )(h, h, h, h, rt["cos"], rt["sin"], rt["intra"], rt["qw"], rt["kw"], rt["dec"], rt["bd"], gn, s0)


def _retention_tables(pos):
    lg = jnp.log1p(-(2.0 ** (-5.0 - jnp.arange(RET_HEADS, dtype=F32))))
    i = jnp.arange(CHUNK, dtype=F32)
    intra = jnp.exp(lg[:, None, None] * jnp.abs(i[:, None] - i[None, :]))
    lane_head = jnp.arange(2 * LANES) // RET_DIM
    lg_lane = lg[lane_head]
    qw = jnp.exp(lg_lane[None, :] * (i[:, None] + 1.0))
    kw = jnp.exp(lg_lane[None, :] * (CHUNK - 1.0 - i)[:, None])
    pair = lambda m: jnp.stack([m[:, :LANES], m[:, LANES:]])
    same_head = (jnp.arange(LANES)[:, None] // RET_DIM) == (jnp.arange(LANES)[None, :] // RET_DIM)
    bd = same_head.astype(F32)
    dec_lane = jnp.exp(lg_lane * CHUNK)
    dec = jnp.stack([bd * dec_lane[:LANES][:, None], bd * dec_lane[LANES:][:, None]])
    half = RET_DIM // 2
    freqs = ROPE_BASE ** (-jnp.arange(half, dtype=F32) / half)
    ang = pos.astype(F32)[:, None] * freqs[None]
    cos = jnp.tile(jnp.cos(ang), (1, LANES // half))
    sin = jnp.tile(jnp.concatenate([-jnp.sin(ang), jnp.sin(ang)], axis=1), (1, LANES // RET_DIM))
    return {"intra": intra, "qw": pair(qw), "kw": pair(kw), "dec": dec, "bd": bd, "cos": cos, "sin": sin}


def _pair_states(s):
    b = s.shape[0]
    s = s.astype(F32).reshape(b, 2, 2, RET_DIM, RET_DIM)
    out = jnp.zeros((b, 2, 2, RET_DIM, 2, RET_DIM), F32)
    out = out.at[:, :, 0, :, 0, :].set(s[:, :, 0]).at[:, :, 1, :, 1, :].set(s[:, :, 1])
    return out.reshape(b, 2, LANES, LANES)


def _unpair_states(s):
    b = s.shape[0]
    s = s.reshape(b, 2, 2, RET_DIM, 2, RET_DIM)
    return jnp.stack([s[:, :, 0, :, 0, :], s[:, :, 1, :, 1, :]], axis=2).reshape(b, RET_HEADS, RET_DIM, RET_DIM)


def _merge_kernel(x_ref, g1_ref, g2_ref, g3_ref, ys_ref, ya_ref, yr_ref, ws_ref, wa_ref, wr_ref, wo_ref,
                  lg_ref, lb_ref, o_ref):
    merged = (jax.nn.sigmoid(g1_ref[...]) * _dot(ys_ref[...].astype(BF), ws_ref[...])
              + jax.nn.sigmoid(g2_ref[...]) * _dot(ya_ref[...].astype(BF), wa_ref[...])
              + jax.nn.sigmoid(g3_ref[...]) * _dot(yr_ref[...].astype(BF), wr_ref[...]))
    r = DN_ALPHA * x_ref[...] + _dot(merged.astype(BF), wo_ref[...])
    o_ref[...] = _layer_norm(r, lg_ref[...], lb_ref[...])


def _merge(x, h, y_ssm, y_att, y_ret, lw, tm):
    n = x.shape[0]
    row = lambda w: pl.BlockSpec((tm, w), lambda i: (i, 0))
    gate = lambda j: pl.BlockSpec((tm, D_MODEL), lambda i: (i, j))
    full = lambda a: pl.BlockSpec(a.shape, lambda i: (0, 0))
    ws = (lw["w_br_ssm"], lw["w_br_att"], lw["w_br_ret"], lw["w_o"], lw["ln1_g"], lw["ln1_b"])
    return pl.pallas_call(
        _merge_kernel,
        out_shape=jax.ShapeDtypeStruct((n, D_MODEL), F32),
        grid=(n // tm,),
        in_specs=[row(D_MODEL), gate(0), gate(1), gate(2), row(SSM_WIDTH), row(ATT_WIDTH),
                  row(RET_HEADS * RET_DIM)] + [full(a) for a in ws],
        out_specs=row(D_MODEL),
        compiler_params=_params(("arbitrary",)),
        name="merge",
    )(x, h, h, h, y_ssm, y_att, y_ret, *ws)


CAND_ROWS = 80


def _cand_flat_index():
    f = [b for b in range(16)]
    for a in range(1, 8):
        f += [a * 16 + b for b in range(8)]
    f += [a * 16 for a in range(8, 16)]
    return np.asarray(f, np.float32)


def _split_bf16(x):
    hi = x.astype(BF)
    return hi, (x - hi.astype(F32)).astype(BF)


def _take16(work, order, break_ties):
    t = work.shape[1]
    row16 = lax.broadcasted_iota(jnp.int32, (PEER_TOPK, t), 0)
    rank = jnp.full(work.shape, float(PEER_TOPK), F32)
    vals = jnp.zeros((PEER_TOPK, t), F32)
    for r in range(PEER_TOPK):
        m = jnp.max(work, axis=0, keepdims=True)
        sel = work == m
        if break_ties:
            first = jnp.min(jnp.where(sel, order, 1e9), axis=0, keepdims=True)
            sel = order == first
        rank = jnp.where(sel, float(r), rank)
        work = jnp.where(sel, -jnp.inf, work)
        vals = jnp.where(row16 == r, m, vals)
    return rank, vals


def _route_heads(q, kh_ref, kl_ref, fidx_all, r2_ref, e2_ref, c1_ref, e1_ref, break_ties):
    t = q.shape[0]
    key_idx = lax.broadcasted_iota(jnp.int32, (N_KEYS, LANES), 0).astype(F32)
    taken = lambda rank: jnp.sum(jnp.where(rank < float(PEER_TOPK), 1.0, 0.0), axis=0, keepdims=True)
    bad = jnp.zeros((1, LANES), F32)

    def scores(hs):
        qh, ql = _split_bf16(q[:, hs * PEER_HALF:(hs + 1) * PEER_HALF])
        return _dot_nt(kh_ref[hs], qh) + (_dot_nt(kl_ref[hs], qh) + _dot_nt(kh_ref[hs], ql))

    all_scores = [scores(hs) for hs in range(2 * PEER_HEADS)]
    for h, c in [(h, c) for h in range(PEER_HEADS) for c in range(t // LANES)]:
        cols = slice(c * LANES, (c + 1) * LANES)
        s1 = all_scores[2 * h][:, cols]
        s2 = all_scores[2 * h + 1][:, cols]
        fidx = fidx_all[:, cols]
        rank1, v1 = _take16(s1, key_idx, break_ties)
        rank2, v2 = _take16(s2, key_idx, break_ties)
        blocks = [v1[0:1] + v2]
        for a in range(1, 8):
            blocks.append(v1[a:a + 1] + v2[0:8])
        blocks.append(v1[8:16] + v2[0:1])
        cand = jnp.concatenate(blocks, axis=0)
        rank_c, _ = _take16(cand, fidx, break_ties)
        chosen = jnp.where(rank_c < float(PEER_TOPK), 1.0, 0.0)
        if not break_ties:
            wrong = lambda n: jnp.where(n != float(PEER_TOPK), 1.0, 0.0)
            bad = bad + wrong(taken(rank1)) + wrong(taken(rank2)) + wrong(jnp.sum(chosen, axis=0, keepdims=True))
        counts = [jnp.sum(chosen[0:16], axis=0, keepdims=True)]
        for a in range(1, 8):
            counts.append(jnp.sum(chosen[8 + 8 * a:16 + 8 * a], axis=0, keepdims=True))
        for a in range(8, 16):
            counts.append(chosen[64 + a:65 + a])
        c1 = jnp.zeros(s1.shape, F32)
        for a in range(PEER_TOPK):
            c1 = jnp.where(rank1 == float(a), counts[a], c1)
        z = jnp.sum(chosen * jnp.exp(cand - cand[0:1]), axis=0, keepdims=True)
        r2_ref[h, :, cols] = rank2.astype(BF)
        e2_ref[h, :, cols] = jnp.exp(s2 - v2[0:1]).astype(BF)
        c1_ref[h, :, cols] = c1
        e1_ref[h, :, cols] = jnp.exp(s1 - v1[0:1]) / z
    return bad


def _route_kernel(x_ref, wh_ref, wl_ref, kh_ref, kl_ref, fidx_ref, r2_ref, e2_ref, c1_ref, e1_ref):
    x = x_ref[...]
    xh, xl = _split_bf16(x)
    q = _dot(xh, wh_ref[...]) + (_dot(xl, wh_ref[...]) + _dot(xh, wl_ref[...]))
    outs = (r2_ref, e2_ref, c1_ref, e1_ref)
    bad = _route_heads(q, kh_ref, kl_ref, fidx_ref[...], *outs, False)

    @pl.when(jnp.max(bad) > 0.0)
    def _():
        _route_heads(q, kh_ref, kl_ref, fidx_ref[...], *outs, True)


def _route(x1, lw, tb):
    n = x1.shape[0]
    fidx = jnp.broadcast_to(jnp.asarray(_cand_flat_index())[:, None], (CAND_ROWS, tb))
    out = jax.ShapeDtypeStruct((PEER_HEADS, N_KEYS, n), F32)
    ospec = pl.BlockSpec((PEER_HEADS, N_KEYS, tb), lambda i: (0, 0, i))
    out_bf = jax.ShapeDtypeStruct((PEER_HEADS, N_KEYS, n), BF)
    ospec_bf = ospec
    full2 = lambda a: pl.BlockSpec(a.shape, lambda i: (0, 0))
    full3 = lambda a: pl.BlockSpec(a.shape, lambda i: (0, 0, 0))
    return pl.pallas_call(
        _route_kernel,
        out_shape=(out_bf, out_bf, out, out),
        grid=(n // tb,),
        in_specs=[pl.BlockSpec((tb, D_MODEL), lambda i: (i, 0)), full2(lw["wq_hi"]), full2(lw["wq_lo"]),
                  full3(lw["keys_hi"]), full3(lw["keys_lo"]), full2(fidx)],
        out_specs=(ospec_bf, ospec_bf, ospec, ospec),
        compiler_params=_params(("arbitrary",)),
        name="peer_route",
    )(x1, lw["wq_hi"], lw["wq_lo"], lw["keys_hi"], lw["keys_lo"], fidx)


PEER_EXPERT_BLOCK = 1024


def _peer_kernel(x_ref, u_ref, vt_ref, r2_ref, e2_ref, c1_ref, e1_ref, lg_ref, lb_ref, o_ref, xb, acc, wt, rowbuf, act):
    s = pl.program_id(1)
    last = pl.num_programs(1) - 1
    tb = x_ref.shape[0]

    @pl.when(s == 0)
    def _():
        xb[...] = x_ref[...].T.astype(BF)
        acc[...] = jnp.zeros_like(acc)
        act[1] = jnp.zeros(act.shape[1:], BF)

    keys_per_step = PEER_EXPERT_BLOCK // N_KEYS
    first_key = pl.multiple_of(jnp.maximum(s - 1, 0) * keys_per_step, keys_per_step)

    for h in range(PEER_HEADS):
        for src, ref in enumerate((c1_ref, e1_ref)):
            blk = ref[h, pl.ds(first_key, keys_per_step), :]
            for j in range(keys_per_step):
                rowbuf[src, h, j] = jnp.broadcast_to(blk[j:j + 1], (SUBLANES, tb)).astype(BF)

    half = N_KEYS // 2
    group = keys_per_step // 2
    tiles = half // SUBLANES

    def spread(row):
        return jnp.broadcast_to(row[None], (tiles, SUBLANES, BF_CHUNK)).reshape(half, BF_CHUNK)

    for tc in range(tb // BF_CHUNK):
        cols = slice(tc * BF_CHUNK, (tc + 1) * BF_CHUNK)
        for jg in range(keys_per_step // group):
            for kh in range(2):
                keys = slice(kh * half, (kh + 1) * half)
                w = [jnp.zeros((half, BF_CHUNK), BF) for _ in range(group)]
                for h in range(PEER_HEADS):
                    r2 = r2_ref[h, keys, cols]
                    e2 = e2_ref[h, keys, cols]
                    for jj in range(group):
                        j = jg * group + jj
                        c1 = spread(rowbuf[0, h, j, :, cols])
                        e1 = spread(rowbuf[1, h, j, :, cols])
                        w[jj] = w[jj] + jnp.where(r2 < c1, e2 * e1, 0)
                for jj in range(group):
                    rows = slice((jg * group + jj) * N_KEYS + kh * half, (jg * group + jj) * N_KEYS + (kh + 1) * half)
                    wt[rows, cols] = w[jj]
    wt[...] = wt[...] * act[(s + 1) % 2]
    acc[...] += _dot(vt_ref[...], wt[...])
    act[s % 2] = jax.nn.gelu(_dot(u_ref[...], xb[...]).astype(BF))

    @pl.when(s == last)
    def _():
        r = DN_ALPHA * x_ref[...] + acc[...].T
        o_ref[...] = _layer_norm(r, lg_ref[...], lb_ref[...])


def _peer(x1, route, lw, tb):
    n = x1.shape[0]
    eb = PEER_EXPERT_BLOCK
    nblk = N_EXPERTS // eb
    rspec = pl.BlockSpec((PEER_HEADS, N_KEYS, tb), lambda i, e: (0, 0, i))
    rspec_bf = rspec
    vec = pl.BlockSpec((1, D_MODEL), lambda i, e: (0, 0))
    return pl.pallas_call(
        _peer_kernel,
        out_shape=jax.ShapeDtypeStruct((n, D_MODEL), F32),
        grid=(n // tb, nblk + 1),
        in_specs=[pl.BlockSpec((tb, D_MODEL), lambda i, s: (i, 0)),
                  pl.BlockSpec((eb, D_MODEL), lambda i, s: (jnp.minimum(s, nblk - 1), 0)),
                  pl.BlockSpec((D_MODEL, eb), lambda i, s: (0, jnp.maximum(s - 1, 0))),
                  rspec_bf, rspec_bf, rspec, rspec, vec, vec],
        out_specs=pl.BlockSpec((tb, D_MODEL), lambda i, s: (i, 0)),
        scratch_shapes=[pltpu.VMEM((D_MODEL, tb), BF), pltpu.VMEM((D_MODEL, tb), F32), pltpu.VMEM((eb, tb), BF),
                        pltpu.VMEM((2, PEER_HEADS, eb // N_KEYS, SUBLANES, tb), BF),
                        pltpu.VMEM((2, eb, tb), BF)],
        compiler_params=_params(("arbitrary", "arbitrary")),
        name="peer_experts",
    )(x1, lw["peer_u"], lw["peer_vt"], *route, lw["ln2_g"], lw["ln2_b"])


def _ple_kernel(x_ref, pe_ref, wg_ref, wp_ref, lg_ref, lb_ref, o_ref):
    x = x_ref[...]
    ple = jax.nn.sigmoid(_dot(x.astype(BF), wg_ref[...])) * _dot(pe_ref[...].astype(BF), wp_ref[...])
    o_ref[...] = _layer_norm(DN_ALPHA * x + ple, lg_ref[...], lb_ref[...])


def _ple(x2, pe, lw, tm):
    n = x2.shape[0]
    full = lambda a: pl.BlockSpec(a.shape, lambda i: (0, 0))
    ws = (lw["ple_w_g"], lw["ple_w_p"], lw["ln3_g"], lw["ln3_b"])
    return pl.pallas_call(
        _ple_kernel,
        out_shape=jax.ShapeDtypeStruct((n, D_MODEL), F32),
        grid=(n // tm,),
        in_specs=[pl.BlockSpec((tm, D_MODEL), lambda i: (i, 0)), pl.BlockSpec((tm, PLE_DIM), lambda i: (i, 0))]
        + [full(a) for a in ws],
        out_specs=pl.BlockSpec((tm, D_MODEL), lambda i: (i, 0)),
        compiler_params=_params(("arbitrary",)),
        name="ple",
    )(x2, pe, *ws)


def _time_major(u, bsz, t, bp):
    u = jnp.transpose(u.reshape(bsz, t, -1), (1, 0, 2))
    return jnp.pad(u, ((0, 0), (0, bp - bsz), (0, 0))).reshape(t * bp, -1)


def _batch_major(y, bsz, t, bp):
    return jnp.transpose(y.reshape(t, bp, -1)[:, :bsz], (1, 0, 2)).reshape(bsz * t, -1)


def _layer(x, pe, h0_re, h0_im, cache_k, cache_v, s0, pos, lw, bsz, t, prompt, n_rows):
    n = bsz * t
    h = _in_projection(x, lw["w_in"], 512, IN_COLS // 2)

    bp = max(8, bsz)
    u_tb = _time_major(h[:, OFF_U_SSM:OFF_U_SSM + SSM_WIDTH], bsz, t, bp)
    h0 = jnp.concatenate([h0_re.reshape(bsz, SSM_FLAT), h0_im.reshape(bsz, SSM_FLAT)], axis=1).astype(F32)
    h0 = jnp.pad(h0, ((0, bp - bsz), (0, 0)))
    y_tb, h_last = _s5_mixer(u_tb, h0, lw["s5"], t, bp, CHUNK)
    y_ssm = _batch_major(y_tb, bsz, t, bp)
    new_re = h_last[:bsz, :SSM_FLAT].reshape(bsz, SSM_GROUPS, SSM_STATE)
    new_im = h_last[:bsz, SSM_FLAT:].reshape(bsz, SSM_GROUPS, SSM_STATE)

    k_att = h[:, OFF_K_ATT:OFF_K_ATT + ATT_WIDTH].reshape(bsz, t, ATT_HEADS, ATT_HEAD_DIM)
    v_att = h[:, OFF_V_ATT:OFF_V_ATT + ATT_WIDTH].reshape(bsz, t, ATT_HEADS, ATT_HEAD_DIM)
    if prompt:
        y_att = _attn_prompt(h, lw["rel_bias"], bsz, t)
        new_k, new_v = k_att[:, -n_rows:], v_att[:, -n_rows:]
    else:
        r = cache_k.shape[1]
        y_att = _attn_sample(h, cache_k.reshape(bsz, r, ATT_WIDTH), cache_v.reshape(bsz, r, ATT_WIDTH),
                             lw["rel_bias"], bsz)
        new_k, new_v = k_att, v_att

    y_ret, s_pair = _retention(h, _pair_states(s0), _retention_tables(pos), lw["ret_gn_g"], bsz, t,
                               4 if prompt else 1)
    s_fin = _unpair_states(s_pair)

    tok = 512 if n % 512 == 0 else 256
    x1 = _merge(x, h, y_ssm, y_att, y_ret, lw, 256)
    route = _route(x1, lw, 256)
    x2 = _peer(x1, route, lw, tok)
    x3 = _ple(x2, pe, lw, 256)
    return x3, (new_re, new_im, new_k, new_v, s_fin)


def _layer_weights(li, w_in, ssm_lam_re, ssm_lam_im, ssm_b_re, ssm_b_im, ssm_c_re, ssm_c_im, ssm_log_dt, ssm_d,
                   ssm_w_glu, ssm_b_glu, att_rel_bias, ret_gn_g, w_br_ssm, w_br_att, w_br_ret, w_o, ln1_g, ln1_b,
                   peer_w_q, peer_sub_keys, peer_u, peer_v, ln2_g, ln2_b, ple_w_g, ple_w_p, ln3_g, ln3_b):
    vec = lambda a: a[li].astype(F32).reshape(1, -1)
    w = w_in[li]
    wq_hi, wq_lo = _split_bf16(peer_w_q[li].astype(F32))
    keys_hi, keys_lo = _split_bf16(peer_sub_keys[li].astype(F32).reshape(2 * PEER_HEADS, N_KEYS, PEER_HALF))
    return {
        "w_in": jnp.concatenate([w[:, OLD_GATE_START:], w[:, :OLD_GATE_START]], axis=1).astype(BF),
        "s5": _s5_tables(ssm_lam_re[li], ssm_lam_im[li], ssm_b_re[li], ssm_b_im[li], ssm_c_re[li], ssm_c_im[li],
                         ssm_log_dt[li], ssm_d[li], ssm_w_glu[li], ssm_b_glu[li]),
        "rel_bias": _rel_bias_tile(att_rel_bias[li]),
        "ret_gn_g": vec(ret_gn_g),
        "w_br_ssm": w_br_ssm[li].astype(BF), "w_br_att": w_br_att[li].astype(BF), "w_br_ret": w_br_ret[li].astype(BF),
        "w_o": w_o[li].astype(BF), "ln1_g": vec(ln1_g), "ln1_b": vec(ln1_b),
        "wq_hi": wq_hi, "wq_lo": wq_lo, "keys_hi": keys_hi, "keys_lo": keys_lo,
        "peer_u": peer_u[li].astype(BF), "peer_vt": peer_v[li].astype(BF).T,
        "ln2_g": vec(ln2_g), "ln2_b": vec(ln2_b),
        "ple_w_g": ple_w_g[li].astype(BF), "ple_w_p": ple_w_p[li].astype(BF), "ln3_g": vec(ln3_g), "ln3_b": vec(ln3_b),
    }


def kernel(x_prompt, x_sample, p_prompt, p_sample, state_ssm_re, state_ssm_im, cache_attn_k, cache_attn_v, state_ret, w_in, ssm_lam_re, ssm_lam_im, ssm_b_re, ssm_b_im, ssm_c_re, ssm_c_im, ssm_log_dt, ssm_d, ssm_w_glu, ssm_b_glu, att_rel_bias, ret_gn_g, w_br_ssm, w_br_att, w_br_ret, w_o, ln1_g, ln1_b, peer_w_q, peer_sub_keys, peer_u, peer_v, ln2_g, ln2_b, ple_w_g, ple_w_p, ln3_g, ln3_b):
    bp, tp, _ = x_prompt.shape
    bs, ts, _ = x_sample.shape
    n_rows = cache_attn_k.shape[2]
    pos_p = jnp.arange(tp)
    pos_s = PAST_LEN + jnp.arange(ts)
    y_p = x_prompt.reshape(bp * tp, D_MODEL).astype(F32)
    y_s = x_sample.reshape(bs * ts, D_MODEL).astype(F32)
    new_p, new_s = [], []
    for li in range(DEPTH):
        lw = _layer_weights(li, w_in, ssm_lam_re, ssm_lam_im, ssm_b_re, ssm_b_im, ssm_c_re, ssm_c_im, ssm_log_dt,
                            ssm_d, ssm_w_glu, ssm_b_glu, att_rel_bias, ret_gn_g, w_br_ssm, w_br_att, w_br_ret, w_o,
                            ln1_g, ln1_b, peer_w_q, peer_sub_keys, peer_u, peer_v, ln2_g, ln2_b, ple_w_g, ple_w_p,
                            ln3_g, ln3_b)
        zero_ssm = jnp.zeros((bp, SSM_GROUPS, SSM_STATE), F32)
        zero_ret = jnp.zeros((bp, RET_HEADS, RET_DIM, RET_DIM), F32)
        y_p, st = _layer(y_p, p_prompt[li].reshape(bp * tp, PLE_DIM), zero_ssm, zero_ssm, None, None, zero_ret,
                         pos_p, lw, bp, tp, True, n_rows)
        new_p.append(st)
        y_s, st = _layer(y_s, p_sample[li].reshape(bs * ts, PLE_DIM), state_ssm_re[li], state_ssm_im[li],
                         cache_attn_k[li], cache_attn_v[li], state_ret[li], pos_s, lw, bs, ts, False, n_rows)
        new_s.append(st)
    stack = lambda states, k: jnp.stack([s[k] for s in states])
    return (y_p.reshape(bp, tp, D_MODEL), y_s.reshape(bs, ts, D_MODEL),
            stack(new_p, 0), stack(new_p, 1), stack(new_p, 2), stack(new_p, 3), stack(new_p, 4),
            stack(new_s, 0), stack(new_s, 1), stack(new_s, 2), stack(new_s, 3), stack(new_s, 4))
```

```python
import functools

import jax
import jax.numpy as jnp
import numpy as np
from jax import lax
from jax.experimental import pallas as pl
from jax.experimental.pallas import tpu as pltpu

F32 = jnp.float32
BF = jnp.bfloat16

D_MODEL = 1024
DEPTH = 2
PAST_LEN = 2048
CHUNK = 64
PLE_DIM = 256
SSM_WIDTH = 256
SSM_GROUP = 16
SSM_GROUPS = 16
SSM_STATE = 64
SSM_FLAT = SSM_GROUPS * SSM_STATE
ATT_HEADS = 8
ATT_HEAD_DIM = 64
ATT_WIDTH = 512
ATT_PAST = 8 * CHUNK
ATT_BAND = ATT_PAST + CHUNK
REL_CLIP = 256
RET_HEADS = 4
RET_DIM = 64
ROPE_BASE = 10000.0
PEER_HEADS = 8
PEER_HALF = 128
N_KEYS = 128
N_EXPERTS = N_KEYS * N_KEYS
PEER_TOPK = 16
SUBLANES = 8
BF_CHUNK = 256
DN_ALPHA = (2 * DEPTH) ** 0.25
LN_EPS = 1e-5
LANES = 128
NEG_MASK = float(jnp.finfo(jnp.float32).min)

IN_COLS = 5888
OFF_GATES = 0
OFF_U_SSM = 3072
OFF_Q_ATT = 3328
OFF_K_ATT = 3840
OFF_V_ATT = 4352
OFF_Q_RET = 4864
OFF_K_RET = 5120
OFF_V_RET = 5376
OFF_G_RET = 5632
OLD_GATE_START = 2816

VMEM_LIMIT = 48 * 1024 * 1024


def _params(sem):
    return pltpu.CompilerParams(dimension_semantics=sem, vmem_limit_bytes=VMEM_LIMIT)


def _dot(a, b):
    return jnp.dot(a, b, preferred_element_type=F32)


def _dot_nt(a, b):
    return lax.dot_general(a, b, (((1,), (1,)), ((), ())), preferred_element_type=F32)


def _dot_tn(a, b):
    return lax.dot_general(a, b, (((0,), (0,)), ((), ())), preferred_element_type=F32)


def _layer_norm(r, g, b):
    mu = jnp.mean(r, axis=-1, keepdims=True)
    d = r - mu
    var = jnp.mean(d * d, axis=-1, keepdims=True)
    return d * lax.rsqrt(var + LN_EPS) * g + b


def _matmul_kernel(x_ref, w_ref, o_ref):
    o_ref[...] = _dot(x_ref[...].astype(BF), w_ref[...])


def _in_projection(x, w, tm, tn):
    m, k = x.shape
    n = w.shape[1]
    return pl.pallas_call(
        _matmul_kernel,
        out_shape=jax.ShapeDtypeStruct((m, n), F32),
        grid=(n // tn, m // tm),
        in_specs=[pl.BlockSpec((tm, k), lambda j, i: (i, 0)),
                  pl.BlockSpec((k, tn), lambda j, i: (0, j))],
        out_specs=pl.BlockSpec((tm, tn), lambda j, i: (i, j)),
        compiler_params=_params(("arbitrary", "arbitrary")),
        name="in_projection",
    )(x, w)


def _s5_kernel(u_ref, bm_ref, a_ref, cm_ref, d_ref, wg_ref, bg_ref, h0_ref, y_ref, hl_ref, hbuf, hst, *, steps, bp):
    @pl.when(pl.program_id(0) == 0)
    def _():
        hst[...] = h0_ref[...]

    u = u_ref[...]
    hbuf[...] = _dot(u.astype(BF), bm_ref[...])
    a_re = jnp.broadcast_to(a_ref[0:1, :], (bp, SSM_FLAT))
    a_im = jnp.broadcast_to(a_ref[1:2, :], (bp, SSM_FLAT))

    def step(t, carry):
        h_re, h_im = carry
        r0 = pl.multiple_of(t * bp, bp)
        n_re = a_re * h_re - a_im * h_im + hbuf[pl.ds(r0, bp), 0:SSM_FLAT]
        n_im = a_re * h_im + a_im * h_re + hbuf[pl.ds(r0, bp), SSM_FLAT:2 * SSM_FLAT]
        hbuf[pl.ds(r0, bp), 0:SSM_FLAT] = n_re
        hbuf[pl.ds(r0, bp), SSM_FLAT:2 * SSM_FLAT] = n_im
        return n_re, n_im

    h_re, h_im = lax.fori_loop(0, steps, step, (hst[:, 0:SSM_FLAT], hst[:, SSM_FLAT:2 * SSM_FLAT]))
    hst[:, 0:SSM_FLAT] = h_re
    hst[:, SSM_FLAT:2 * SSM_FLAT] = h_im
    hl_ref[...] = hst[...]
    y = _dot(hbuf[...].astype(BF), cm_ref[...]) + d_ref[...] * u
    z = jax.nn.gelu(y)
    y_ref[...] = z * jax.nn.sigmoid(_dot(z.astype(BF), wg_ref[...]) + bg_ref[...])


def _s5_mixer(u_tb, h0, sp, n_steps, bp, steps):
    rows = steps * bp
    kern = functools.partial(_s5_kernel, steps=steps, bp=bp)
    const = lambda c: (0, 0)
    return pl.pallas_call(
        kern,
        out_shape=(jax.ShapeDtypeStruct((n_steps * bp, SSM_WIDTH), F32),
                   jax.ShapeDtypeStruct((bp, 2 * SSM_FLAT), F32)),
        grid=(n_steps // steps,),
        in_specs=[pl.BlockSpec((rows, SSM_WIDTH), lambda c: (c, 0)),
                  pl.BlockSpec((SSM_WIDTH, 2 * SSM_FLAT), const),
                  pl.BlockSpec((2, SSM_FLAT), const),
                  pl.BlockSpec((2 * SSM_FLAT, SSM_WIDTH), const),
                  pl.BlockSpec((1, SSM_WIDTH), const),
                  pl.BlockSpec((SSM_WIDTH, SSM_WIDTH), const),
                  pl.BlockSpec((1, SSM_WIDTH), const),
                  pl.BlockSpec((bp, 2 * SSM_FLAT), const)],
        out_specs=(pl.BlockSpec((rows, SSM_WIDTH), lambda c: (c, 0)),
                   pl.BlockSpec((bp, 2 * SSM_FLAT), const)),
        scratch_shapes=[pltpu.VMEM((rows, 2 * SSM_FLAT), F32), pltpu.VMEM((bp, 2 * SSM_FLAT), F32)],
        compiler_params=_params(("arbitrary",)),
        name="s5_mixer",
    )(u_tb, sp["bmat"], sp["a"], sp["cmat"], sp["d"], sp["w_glu"], sp["b_glu"], h0)


def _s5_tables(lam_re, lam_im, b_re, b_im, c_re, c_im, log_dt, d_skip, w_glu, b_glu):
    lr, li = lam_re.astype(F32), lam_im.astype(F32)
    dt = jnp.exp(log_dt.astype(F32))[:, None]
    mag = jnp.exp(lr * dt)
    a_re, a_im = mag * jnp.cos(li * dt), mag * jnp.sin(li * dt)
    den = lr * lr + li * li
    k_re = ((a_re - 1.0) * lr + a_im * li) / den
    k_im = (a_im * lr - (a_re - 1.0) * li) / den
    br, bi = b_re.astype(F32), b_im.astype(F32)
    bbar_re = k_re[..., None] * br - k_im[..., None] * bi
    bbar_im = k_re[..., None] * bi + k_im[..., None] * br
    eye = jnp.eye(SSM_GROUPS, dtype=F32)

    def in_blockdiag(m):
        m = jnp.transpose(m, (0, 2, 1))
        return (m[:, :, None, :] * eye[:, None, :, None]).reshape(SSM_WIDTH, SSM_FLAT)

    def out_blockdiag(m):
        m = jnp.transpose(m, (0, 2, 1))
        return (m[:, :, None, :] * eye[:, None, :, None]).reshape(SSM_FLAT, SSM_WIDTH)

    bmat = jnp.concatenate([in_blockdiag(bbar_re), in_blockdiag(bbar_im)], axis=1)
    cmat = jnp.concatenate([out_blockdiag(c_re.astype(F32)), -out_blockdiag(c_im.astype(F32))], axis=0)
    a = jnp.stack([a_re.reshape(SSM_FLAT), a_im.reshape(SSM_FLAT)])
    return {"bmat": bmat.astype(BF), "cmat": cmat.astype(BF), "a": a,
            "d": d_skip.astype(F32).reshape(1, SSM_WIDTH), "w_glu": w_glu.astype(BF),
            "b_glu": b_glu.astype(F32).reshape(1, SSM_WIDTH)}


def _attend_chunk(q, kc, vc, bias_ref, valid):
    first = lax.broadcasted_iota(jnp.int32, (1, LANES), 1) < ATT_HEAD_DIM
    outs = []
    for hh in range(2):
        qm = jnp.where(first if hh == 0 else jnp.logical_not(first), q, 0.0).astype(BF)
        s = _dot_nt(qm, kc) + bias_ref[hh]
        if valid is not None:
            s = jnp.where(valid, s, NEG_MASK)
        m = jnp.max(s, axis=-1, keepdims=True)
        p = jnp.exp(s - m)
        l = jnp.sum(p, axis=-1, keepdims=True)
        outs.append(_dot(p.astype(BF), vc) / l)
    return jnp.where(first, outs[0], outs[1])


def _attn_prompt_kernel(q_ref, k0_ref, k1_ref, k2_ref, v0_ref, v1_ref, v2_ref, bias_ref, o_ref, *, cpb):
    tb = pl.program_id(2)
    rows = cpb * CHUNK
    kwin = jnp.concatenate([k0_ref[...], k1_ref[...], k2_ref[...]], axis=0).astype(BF)
    vwin = jnp.concatenate([v0_ref[...], v1_ref[...], v2_ref[...]], axis=0).astype(BF)
    col = lax.broadcasted_iota(jnp.int32, (1, ATT_BAND), 1)
    for c in range(cpb):
        lo = 2 * rows - ATT_PAST + c * CHUNK
        q = q_ref[c * CHUNK:(c + 1) * CHUNK, :] * (ATT_HEAD_DIM ** -0.5)
        valid = (col + (tb * rows - ATT_PAST + c * CHUNK)) >= 0
        o_ref[c * CHUNK:(c + 1) * CHUNK, :] = _attend_chunk(
            q, kwin[lo:lo + ATT_BAND], vwin[lo:lo + ATT_BAND], bias_ref, valid)


def _attn_prompt(h, bias, bsz, t):
    cpb = 4
    rows = cpb * CHUNK
    tbn = t // rows
    qc, kc, vc = OFF_Q_ATT // LANES, OFF_K_ATT // LANES, OFF_V_ATT // LANES

    def past(col0, back):
        return pl.BlockSpec((rows, LANES), lambda b, hp, tb: (b * tbn + jnp.maximum(tb - back, 0), col0 + hp))

    return pl.pallas_call(
        functools.partial(_attn_prompt_kernel, cpb=cpb),
        out_shape=jax.ShapeDtypeStruct((bsz * t, ATT_WIDTH), F32),
        grid=(bsz, ATT_HEADS // 2, tbn),
        in_specs=[past(qc, 0), past(kc, 2), past(kc, 1), past(kc, 0), past(vc, 2), past(vc, 1), past(vc, 0),
                  pl.BlockSpec((2, CHUNK, ATT_BAND), lambda b, hp, tb: (hp, 0, 0))],
        out_specs=pl.BlockSpec((rows, LANES), lambda b, hp, tb: (b * tbn + tb, hp)),
        compiler_params=_params(("arbitrary", "arbitrary", "arbitrary")),
        name="attn_prompt",
    )(h, h, h, h, h, h, h, bias)


def _attn_sample_kernel(q_ref, k_ref, v_ref, ck_ref, cv_ref, bias_ref, o_ref):
    kwin = jnp.concatenate([ck_ref[0], k_ref[...]], axis=0).astype(BF)
    vwin = jnp.concatenate([cv_ref[0], v_ref[...]], axis=0).astype(BF)
    q = q_ref[...] * (ATT_HEAD_DIM ** -0.5)
    o_ref[...] = _attend_chunk(q, kwin, vwin, bias_ref, None)


def _attn_sample(h, cache_k, cache_v, bias, bsz):
    qc, kc, vc = OFF_Q_ATT // LANES, OFF_K_ATT // LANES, OFF_V_ATT // LANES
    new = lambda col0: pl.BlockSpec((CHUNK, LANES), lambda b, hp: (b, col0 + hp))
    old = pl.BlockSpec((1, ATT_PAST, LANES), lambda b, hp: (b, 0, hp))
    return pl.pallas_call(
        _attn_sample_kernel,
        out_shape=jax.ShapeDtypeStruct((bsz * CHUNK, ATT_WIDTH), F32),
        grid=(bsz, ATT_HEADS // 2),
        in_specs=[new(qc), new(kc), new(vc), old, old,
                  pl.BlockSpec((2, CHUNK, ATT_BAND), lambda b, hp: (hp, 0, 0))],
        out_specs=pl.BlockSpec((CHUNK, LANES), lambda b, hp: (b, hp)),
        compiler_params=_params(("arbitrary", "arbitrary")),
        name="attn_sample",
    )(h, h, h, cache_k, cache_v, bias)


def _rel_bias_tile(table):
    i = jnp.arange(CHUNK)
    j = jnp.arange(ATT_BAND)
    idx = jnp.clip(i[:, None] - j[None, :] + ATT_PAST, -REL_CLIP, REL_CLIP) + REL_CLIP
    return jnp.moveaxis(table[idx], -1, 0).astype(F32)


def _ret_kernel(q_ref, k_ref, v_ref, g_ref, cos_ref, sin_ref, intra_ref, qw_ref, kw_ref, dec_ref, bd_ref, gn_ref,
                s0_ref, y_ref, sf_ref, state, *, cpb):
    @pl.when(pl.program_id(1) == 0)
    def _():
        state[...] = s0_ref[0]

    lane = lax.broadcasted_iota(jnp.int32, (1, LANES), 1)
    first = lane < RET_DIM
    low_half = (lane % RET_DIM) < (RET_DIM // 2)

    for c in range(cpb):
        rows = slice(c * CHUNK, (c + 1) * CHUNK)
        cos = cos_ref[rows, :]
        sin = sin_ref[rows, :]

        def rope(x):
            swapped = jnp.where(low_half, pltpu.roll(x, LANES - RET_DIM // 2, 1), pltpu.roll(x, RET_DIM // 2, 1))
            return x * cos + swapped * sin

        for hp in range(RET_HEADS // 2):
            cols = slice(hp * LANES, (hp + 1) * LANES)
            q = rope(q_ref[rows, cols]) * (RET_DIM ** -0.5)
            k = rope(k_ref[rows, cols])
            kb = k.astype(BF)
            vb = v_ref[rows, cols].astype(BF)
            intra = []
            for hh in range(2):
                qm = jnp.where(first if hh == 0 else jnp.logical_not(first), q, 0.0).astype(BF)
                s = _dot_nt(qm, kb) * intra_ref[2 * hp + hh]
                intra.append(_dot(s.astype(BF), vb))
            s_prev = state[hp]
            o = jnp.where(first, intra[0], intra[1]) + _dot((q * qw_ref[hp]).astype(BF), s_prev.astype(BF))
            kv = _dot_tn((k * kw_ref[hp]).astype(BF), vb)
            state[hp] = dec_ref[hp] * s_prev + bd_ref[...] * kv
            sum_a = jnp.sum(jnp.where(first, o, 0.0), axis=-1, keepdims=True)
            sum_b = jnp.sum(o, axis=-1, keepdims=True) - sum_a
            d = o - jnp.where(first, sum_a, sum_b) * (1.0 / RET_DIM)
            sq = d * d
            var_a = jnp.sum(jnp.where(first, sq, 0.0), axis=-1, keepdims=True)
            var_b = jnp.sum(sq, axis=-1, keepdims=True) - var_a
            var = jnp.where(first, var_a, var_b) * (1.0 / RET_DIM)
            g = g_ref[rows, cols]
            y_ref[rows, cols] = d * lax.rsqrt(var + LN_EPS) * gn_ref[:, cols] * (g * jax.nn.sigmoid(g))
    sf_ref[0] = state[...]


def _retention(h, s0, rt, gn, bsz, t, cpb):
    rows = cpb * CHUNK
    tbn = t // rows
    width = RET_HEADS * RET_DIM
    col = lambda off: pl.BlockSpec((rows, width), lambda b, tb: (b * tbn + tb, off // width))
    tab = pl.BlockSpec((rows, LANES), lambda b, tb: (tb, 0))
    const3 = lambda shape: pl.BlockSpec(shape, lambda b, tb: (0, 0, 0))
    st = pl.BlockSpec((1, 2, LANES, LANES), lambda b, tb: (b, 0, 0, 0))
    return pl.pallas_call(
        functools.partial(_ret_kernel, cpb=cpb),
        out_shape=(jax.ShapeDtypeStruct((bsz * t, width), F32),
                   jax.ShapeDtypeStruct((bsz, 2, LANES, LANES), F32)),
        grid=(bsz, tbn),
        in_specs=[col(OFF_Q_RET), col(OFF_K_RET), col(OFF_V_RET), col(OFF_G_RET), tab, tab,
                  const3((RET_HEADS, CHUNK, CHUNK)), const3((2, CHUNK, LANES)), const3((2, CHUNK, LANES)),
                  const3((2, LANES, LANES)), pl.BlockSpec((LANES, LANES), lambda b, tb: (0, 0)),
                  pl.BlockSpec((1, width), lambda b, tb: (0, 0)), st],
        out_specs=(pl.BlockSpec((rows, width), lambda b, tb: (b * tbn + tb, 0)), st),
        scratch_shapes=[pltpu.VMEM((2, LANES, LANES), F32)],
        compiler_params=_params(("arbitrary", "arbitrary")),
        name="retention",
    )(h, h, h, h, rt["cos"], rt["sin"], rt["intra"], rt["qw"], rt["kw"], rt["dec"], rt["bd"], gn, s0)


def _retention_tables(pos):
    lg = jnp.log1p(-(2.0 ** (-5.0 - jnp.arange(RET_HEADS, dtype=F32))))
    i = jnp.arange(CHUNK, dtype=F32)
    intra = jnp.exp(lg[:, None, None] * jnp.abs(i[:, None] - i[None, :]))
    lane_head = jnp.arange(2 * LANES) // RET_DIM
    lg_lane = lg[lane_head]
    qw = jnp.exp(lg_lane[None, :] * (i[:, None] + 1.0))
    kw = jnp.exp(lg_lane[None, :] * (CHUNK - 1.0 - i)[:, None])
    pair = lambda m: jnp.stack([m[:, :LANES], m[:, LANES:]])
    same_head = (jnp.arange(LANES)[:, None] // RET_DIM) == (jnp.arange(LANES)[None, :] // RET_DIM)
    bd = same_head.astype(F32)
    dec_lane = jnp.exp(lg_lane * CHUNK)
    dec = jnp.stack([bd * dec_lane[:LANES][:, None], bd * dec_lane[LANES:][:, None]])
    half = RET_DIM // 2
    freqs = ROPE_BASE ** (-jnp.arange(half, dtype=F32) / half)
    ang = pos.astype(F32)[:, None] * freqs[None]
    cos = jnp.tile(jnp.cos(ang), (1, LANES // half))
    sin = jnp.tile(jnp.concatenate([-jnp.sin(ang), jnp.sin(ang)], axis=1), (1, LANES // RET_DIM))
    return {"intra": intra, "qw": pair(qw), "kw": pair(kw), "dec": dec, "bd": bd, "cos": cos, "sin": sin}


def _pair_states(s):
    b = s.shape[0]
    s = s.astype(F32).reshape(b, 2, 2, RET_DIM, RET_DIM)
    out = jnp.zeros((b, 2, 2, RET_DIM, 2, RET_DIM), F32)
    out = out.at[:, :, 0, :, 0, :].set(s[:, :, 0]).at[:, :, 1, :, 1, :].set(s[:, :, 1])
    return out.reshape(b, 2, LANES, LANES)


def _unpair_states(s):
    b = s.shape[0]
    s = s.reshape(b, 2, 2, RET_DIM, 2, RET_DIM)
    return jnp.stack([s[:, :, 0, :, 0, :], s[:, :, 1, :, 1, :]], axis=2).reshape(b, RET_HEADS, RET_DIM, RET_DIM)


def _merge_kernel(x_ref, g1_ref, g2_ref, g3_ref, ys_ref, ya_ref, yr_ref, ws_ref, wa_ref, wr_ref, wo_ref,
                  lg_ref, lb_ref, o_ref):
    merged = (jax.nn.sigmoid(g1_ref[...]) * _dot(ys_ref[...].astype(BF), ws_ref[...])
              + jax.nn.sigmoid(g2_ref[...]) * _dot(ya_ref[...].astype(BF), wa_ref[...])
              + jax.nn.sigmoid(g3_ref[...]) * _dot(yr_ref[...].astype(BF), wr_ref[...]))
    r = DN_ALPHA * x_ref[...] + _dot(merged.astype(BF), wo_ref[...])
    o_ref[...] = _layer_norm(r, lg_ref[...], lb_ref[...])


def _merge(x, h, y_ssm, y_att, y_ret, lw, tm):
    n = x.shape[0]
    row = lambda w: pl.BlockSpec((tm, w), lambda i: (i, 0))
    gate = lambda j: pl.BlockSpec((tm, D_MODEL), lambda i: (i, j))
    full = lambda a: pl.BlockSpec(a.shape, lambda i: (0, 0))
    ws = (lw["w_br_ssm"], lw["w_br_att"], lw["w_br_ret"], lw["w_o"], lw["ln1_g"], lw["ln1_b"])
    return pl.pallas_call(
        _merge_kernel,
        out_shape=jax.ShapeDtypeStruct((n, D_MODEL), F32),
        grid=(n // tm,),
        in_specs=[row(D_MODEL), gate(0), gate(1), gate(2), row(SSM_WIDTH), row(ATT_WIDTH),
                  row(RET_HEADS * RET_DIM)] + [full(a) for a in ws],
        out_specs=row(D_MODEL),
        compiler_params=_params(("arbitrary",)),
        name="merge",
    )(x, h, h, h, y_ssm, y_att, y_ret, *ws)


CAND_ROWS = 80


def _cand_flat_index():
    f = [b for b in range(16)]
    for a in range(1, 8):
        f += [a * 16 + b for b in range(8)]
    f += [a * 16 for a in range(8, 16)]
    return np.asarray(f, np.float32)


def _split_bf16(x):
    hi = x.astype(BF)
    return hi, (x - hi.astype(F32)).astype(BF)


def _take16(work, order):
    t = work.shape[1]
    row16 = lax.broadcasted_iota(jnp.int32, (PEER_TOPK, t), 0)
    rank = jnp.full(work.shape, float(PEER_TOPK), F32)
    vals = jnp.zeros((PEER_TOPK, t), F32)
    for r in range(PEER_TOPK):
        m = jnp.max(work, axis=0, keepdims=True)
        first = jnp.min(jnp.where(work == m, order, 1e9), axis=0, keepdims=True)
        sel = order == first
        rank = jnp.where(sel, float(r), rank)
        work = jnp.where(sel, -jnp.inf, work)
        vals = jnp.where(row16 == r, m, vals)
    return rank, vals


def _route_head_exact(s1, s2, fidx, outs, h):
    r2_ref, e2_ref, c1_ref, e1_ref = outs
    key_idx = lax.broadcasted_iota(jnp.int32, s1.shape, 0).astype(F32)
    rank1, v1 = _take16(s1, key_idx)
    rank2, v2 = _take16(s2, key_idx)
    blocks = [v1[0:1] + v2]
    for a in range(1, 8):
        blocks.append(v1[a:a + 1] + v2[0:8])
    blocks.append(v1[8:16] + v2[0:1])
    cand = jnp.concatenate(blocks, axis=0)
    rank_c, _ = _take16(cand, fidx)
    chosen = jnp.where(rank_c < float(PEER_TOPK), 1.0, 0.0)
    counts = [jnp.sum(chosen[0:16], axis=0, keepdims=True)]
    for a in range(1, 8):
        counts.append(jnp.sum(chosen[8 + 8 * a:16 + 8 * a], axis=0, keepdims=True))
    for a in range(8, 16):
        counts.append(chosen[64 + a:65 + a])
    c1 = jnp.zeros(s1.shape, F32)
    for a in range(PEER_TOPK):
        c1 = jnp.where(rank1 == float(a), counts[a], c1)
    z = jnp.sum(chosen * jnp.exp(cand - cand[0:1]), axis=0, keepdims=True)
    r2_ref[h] = rank2.astype(BF)
    e2_ref[h] = jnp.exp(s2 - v2[0:1]).astype(BF)
    c1_ref[h] = c1
    e1_ref[h] = jnp.exp(s1 - v1[0:1]) / z


def _batcher_pairs(n):
    pairs = []

    def merge(lo, m, r):
        step = r * 2
        if step < m:
            merge(lo, m, step)
            merge(lo + r, m, step)
            pairs.extend((i, i + r) for i in range(lo + r, lo + m - r, step))
        else:
            pairs.append((lo, lo + r))

    def sort(lo, m):
        if m > 1:
            sort(lo, m // 2)
            sort(lo + m // 2, m // 2)
            merge(lo, m, 1)

    sort(0, n)
    return pairs


SORT16_PAIRS = _batcher_pairs(PEER_TOPK)


def _exchange(x, i, j):
    a, b = x[i], x[j]
    if b is None:
        return
    if a is None:
        x[i], x[j] = b, None
    else:
        x[i], x[j] = jnp.maximum(a, b), jnp.minimum(a, b)


def _top16_values(tiles):
    x = list(tiles) + [None] * (PEER_TOPK - len(tiles))
    for i, j in SORT16_PAIRS:
        _exchange(x, i, j)
    for shift in (4, 2, 1):
        y = [None if v is None else pltpu.roll(v, shift, 0) for v in x]
        z = []
        for i in range(PEER_TOPK):
            a, b = x[i], y[PEER_TOPK - 1 - i]
            z.append(b if a is None else a if b is None else jnp.maximum(a, b))
        for stride in (8, 4, 2, 1):
            for i in range(PEER_TOPK):
                if i & stride == 0:
                    _exchange(z, i, i + stride)
        x = z
    return x


def _count_greater(v, x):
    b3 = v[7] > x
    b2 = jnp.where(b3, v[11], v[3]) > x
    b1 = jnp.where(b3, jnp.where(b2, v[13], v[9]), jnp.where(b2, v[5], v[1])) > x
    b0 = jnp.where(b3, jnp.where(b2, jnp.where(b1, v[14], v[12]), jnp.where(b1, v[10], v[8])),
                   jnp.where(b2, jnp.where(b1, v[6], v[4]), jnp.where(b1, v[2], v[0]))) > x
    lo = (jnp.where(b3, 8.0, 0.0) + jnp.where(b2, 4.0, 0.0)) + (jnp.where(b1, 2.0, 0.0) + jnp.where(b0, 1.0, 0.0))
    return jnp.where(v[15] > x, float(PEER_TOPK), lo)


def _column_sum(tiles):
    parts = list(tiles)
    while len(parts) > 1:
        parts = [parts[i] + parts[i + 1] for i in range(0, len(parts) - 1, 2)] + parts[len(parts) & ~1:]
    total = parts[0]
    for shift in (4, 2, 1):
        total = total + pltpu.roll(total, shift, 0)
    return total


def _route_head_fast(s1, s2, outs, h, cols):
    r2_ref, e2_ref, c1_ref, e1_ref = outs
    n_tiles = N_KEYS // SUBLANES
    t1 = [s1[i * SUBLANES:(i + 1) * SUBLANES] for i in range(n_tiles)]
    t2 = [s2[i * SUBLANES:(i + 1) * SUBLANES] for i in range(n_tiles)]
    v1 = _top16_values(t1)
    v2 = _top16_values(t2)
    sub = lax.broadcasted_iota(jnp.int32, t1[0].shape, 0)

    def by_sublane(vals):
        out = vals[0]
        for b in range(1, SUBLANES):
            out = jnp.where(sub == b, vals[b], out)
        return out

    v2_lo, v2_hi, v1_hi = by_sublane(v2[0:8]), by_sublane(v2[8:16]), by_sublane(v1[8:16])
    cand = [v1[0] + v2_lo, v1[0] + v2_hi] + [v1[a] + v2_lo for a in range(1, 8)] + [v1_hi + v2[0]]
    tau = _top16_values(cand)[PEER_TOPK - 1]
    chosen = [jnp.where(c >= tau, 1.0, 0.0) for c in cand]
    counts = [_column_sum(chosen[0:2])] + [_column_sum([chosen[1 + a]]) for a in range(1, 8)]
    counts += [jnp.broadcast_to(chosen[9][a:a + 1], sub.shape) for a in range(SUBLANES)]
    z = _column_sum([ch * jnp.exp(c - cand[0][0:1]) for ch, c in zip(chosen, cand)])
    inv_z = 1.0 / z

    rank2 = [_count_greater(v2, x) for x in t2]
    c1 = []
    for x in t1:
        c = jnp.zeros_like(x)
        for a in range(PEER_TOPK):
            c = jnp.where(x == v1[a], counts[a], c)
        c1.append(c)
    r2_ref[h, :, cols] = jnp.concatenate(rank2, axis=0).astype(BF)
    e2_ref[h, :, cols] = jnp.concatenate([jnp.exp(x - v2[0]) for x in t2], axis=0).astype(BF)
    c1_ref[h, :, cols] = jnp.concatenate(c1, axis=0)
    e1_ref[h, :, cols] = jnp.concatenate([jnp.exp(x - v1[0]) * inv_z for x in t1], axis=0)

    no_tie_rank_sum = float(sum(range(PEER_TOPK)) + (N_KEYS - PEER_TOPK) * PEER_TOPK)
    bad = jnp.where(_column_sum(rank2) != no_tie_rank_sum, 1.0, 0.0)
    bad = bad + jnp.where(_column_sum([jnp.where(x >= v1[PEER_TOPK - 1], 1.0, 0.0) for x in t1]) != PEER_TOPK, 1.0, 0.0)
    for a in range(PEER_TOPK - 1):
        bad = bad + jnp.where(v1[a] == v1[a + 1], 1.0, 0.0)
    return bad + jnp.where(_column_sum(chosen) != PEER_TOPK, 1.0, 0.0)


def _route_kernel(x_ref, wh_ref, wl_ref, kh_ref, kl_ref, fidx_ref, r2_ref, e2_ref, c1_ref, e1_ref):
    x = x_ref[...]
    t = x.shape[0]
    xh, xl = _split_bf16(x)
    q = _dot(xh, wh_ref[...]) + (_dot(xl, wh_ref[...]) + _dot(xh, wl_ref[...]))
    outs = (r2_ref, e2_ref, c1_ref, e1_ref)

    def scores(hs):
        qh, ql = _split_bf16(q[:, hs * PEER_HALF:(hs + 1) * PEER_HALF])
        return _dot_nt(kh_ref[hs], qh) + (_dot_nt(kl_ref[hs], qh) + _dot_nt(kh_ref[hs], ql))

    all_scores = [scores(hs) for hs in range(2 * PEER_HEADS)]
    bad = []
    for h in range(PEER_HEADS):
        flags = [_route_head_fast(all_scores[2 * h][:, c * LANES:(c + 1) * LANES],
                                  all_scores[2 * h + 1][:, c * LANES:(c + 1) * LANES], outs, h,
                                  slice(c * LANES, (c + 1) * LANES)) for c in range(t // LANES)]
        bad.append(sum(flags[1:], flags[0]))

    for h in range(PEER_HEADS):
        @pl.when(jnp.max(bad[h]) > 0.0)
        def _(h=h):
            _route_head_exact(all_scores[2 * h], all_scores[2 * h + 1], fidx_ref[...], outs, h)


def _route(x1, lw, tb):
    n = x1.shape[0]
    fidx = jnp.broadcast_to(jnp.asarray(_cand_flat_index())[:, None], (CAND_ROWS, tb))
    out = jax.ShapeDtypeStruct((PEER_HEADS, N_KEYS, n), F32)
    ospec = pl.BlockSpec((PEER_HEADS, N_KEYS, tb), lambda i: (0, 0, i))
    out_bf = jax.ShapeDtypeStruct((PEER_HEADS, N_KEYS, n), BF)
    ospec_bf = ospec
    full2 = lambda a: pl.BlockSpec(a.shape, lambda i: (0, 0))
    full3 = lambda a: pl.BlockSpec(a.shape, lambda i: (0, 0, 0))
    return pl.pallas_call(
        _route_kernel,
        out_shape=(out_bf, out_bf, out, out),
        grid=(n // tb,),
        in_specs=[pl.BlockSpec((tb, D_MODEL), lambda i: (i, 0)), full2(lw["wq_hi"]), full2(lw["wq_lo"]),
                  full3(lw["keys_hi"]), full3(lw["keys_lo"]), full2(fidx)],
        out_specs=(ospec_bf, ospec_bf, ospec, ospec),
        compiler_params=_params(("arbitrary",)),
        name="peer_route",
    )(x1, lw["wq_hi"], lw["wq_lo"], lw["keys_hi"], lw["keys_lo"], fidx)


PEER_EXPERT_BLOCK = 1024


def _peer_kernel(x_ref, u_ref, vt_ref, r2_ref, e2_ref, c1_ref, e1_ref, lg_ref, lb_ref, o_ref, xb, acc, wt, rowbuf, act):
    s = pl.program_id(1)
    last = pl.num_programs(1) - 1
    tb = x_ref.shape[0]

    @pl.when(s == 0)
    def _():
        xb[...] = x_ref[...].T.astype(BF)
        acc[...] = jnp.zeros_like(acc)
        act[1] = jnp.zeros(act.shape[1:], BF)

    keys_per_step = PEER_EXPERT_BLOCK // N_KEYS
    first_key = pl.multiple_of(jnp.maximum(s - 1, 0) * keys_per_step, keys_per_step)

    for h in range(PEER_HEADS):
        for src, ref in enumerate((c1_ref, e1_ref)):
            blk = ref[h, pl.ds(first_key, keys_per_step), :]
            for j in range(keys_per_step):
                rowbuf[src, h, j] = jnp.broadcast_to(blk[j:j + 1], (SUBLANES, tb)).astype(BF)

    half = N_KEYS // 2
    group = keys_per_step // 2
    tiles = half // SUBLANES

    def spread(row):
        return jnp.broadcast_to(row[None], (tiles, SUBLANES, BF_CHUNK)).reshape(half, BF_CHUNK)

    for tc in range(tb // BF_CHUNK):
        cols = slice(tc * BF_CHUNK, (tc + 1) * BF_CHUNK)
        for jg in range(keys_per_step // group):
            for kh in range(2):
                keys = slice(kh * half, (kh + 1) * half)
                w = [jnp.zeros((half, BF_CHUNK), BF) for _ in range(group)]
                for h in range(PEER_HEADS):
                    r2 = r2_ref[h, keys, cols]
                    e2 = e2_ref[h, keys, cols]
                    for jj in range(group):
                        j = jg * group + jj
                        c1 = spread(rowbuf[0, h, j, :, cols])
                        e1 = spread(rowbuf[1, h, j, :, cols])
                        w[jj] = w[jj] + jnp.where(r2 < c1, e2 * e1, 0)
                for jj in range(group):
                    rows = slice((jg * group + jj) * N_KEYS + kh * half, (jg * group + jj) * N_KEYS + (kh + 1) * half)
                    wt[rows, cols] = w[jj]
    wt[...] = wt[...] * act[(s + 1) % 2]
    acc[...] += _dot(vt_ref[...], wt[...])
    act[s % 2] = jax.nn.gelu(_dot(u_ref[...], xb[...]).astype(BF))

    @pl.when(s == last)
    def _():
        r = DN_ALPHA * x_ref[...] + acc[...].T
        o_ref[...] = _layer_norm(r, lg_ref[...], lb_ref[...])


def _peer(x1, route, lw, tb):
    n = x1.shape[0]
    eb = PEER_EXPERT_BLOCK
    nblk = N_EXPERTS // eb
    rspec = pl.BlockSpec((PEER_HEADS, N_KEYS, tb), lambda i, e: (0, 0, i))
    rspec_bf = rspec
    vec = pl.BlockSpec((1, D_MODEL), lambda i, e: (0, 0))
    return pl.pallas_call(
        _peer_kernel,
        out_shape=jax.ShapeDtypeStruct((n, D_MODEL), F32),
        grid=(n // tb, nblk + 1),
        in_specs=[pl.BlockSpec((tb, D_MODEL), lambda i, s: (i, 0)),
                  pl.BlockSpec((eb, D_MODEL), lambda i, s: (jnp.minimum(s, nblk - 1), 0)),
                  pl.BlockSpec((D_MODEL, eb), lambda i, s: (0, jnp.maximum(s - 1, 0))),
                  rspec_bf, rspec_bf, rspec, rspec, vec, vec],
        out_specs=pl.BlockSpec((tb, D_MODEL), lambda i, s: (i, 0)),
        scratch_shapes=[pltpu.VMEM((D_MODEL, tb), BF), pltpu.VMEM((D_MODEL, tb), F32), pltpu.VMEM((eb, tb), BF),
                        pltpu.VMEM((2, PEER_HEADS, eb // N_KEYS, SUBLANES, tb), BF),
                        pltpu.VMEM((2, eb, tb), BF)],
        compiler_params=_params(("arbitrary", "arbitrary")),
        name="peer_experts",
    )(x1, lw["peer_u"], lw["peer_vt"], *route, lw["ln2_g"], lw["ln2_b"])


def _ple_kernel(x_ref, pe_ref, wg_ref, wp_ref, lg_ref, lb_ref, o_ref):
    x = x_ref[...]
    ple = jax.nn.sigmoid(_dot(x.astype(BF), wg_ref[...])) * _dot(pe_ref[...].astype(BF), wp_ref[...])
    o_ref[...] = _layer_norm(DN_ALPHA * x + ple, lg_ref[...], lb_ref[...])


def _ple(x2, pe, lw, tm):
    n = x2.shape[0]
    full = lambda a: pl.BlockSpec(a.shape, lambda i: (0, 0))
    ws = (lw["ple_w_g"], lw["ple_w_p"], lw["ln3_g"], lw["ln3_b"])
    return pl.pallas_call(
        _ple_kernel,
        out_shape=jax.ShapeDtypeStruct((n, D_MODEL), F32),
        grid=(n // tm,),
        in_specs=[pl.BlockSpec((tm, D_MODEL), lambda i: (i, 0)), pl.BlockSpec((tm, PLE_DIM), lambda i: (i, 0))]
        + [full(a) for a in ws],
        out_specs=pl.BlockSpec((tm, D_MODEL), lambda i: (i, 0)),
        compiler_params=_params(("arbitrary",)),
        name="ple",
    )(x2, pe, *ws)


def _time_major(u, bsz, t, bp):
    u = jnp.transpose(u.reshape(bsz, t, -1), (1, 0, 2))
    return jnp.pad(u, ((0, 0), (0, bp - bsz), (0, 0))).reshape(t * bp, -1)


def _batch_major(y, bsz, t, bp):
    return jnp.transpose(y.reshape(t, bp, -1)[:, :bsz], (1, 0, 2)).reshape(bsz * t, -1)


def _layer(x, pe, h0_re, h0_im, cache_k, cache_v, s0, pos, lw, bsz, t, prompt, n_rows):
    n = bsz * t
    h = _in_projection(x, lw["w_in"], 512, IN_COLS // 2)

    bp = max(8, bsz)
    u_tb = _time_major(h[:, OFF_U_SSM:OFF_U_SSM + SSM_WIDTH], bsz, t, bp)
    h0 = jnp.concatenate([h0_re.reshape(bsz, SSM_FLAT), h0_im.reshape(bsz, SSM_FLAT)], axis=1).astype(F32)
    h0 = jnp.pad(h0, ((0, bp - bsz), (0, 0)))
    y_tb, h_last = _s5_mixer(u_tb, h0, lw["s5"], t, bp, CHUNK)
    y_ssm = _batch_major(y_tb, bsz, t, bp)
    new_re = h_last[:bsz, :SSM_FLAT].reshape(bsz, SSM_GROUPS, SSM_STATE)
    new_im = h_last[:bsz, SSM_FLAT:].reshape(bsz, SSM_GROUPS, SSM_STATE)

    k_att = h[:, OFF_K_ATT:OFF_K_ATT + ATT_WIDTH].reshape(bsz, t, ATT_HEADS, ATT_HEAD_DIM)
    v_att = h[:, OFF_V_ATT:OFF_V_ATT + ATT_WIDTH].reshape(bsz, t, ATT_HEADS, ATT_HEAD_DIM)
    if prompt:
        y_att = _attn_prompt(h, lw["rel_bias"], bsz, t)
        new_k, new_v = k_att[:, -n_rows:], v_att[:, -n_rows:]
    else:
        r = cache_k.shape[1]
        y_att = _attn_sample(h, cache_k.reshape(bsz, r, ATT_WIDTH), cache_v.reshape(bsz, r, ATT_WIDTH),
                             lw["rel_bias"], bsz)
        new_k, new_v = k_att, v_att

    y_ret, s_pair = _retention(h, _pair_states(s0), _retention_tables(pos), lw["ret_gn_g"], bsz, t,
                               4 if prompt else 1)
    s_fin = _unpair_states(s_pair)

    tok = 512 if n % 512 == 0 else 256
    x1 = _merge(x, h, y_ssm, y_att, y_ret, lw, 256)
    route = _route(x1, lw, 256)
    x2 = _peer(x1, route, lw, tok)
    x3 = _ple(x2, pe, lw, 256)
    return x3, (new_re, new_im, new_k, new_v, s_fin)


def _layer_weights(li, w_in, ssm_lam_re, ssm_lam_im, ssm_b_re, ssm_b_im, ssm_c_re, ssm_c_im, ssm_log_dt, ssm_d,
                   ssm_w_glu, ssm_b_glu, att_rel_bias, ret_gn_g, w_br_ssm, w_br_att, w_br_ret, w_o, ln1_g, ln1_b,
                   peer_w_q, peer_sub_keys, peer_u, peer_v, ln2_g, ln2_b, ple_w_g, ple_w_p, ln3_g, ln3_b):
    vec = lambda a: a[li].astype(F32).reshape(1, -1)
    w = w_in[li]
    wq_hi, wq_lo = _split_bf16(peer_w_q[li].astype(F32))
    keys_hi, keys_lo = _split_bf16(peer_sub_keys[li].astype(F32).reshape(2 * PEER_HEADS, N_KEYS, PEER_HALF))
    return {
        "w_in": jnp.concatenate([w[:, OLD_GATE_START:], w[:, :OLD_GATE_START]], axis=1).astype(BF),
        "s5": _s5_tables(ssm_lam_re[li], ssm_lam_im[li], ssm_b_re[li], ssm_b_im[li], ssm_c_re[li], ssm_c_im[li],
                         ssm_log_dt[li], ssm_d[li], ssm_w_glu[li], ssm_b_glu[li]),
        "rel_bias": _rel_bias_tile(att_rel_bias[li]),
        "ret_gn_g": vec(ret_gn_g),
        "w_br_ssm": w_br_ssm[li].astype(BF), "w_br_att": w_br_att[li].astype(BF), "w_br_ret": w_br_ret[li].astype(BF),
        "w_o": w_o[li].astype(BF), "ln1_g": vec(ln1_g), "ln1_b": vec(ln1_b),
        "wq_hi": wq_hi, "wq_lo": wq_lo, "keys_hi": keys_hi, "keys_lo": keys_lo,
        "peer_u": peer_u[li].astype(BF), "peer_vt": peer_v[li].astype(BF).T,
        "ln2_g": vec(ln2_g), "ln2_b": vec(ln2_b),
        "ple_w_g": ple_w_g[li].astype(BF), "ple_w_p": ple_w_p[li].astype(BF), "ln3_g": vec(ln3_g), "ln3_b": vec(ln3_b),
    }


def kernel(x_prompt, x_sample, p_prompt, p_sample, state_ssm_re, state_ssm_im, cache_attn_k, cache_attn_v, state_ret, w_in, ssm_lam_re, ssm_lam_im, ssm_b_re, ssm_b_im, ssm_c_re, ssm_c_im, ssm_log_dt, ssm_d, ssm_w_glu, ssm_b_glu, att_rel_bias, ret_gn_g, w_br_ssm, w_br_att, w_br_ret, w_o, ln1_g, ln1_b, peer_w_q, peer_sub_keys, peer_u, peer_v, ln2_g, ln2_b, ple_w_g, ple_w_p, ln3_g, ln3_b):
    bp, tp, _ = x_prompt.shape
    bs, ts, _ = x_sample.shape
    n_rows = cache_attn_k.shape[2]
    pos_p = jnp.arange(tp)
    pos_s = PAST_LEN + jnp.arange(ts)
    y_p = x_prompt.reshape(bp * tp, D_MODEL).astype(F32)
    y_s = x_sample.reshape(bs * ts, D_MODEL).astype(F32)
    new_p, new_s = [], []
    for li in range(DEPTH):
        lw = _layer_weights(li, w_in, ssm_lam_re, ssm_lam_im, ssm_b_re, ssm_b_im, ssm_c_re, ssm_c_im, ssm_log_dt,
                            ssm_d, ssm_w_glu, ssm_b_glu, att_rel_bias, ret_gn_g, w_br_ssm, w_br_att, w_br_ret, w_o,
                            ln1_g, ln1_b, peer_w_q, peer_sub_keys, peer_u, peer_v, ln2_g, ln2_b, ple_w_g, ple_w_p,
                            ln3_g, ln3_b)
        zero_ssm = jnp.zeros((bp, SSM_GROUPS, SSM_STATE), F32)
        zero_ret = jnp.zeros((bp, RET_HEADS, RET_DIM, RET_DIM), F32)
        y_p, st = _layer(y_p, p_prompt[li].reshape(bp * tp, PLE_DIM), zero_ssm, zero_ssm, None, None, zero_ret,
                         pos_p, lw, bp, tp, True, n_rows)
        new_p.append(st)
        y_s, st = _layer(y_s, p_sample[li].reshape(bs * ts, PLE_DIM), state_ssm_re[li], state_ssm_im[li],
                         cache_attn_k[li], cache_attn_v[li], state_ret[li], pos_s, lw, bs, ts, False, n_rows)
        new_s.append(st)
    stack = lambda states, k: jnp.stack([s[k] for s in states])
    return (y_p.reshape(bp, tp, D_MODEL), y_s.reshape(bs, ts, D_MODEL),
            stack(new_p, 0), stack(new_p, 1), stack(new_p, 2), stack(new_p, 3), stack(new_p, 4),
            stack(new_s, 0), stack(new_s, 1), stack(new_s, 2), stack(new_s, 3), stack(new_s, 4))
```

```python
import functools

import jax
import jax.numpy as jnp
import numpy as np
from jax import lax
from jax.experimental import pallas as pl
from jax.experimental.pallas import tpu as pltpu

F32 = jnp.float32
BF = jnp.bfloat16

D_MODEL = 1024
DEPTH = 2
PAST_LEN = 2048
CHUNK = 64
PLE_DIM = 256
SSM_WIDTH = 256
SSM_GROUP = 16
SSM_GROUPS = 16
SSM_STATE = 64
SSM_FLAT = SSM_GROUPS * SSM_STATE
ATT_HEADS = 8
ATT_HEAD_DIM = 64
ATT_WIDTH = 512
ATT_PAST = 8 * CHUNK
ATT_BAND = ATT_PAST + CHUNK
REL_CLIP = 256
RET_HEADS = 4
RET_DIM = 64
ROPE_BASE = 10000.0
PEER_HEADS = 8
PEER_HALF = 128
N_KEYS = 128
N_EXPERTS = N_KEYS * N_KEYS
PEER_TOPK = 16
SUBLANES = 8
BF_CHUNK = 256
DN_ALPHA = (2 * DEPTH) ** 0.25
LN_EPS = 1e-5
LANES = 128
NEG_MASK = float(jnp.finfo(jnp.float32).min)

IN_COLS = 5888
OFF_GATES = 0
OFF_U_SSM = 3072
OFF_Q_ATT = 3328
OFF_K_ATT = 3840
OFF_V_ATT = 4352
OFF_Q_RET = 4864
OFF_K_RET = 5120
OFF_V_RET = 5376
OFF_G_RET = 5632
OLD_GATE_START = 2816

VMEM_LIMIT = 48 * 1024 * 1024


def _params(sem):
    return pltpu.CompilerParams(dimension_semantics=sem, vmem_limit_bytes=VMEM_LIMIT)


def _dot(a, b):
    return jnp.dot(a, b, preferred_element_type=F32)


def _dot_nt(a, b):
    return lax.dot_general(a, b, (((1,), (1,)), ((), ())), preferred_element_type=F32)


def _dot_tn(a, b):
    return lax.dot_general(a, b, (((0,), (0,)), ((), ())), preferred_element_type=F32)


def _layer_norm(r, g, b):
    mu = jnp.mean(r, axis=-1, keepdims=True)
    d = r - mu
    var = jnp.mean(d * d, axis=-1, keepdims=True)
    return d * lax.rsqrt(var + LN_EPS) * g + b


def _matmul_kernel(x_ref, w_ref, o_ref):
    o_ref[...] = _dot(x_ref[...].astype(BF), w_ref[...])


def _in_projection(x, w, tm, tn):
    m, k = x.shape
    n = w.shape[1]
    return pl.pallas_call(
        _matmul_kernel,
        out_shape=jax.ShapeDtypeStruct((m, n), F32),
        grid=(n // tn, m // tm),
        in_specs=[pl.BlockSpec((tm, k), lambda j, i: (i, 0)),
                  pl.BlockSpec((k, tn), lambda j, i: (0, j))],
        out_specs=pl.BlockSpec((tm, tn), lambda j, i: (i, j)),
        compiler_params=_params(("arbitrary", "arbitrary")),
        name="in_projection",
    )(x, w)


def _s5_kernel(u_ref, bm_ref, a_ref, cm_ref, d_ref, wg_ref, bg_ref, h0_ref, y_ref, hl_ref, hbuf, hst, *, steps, bp):
    @pl.when(pl.program_id(0) == 0)
    def _():
        hst[...] = h0_ref[...]

    u = u_ref[...]
    hbuf[...] = _dot(u.astype(BF), bm_ref[...])
    a_re = jnp.broadcast_to(a_ref[0:1, :], (bp, SSM_FLAT))
    a_im = jnp.broadcast_to(a_ref[1:2, :], (bp, SSM_FLAT))

    def step(t, carry):
        h_re, h_im = carry
        r0 = pl.multiple_of(t * bp, bp)
        n_re = a_re * h_re - a_im * h_im + hbuf[pl.ds(r0, bp), 0:SSM_FLAT]
        n_im = a_re * h_im + a_im * h_re + hbuf[pl.ds(r0, bp), SSM_FLAT:2 * SSM_FLAT]
        hbuf[pl.ds(r0, bp), 0:SSM_FLAT] = n_re
        hbuf[pl.ds(r0, bp), SSM_FLAT:2 * SSM_FLAT] = n_im
        return n_re, n_im

    h_re, h_im = lax.fori_loop(0, steps, step, (hst[:, 0:SSM_FLAT], hst[:, SSM_FLAT:2 * SSM_FLAT]))
    hst[:, 0:SSM_FLAT] = h_re
    hst[:, SSM_FLAT:2 * SSM_FLAT] = h_im
    hl_ref[...] = hst[...]
    y = _dot(hbuf[...].astype(BF), cm_ref[...]) + d_ref[...] * u
    z = jax.nn.gelu(y)
    y_ref[...] = z * jax.nn.sigmoid(_dot(z.astype(BF), wg_ref[...]) + bg_ref[...])


def _s5_mixer(u_tb, h0, sp, n_steps, bp, steps):
    rows = steps * bp
    kern = functools.partial(_s5_kernel, steps=steps, bp=bp)
    const = lambda c: (0, 0)
    return pl.pallas_call(
        kern,
        out_shape=(jax.ShapeDtypeStruct((n_steps * bp, SSM_WIDTH), F32),
                   jax.ShapeDtypeStruct((bp, 2 * SSM_FLAT), F32)),
        grid=(n_steps // steps,),
        in_specs=[pl.BlockSpec((rows, SSM_WIDTH), lambda c: (c, 0)),
                  pl.BlockSpec((SSM_WIDTH, 2 * SSM_FLAT), const),
                  pl.BlockSpec((2, SSM_FLAT), const),
                  pl.BlockSpec((2 * SSM_FLAT, SSM_WIDTH), const),
                  pl.BlockSpec((1, SSM_WIDTH), const),
                  pl.BlockSpec((SSM_WIDTH, SSM_WIDTH), const),
                  pl.BlockSpec((1, SSM_WIDTH), const),
                  pl.BlockSpec((bp, 2 * SSM_FLAT), const)],
        out_specs=(pl.BlockSpec((rows, SSM_WIDTH), lambda c: (c, 0)),
                   pl.BlockSpec((bp, 2 * SSM_FLAT), const)),
        scratch_shapes=[pltpu.VMEM((rows, 2 * SSM_FLAT), F32), pltpu.VMEM((bp, 2 * SSM_FLAT), F32)],
        compiler_params=_params(("arbitrary",)),
        name="s5_mixer",
    )(u_tb, sp["bmat"], sp["a"], sp["cmat"], sp["d"], sp["w_glu"], sp["b_glu"], h0)


def _s5_tables(lam_re, lam_im, b_re, b_im, c_re, c_im, log_dt, d_skip, w_glu, b_glu):
    lr, li = lam_re.astype(F32), lam_im.astype(F32)
    dt = jnp.exp(log_dt.astype(F32))[:, None]
    mag = jnp.exp(lr * dt)
    a_re, a_im = mag * jnp.cos(li * dt), mag * jnp.sin(li * dt)
    den = lr * lr + li * li
    k_re = ((a_re - 1.0) * lr + a_im * li) / den
    k_im = (a_im * lr - (a_re - 1.0) * li) / den
    br, bi = b_re.astype(F32), b_im.astype(F32)
    bbar_re = k_re[..., None] * br - k_im[..., None] * bi
    bbar_im = k_re[..., None] * bi + k_im[..., None] * br
    eye = jnp.eye(SSM_GROUPS, dtype=F32)

    def in_blockdiag(m):
        m = jnp.transpose(m, (0, 2, 1))
        return (m[:, :, None, :] * eye[:, None, :, None]).reshape(SSM_WIDTH, SSM_FLAT)

    def out_blockdiag(m):
        m = jnp.transpose(m, (0, 2, 1))
        return (m[:, :, None, :] * eye[:, None, :, None]).reshape(SSM_FLAT, SSM_WIDTH)

    bmat = jnp.concatenate([in_blockdiag(bbar_re), in_blockdiag(bbar_im)], axis=1)
    cmat = jnp.concatenate([out_blockdiag(c_re.astype(F32)), -out_blockdiag(c_im.astype(F32))], axis=0)
    a = jnp.stack([a_re.reshape(SSM_FLAT), a_im.reshape(SSM_FLAT)])
    return {"bmat": bmat.astype(BF), "cmat": cmat.astype(BF), "a": a,
            "d": d_skip.astype(F32).reshape(1, SSM_WIDTH), "w_glu": w_glu.astype(BF),
            "b_glu": b_glu.astype(F32).reshape(1, SSM_WIDTH)}


def _attend_chunk(q, kc, vc, bias_ref, valid):
    first = lax.broadcasted_iota(jnp.int32, (1, LANES), 1) < ATT_HEAD_DIM
    outs = []
    for hh in range(2):
        qm = jnp.where(first if hh == 0 else jnp.logical_not(first), q, 0.0).astype(BF)
        s = _dot_nt(qm, kc) + bias_ref[hh]
        if valid is not None:
            s = jnp.where(valid, s, NEG_MASK)
        m = jnp.max(s, axis=-1, keepdims=True)
        p = jnp.exp(s - m)
        l = jnp.sum(p, axis=-1, keepdims=True)
        outs.append(_dot(p.astype(BF), vc) / l)
    return jnp.where(first, outs[0], outs[1])


ATT_QROWS = 4 * CHUNK
ATT_WINDOW = ATT_PAST + ATT_QROWS


def _attn_prompt_kernel(q_ref, k0_ref, k1_ref, k2_ref, v0_ref, v1_ref, v2_ref, bias_ref, o_ref):
    tb = pl.program_id(2)
    kwin = jnp.concatenate([k0_ref[...], k1_ref[...], k2_ref[...]], axis=0).astype(BF)
    vwin = jnp.concatenate([v0_ref[...], v1_ref[...], v2_ref[...]], axis=0).astype(BF)
    first = lax.broadcasted_iota(jnp.int32, (1, LANES), 1) < ATT_HEAD_DIM
    q = q_ref[...] * (ATT_HEAD_DIM ** -0.5)
    q2 = jnp.concatenate([jnp.where(first, q, 0.0), jnp.where(first, 0.0, q)], axis=0).astype(BF)
    s = _dot_nt(q2, kwin) + jnp.concatenate([bias_ref[0], bias_ref[1]], axis=0)
    col = lax.broadcasted_iota(jnp.int32, (1, ATT_WINDOW), 1)
    s = jnp.where(col + (tb * ATT_QROWS - ATT_PAST) >= 0, s, NEG_MASK)
    p = jnp.exp(s - jnp.max(s, axis=-1, keepdims=True))
    o = _dot(p.astype(BF), vwin) / jnp.sum(p, axis=-1, keepdims=True)
    o_ref[...] = jnp.where(first, o[:ATT_QROWS], o[ATT_QROWS:])


def _attn_prompt(h, bias, bsz, t):
    rows = ATT_QROWS
    tbn = t // rows
    qc, kc, vc = OFF_Q_ATT // LANES, OFF_K_ATT // LANES, OFF_V_ATT // LANES
    band = jnp.full((ATT_HEADS, rows, ATT_WINDOW), NEG_MASK, F32)
    for c in range(rows // CHUNK):
        band = band.at[:, c * CHUNK:(c + 1) * CHUNK, c * CHUNK:c * CHUNK + ATT_BAND].set(bias)

    def past(col0, back):
        return pl.BlockSpec((rows, LANES), lambda b, hp, tb: (b * tbn + jnp.maximum(tb - back, 0), col0 + hp))

    return pl.pallas_call(
        _attn_prompt_kernel,
        out_shape=jax.ShapeDtypeStruct((bsz * t, ATT_WIDTH), F32),
        grid=(bsz, ATT_HEADS // 2, tbn),
        in_specs=[past(qc, 0), past(kc, 2), past(kc, 1), past(kc, 0), past(vc, 2), past(vc, 1), past(vc, 0),
                  pl.BlockSpec((2, rows, ATT_WINDOW), lambda b, hp, tb: (hp, 0, 0))],
        out_specs=pl.BlockSpec((rows, LANES), lambda b, hp, tb: (b * tbn + tb, hp)),
        compiler_params=_params(("arbitrary", "arbitrary", "arbitrary")),
        name="attn_prompt",
    )(h, h, h, h, h, h, h, band)


def _attn_sample_kernel(q_ref, k_ref, v_ref, ck_ref, cv_ref, bias_ref, o_ref):
    kwin = jnp.concatenate([ck_ref[0], k_ref[...]], axis=0).astype(BF)
    vwin = jnp.concatenate([cv_ref[0], v_ref[...]], axis=0).astype(BF)
    q = q_ref[...] * (ATT_HEAD_DIM ** -0.5)
    o_ref[...] = _attend_chunk(q, kwin, vwin, bias_ref, None)


def _attn_sample(h, cache_k, cache_v, bias, bsz):
    qc, kc, vc = OFF_Q_ATT // LANES, OFF_K_ATT // LANES, OFF_V_ATT // LANES
    new = lambda col0: pl.BlockSpec((CHUNK, LANES), lambda b, hp: (b, col0 + hp))
    old = pl.BlockSpec((1, ATT_PAST, LANES), lambda b, hp: (b, 0, hp))
    return pl.pallas_call(
        _attn_sample_kernel,
        out_shape=jax.ShapeDtypeStruct((bsz * CHUNK, ATT_WIDTH), F32),
        grid=(bsz, ATT_HEADS // 2),
        in_specs=[new(qc), new(kc), new(vc), old, old,
                  pl.BlockSpec((2, CHUNK, ATT_BAND), lambda b, hp: (hp, 0, 0))],
        out_specs=pl.BlockSpec((CHUNK, LANES), lambda b, hp: (b, hp)),
        compiler_params=_params(("arbitrary", "arbitrary")),
        name="attn_sample",
    )(h, h, h, cache_k, cache_v, bias)


def _rel_bias_tile(table):
    dist = jnp.arange(-(CHUNK - 1), ATT_BAND)
    per_dist = table[jnp.clip(dist, -REL_CLIP, REL_CLIP) + REL_CLIP].astype(F32).T
    rev = per_dist[:, ::-1]
    n = dist.shape[0]
    return jnp.stack([rev[:, CHUNK - 1 - i:n - i] for i in range(CHUNK)], axis=1)


def _ret_kernel(q_ref, k_ref, v_ref, g_ref, cos_ref, sin_ref, intra_ref, qw_ref, kw_ref, dec_ref, bd_ref, gn_ref,
                s0_ref, y_ref, sf_ref, state, *, cpb):
    @pl.when(pl.program_id(1) == 0)
    def _():
        state[...] = s0_ref[0]

    lane = lax.broadcasted_iota(jnp.int32, (1, LANES), 1)
    first = lane < RET_DIM
    low_half = (lane % RET_DIM) < (RET_DIM // 2)

    for c in range(cpb):
        rows = slice(c * CHUNK, (c + 1) * CHUNK)
        cos = cos_ref[rows, :]
        sin = sin_ref[rows, :]

        def rope(x):
            swapped = jnp.where(low_half, pltpu.roll(x, LANES - RET_DIM // 2, 1), pltpu.roll(x, RET_DIM // 2, 1))
            return x * cos + swapped * sin

        for hp in range(RET_HEADS // 2):
            cols = slice(hp * LANES, (hp + 1) * LANES)
            q = rope(q_ref[rows, cols]) * (RET_DIM ** -0.5)
            k = rope(k_ref[rows, cols])
            kb = k.astype(BF)
            vb = v_ref[rows, cols].astype(BF)
            intra = []
            for hh in range(2):
                qm = jnp.where(first if hh == 0 else jnp.logical_not(first), q, 0.0).astype(BF)
                s = _dot_nt(qm, kb) * intra_ref[2 * hp + hh]
                intra.append(_dot(s.astype(BF), vb))
            s_prev = state[hp]
            o = jnp.where(first, intra[0], intra[1]) + _dot((q * qw_ref[hp]).astype(BF), s_prev.astype(BF))
            kv = _dot_tn((k * kw_ref[hp]).astype(BF), vb)
            state[hp] = dec_ref[hp] * s_prev + bd_ref[...] * kv
            sum_a = jnp.sum(jnp.where(first, o, 0.0), axis=-1, keepdims=True)
            sum_b = jnp.sum(o, axis=-1, keepdims=True) - sum_a
            d = o - jnp.where(first, sum_a, sum_b) * (1.0 / RET_DIM)
            sq = d * d
            var_a = jnp.sum(jnp.where(first, sq, 0.0), axis=-1, keepdims=True)
            var_b = jnp.sum(sq, axis=-1, keepdims=True) - var_a
            var = jnp.where(first, var_a, var_b) * (1.0 / RET_DIM)
            g = g_ref[rows, cols]
            y_ref[rows, cols] = d * lax.rsqrt(var + LN_EPS) * gn_ref[:, cols] * (g * jax.nn.sigmoid(g))
    sf_ref[0] = state[...]


def _retention(h, s0, rt, gn, bsz, t, cpb):
    rows = cpb * CHUNK
    tbn = t // rows
    width = RET_HEADS * RET_DIM
    col = lambda off: pl.BlockSpec((rows, width), lambda b, tb: (b * tbn + tb, off // width))
    tab = pl.BlockSpec((rows, LANES), lambda b, tb: (tb, 0))
    const3 = lambda shape: pl.BlockSpec(shape, lambda b, tb: (0, 0, 0))
    st = pl.BlockSpec((1, 2, LANES, LANES), lambda b, tb: (b, 0, 0, 0))
    return pl.pallas_call(
        functools.partial(_ret_kernel, cpb=cpb),
        out_shape=(jax.ShapeDtypeStruct((bsz * t, width), F32),
                   jax.ShapeDtypeStruct((bsz, 2, LANES, LANES), F32)),
        grid=(bsz, tbn),
        in_specs=[col(OFF_Q_RET), col(OFF_K_RET), col(OFF_V_RET), col(OFF_G_RET), tab, tab,
                  const3((RET_HEADS, CHUNK, CHUNK)), const3((2, CHUNK, LANES)), const3((2, CHUNK, LANES)),
                  const3((2, LANES, LANES)), pl.BlockSpec((LANES, LANES), lambda b, tb: (0, 0)),
                  pl.BlockSpec((1, width), lambda b, tb: (0, 0)), st],
        out_specs=(pl.BlockSpec((rows, width), lambda b, tb: (b * tbn + tb, 0)), st),
        scratch_shapes=[pltpu.VMEM((2, LANES, LANES), F32)],
        compiler_params=_params(("arbitrary", "arbitrary")),
        name="retention",
    )(h, h, h, h, rt["cos"], rt["sin"], rt["intra"], rt["qw"], rt["kw"], rt["dec"], rt["bd"], gn, s0)


def _retention_tables(pos):
    lg = jnp.log1p(-(2.0 ** (-5.0 - jnp.arange(RET_HEADS, dtype=F32))))
    i = jnp.arange(CHUNK, dtype=F32)
    intra = jnp.exp(lg[:, None, None] * jnp.abs(i[:, None] - i[None, :]))
    lane_head = jnp.arange(2 * LANES) // RET_DIM
    lg_lane = lg[lane_head]
    qw = jnp.exp(lg_lane[None, :] * (i[:, None] + 1.0))
    kw = jnp.exp(lg_lane[None, :] * (CHUNK - 1.0 - i)[:, None])
    pair = lambda m: jnp.stack([m[:, :LANES], m[:, LANES:]])
    same_head = (jnp.arange(LANES)[:, None] // RET_DIM) == (jnp.arange(LANES)[None, :] // RET_DIM)
    bd = same_head.astype(F32)
    dec_lane = jnp.exp(lg_lane * CHUNK)
    dec = jnp.stack([bd * dec_lane[:LANES][:, None], bd * dec_lane[LANES:][:, None]])
    half = RET_DIM // 2
    freqs = ROPE_BASE ** (-jnp.arange(half, dtype=F32) / half)
    ang = pos.astype(F32)[:, None] * freqs[None]
    cos = jnp.tile(jnp.cos(ang), (1, LANES // half))
    sin = jnp.tile(jnp.concatenate([-jnp.sin(ang), jnp.sin(ang)], axis=1), (1, LANES // RET_DIM))
    return {"intra": intra, "qw": pair(qw), "kw": pair(kw), "dec": dec, "bd": bd, "cos": cos, "sin": sin}


def _pair_states(s):
    b = s.shape[0]
    s = s.astype(F32).reshape(b, 2, 2, RET_DIM, RET_DIM)
    out = jnp.zeros((b, 2, 2, RET_DIM, 2, RET_DIM), F32)
    out = out.at[:, :, 0, :, 0, :].set(s[:, :, 0]).at[:, :, 1, :, 1, :].set(s[:, :, 1])
    return out.reshape(b, 2, LANES, LANES)


def _unpair_states(s):
    b = s.shape[0]
    s = s.reshape(b, 2, 2, RET_DIM, 2, RET_DIM)
    return jnp.stack([s[:, :, 0, :, 0, :], s[:, :, 1, :, 1, :]], axis=2).reshape(b, RET_HEADS, RET_DIM, RET_DIM)


def _merge_kernel(x_ref, g1_ref, g2_ref, g3_ref, ys_ref, ya_ref, yr_ref, ws_ref, wa_ref, wr_ref, wo_ref,
                  lg_ref, lb_ref, o_ref):
    merged = (jax.nn.sigmoid(g1_ref[...]) * _dot(ys_ref[...].astype(BF), ws_ref[...])
              + jax.nn.sigmoid(g2_ref[...]) * _dot(ya_ref[...].astype(BF), wa_ref[...])
              + jax.nn.sigmoid(g3_ref[...]) * _dot(yr_ref[...].astype(BF), wr_ref[...]))
    r = DN_ALPHA * x_ref[...] + _dot(merged.astype(BF), wo_ref[...])
    o_ref[...] = _layer_norm(r, lg_ref[...], lb_ref[...])


def _merge(x, h, y_ssm, y_att, y_ret, lw, tm):
    n = x.shape[0]
    row = lambda w: pl.BlockSpec((tm, w), lambda i: (i, 0))
    gate = lambda j: pl.BlockSpec((tm, D_MODEL), lambda i: (i, j))
    full = lambda a: pl.BlockSpec(a.shape, lambda i: (0, 0))
    ws = (lw["w_br_ssm"], lw["w_br_att"], lw["w_br_ret"], lw["w_o"], lw["ln1_g"], lw["ln1_b"])
    return pl.pallas_call(
        _merge_kernel,
        out_shape=jax.ShapeDtypeStruct((n, D_MODEL), F32),
        grid=(n // tm,),
        in_specs=[row(D_MODEL), gate(0), gate(1), gate(2), row(SSM_WIDTH), row(ATT_WIDTH),
                  row(RET_HEADS * RET_DIM)] + [full(a) for a in ws],
        out_specs=row(D_MODEL),
        compiler_params=_params(("arbitrary",)),
        name="merge",
    )(x, h, h, h, y_ssm, y_att, y_ret, *ws)


CAND_ROWS = 80


def _cand_flat_index():
    f = [b for b in range(16)]
    for a in range(1, 8):
        f += [a * 16 + b for b in range(8)]
    f += [a * 16 for a in range(8, 16)]
    return np.asarray(f, np.float32)


def _split_bf16(x):
    hi = x.astype(BF)
    return hi, (x - hi.astype(F32)).astype(BF)


def _take16(work, order):
    t = work.shape[1]
    row16 = lax.broadcasted_iota(jnp.int32, (PEER_TOPK, t), 0)
    rank = jnp.full(work.shape, float(PEER_TOPK), F32)
    vals = jnp.zeros((PEER_TOPK, t), F32)
    for r in range(PEER_TOPK):
        m = jnp.max(work, axis=0, keepdims=True)
        first = jnp.min(jnp.where(work == m, order, 1e9), axis=0, keepdims=True)
        sel = order == first
        rank = jnp.where(sel, float(r), rank)
        work = jnp.where(sel, -jnp.inf, work)
        vals = jnp.where(row16 == r, m, vals)
    return rank, vals


def _route_head_exact(s1, s2, fidx, outs, h):
    r2_ref, e2_ref, c1_ref, e1_ref = outs
    key_idx = lax.broadcasted_iota(jnp.int32, s1.shape, 0).astype(F32)
    rank1, v1 = _take16(s1, key_idx)
    rank2, v2 = _take16(s2, key_idx)
    blocks = [v1[0:1] + v2]
    for a in range(1, 8):
        blocks.append(v1[a:a + 1] + v2[0:8])
    blocks.append(v1[8:16] + v2[0:1])
    cand = jnp.concatenate(blocks, axis=0)
    rank_c, _ = _take16(cand, fidx)
    chosen = jnp.where(rank_c < float(PEER_TOPK), 1.0, 0.0)
    counts = [jnp.sum(chosen[0:16], axis=0, keepdims=True)]
    for a in range(1, 8):
        counts.append(jnp.sum(chosen[8 + 8 * a:16 + 8 * a], axis=0, keepdims=True))
    for a in range(8, 16):
        counts.append(chosen[64 + a:65 + a])
    c1 = jnp.zeros(s1.shape, F32)
    for a in range(PEER_TOPK):
        c1 = jnp.where(rank1 == float(a), counts[a], c1)
    z = jnp.sum(chosen * jnp.exp(cand - cand[0:1]), axis=0, keepdims=True)
    r2_ref[h] = rank2.astype(BF)
    e2_ref[h] = jnp.exp(s2 - v2[0:1]).astype(BF)
    c1_ref[h] = c1
    e1_ref[h] = jnp.exp(s1 - v1[0:1]) / z


def _batcher_pairs(n):
    pairs = []

    def merge(lo, m, r):
        step = r * 2
        if step < m:
            merge(lo, m, step)
            merge(lo + r, m, step)
            pairs.extend((i, i + r) for i in range(lo + r, lo + m - r, step))
        else:
            pairs.append((lo, lo + r))

    def sort(lo, m):
        if m > 1:
            sort(lo, m // 2)
            sort(lo + m // 2, m // 2)
            merge(lo, m, 1)

    sort(0, n)
    return pairs


SORT16_PAIRS = _batcher_pairs(PEER_TOPK)


def _exchange(x, i, j):
    a, b = x[i], x[j]
    if b is None:
        return
    if a is None:
        x[i], x[j] = b, None
    else:
        x[i], x[j] = jnp.maximum(a, b), jnp.minimum(a, b)


def _top16_values(tiles):
    x = list(tiles) + [None] * (PEER_TOPK - len(tiles))
    for i, j in SORT16_PAIRS:
        _exchange(x, i, j)
    for shift in (4, 2, 1):
        y = [None if v is None else pltpu.roll(v, shift, 0) for v in x]
        z = []
        for i in range(PEER_TOPK):
            a, b = x[i], y[PEER_TOPK - 1 - i]
            z.append(b if a is None else a if b is None else jnp.maximum(a, b))
        for stride in (8, 4, 2, 1):
            for i in range(PEER_TOPK):
                if i & stride == 0:
                    _exchange(z, i, i + stride)
        x = z
    return x


def _count_greater(v, x):
    b3 = v[7] > x
    b2 = jnp.where(b3, v[11], v[3]) > x
    b1 = jnp.where(b3, jnp.where(b2, v[13], v[9]), jnp.where(b2, v[5], v[1])) > x
    b0 = jnp.where(b3, jnp.where(b2, jnp.where(b1, v[14], v[12]), jnp.where(b1, v[10], v[8])),
                   jnp.where(b2, jnp.where(b1, v[6], v[4]), jnp.where(b1, v[2], v[0]))) > x
    lo = (jnp.where(b3, 8.0, 0.0) + jnp.where(b2, 4.0, 0.0)) + (jnp.where(b1, 2.0, 0.0) + jnp.where(b0, 1.0, 0.0))
    return jnp.where(v[15] > x, float(PEER_TOPK), lo)


def _column_sum(tiles):
    parts = list(tiles)
    while len(parts) > 1:
        parts = [parts[i] + parts[i + 1] for i in range(0, len(parts) - 1, 2)] + parts[len(parts) & ~1:]
    total = parts[0]
    for shift in (4, 2, 1):
        total = total + pltpu.roll(total, shift, 0)
    return total


def _route_head_fast(s1, s2, outs, h, cols):
    r2_ref, e2_ref, c1_ref, e1_ref = outs
    n_tiles = N_KEYS // SUBLANES
    t1 = [s1[i * SUBLANES:(i + 1) * SUBLANES] for i in range(n_tiles)]
    t2 = [s2[i * SUBLANES:(i + 1) * SUBLANES] for i in range(n_tiles)]
    v1 = _top16_values(t1)
    v2 = _top16_values(t2)
    sub = lax.broadcasted_iota(jnp.int32, t1[0].shape, 0)

    def by_sublane(vals):
        out = vals[0]
        for b in range(1, SUBLANES):
            out = jnp.where(sub == b, vals[b], out)
        return out

    v2_lo, v2_hi, v1_hi = by_sublane(v2[0:8]), by_sublane(v2[8:16]), by_sublane(v1[8:16])
    cand = [v1[0] + v2_lo, v1[0] + v2_hi] + [v1[a] + v2_lo for a in range(1, 8)] + [v1_hi + v2[0]]
    tau = _top16_values(cand)[PEER_TOPK - 1]
    chosen = [jnp.where(c >= tau, 1.0, 0.0) for c in cand]
    counts = [_column_sum(chosen[0:2])] + [_column_sum([chosen[1 + a]]) for a in range(1, 8)]
    counts += [jnp.broadcast_to(chosen[9][a:a + 1], sub.shape) for a in range(SUBLANES)]
    z = _column_sum([ch * jnp.exp(c - cand[0][0:1]) for ch, c in zip(chosen, cand)])
    inv_z = 1.0 / z

    rank2 = [_count_greater(v2, x) for x in t2]
    c1 = []
    for x in t1:
        c = jnp.zeros_like(x)
        for a in range(PEER_TOPK):
            c = jnp.where(x == v1[a], counts[a], c)
        c1.append(c)
    r2_ref[h, :, cols] = jnp.concatenate(rank2, axis=0).astype(BF)
    e2_ref[h, :, cols] = jnp.concatenate([jnp.exp(x - v2[0]) for x in t2], axis=0).astype(BF)
    c1_ref[h, :, cols] = jnp.concatenate(c1, axis=0)
    e1_ref[h, :, cols] = jnp.concatenate([jnp.exp(x - v1[0]) * inv_z for x in t1], axis=0)

    no_tie_rank_sum = float(sum(range(PEER_TOPK)) + (N_KEYS - PEER_TOPK) * PEER_TOPK)
    bad = jnp.where(_column_sum(rank2) != no_tie_rank_sum, 1.0, 0.0)
    bad = bad + jnp.where(_column_sum([jnp.where(x >= v1[PEER_TOPK - 1], 1.0, 0.0) for x in t1]) != PEER_TOPK, 1.0, 0.0)
    for a in range(PEER_TOPK - 1):
        bad = bad + jnp.where(v1[a] == v1[a + 1], 1.0, 0.0)
    return bad + jnp.where(_column_sum(chosen) != PEER_TOPK, 1.0, 0.0)


def _route_kernel(x_ref, wh_ref, wl_ref, kh_ref, kl_ref, fidx_ref, r2_ref, e2_ref, c1_ref, e1_ref):
    x = x_ref[...]
    t = x.shape[0]
    xh, xl = _split_bf16(x)
    q = _dot(xh, wh_ref[...]) + (_dot(xl, wh_ref[...]) + _dot(xh, wl_ref[...]))
    outs = (r2_ref, e2_ref, c1_ref, e1_ref)

    def scores(hs):
        qh, ql = _split_bf16(q[:, hs * PEER_HALF:(hs + 1) * PEER_HALF])
        return _dot_nt(kh_ref[hs], qh) + (_dot_nt(kl_ref[hs], qh) + _dot_nt(kh_ref[hs], ql))

    all_scores = [scores(hs) for hs in range(2 * PEER_HEADS)]
    bad = []
    for h in range(PEER_HEADS):
        flags = [_route_head_fast(all_scores[2 * h][:, c * LANES:(c + 1) * LANES],
                                  all_scores[2 * h + 1][:, c * LANES:(c + 1) * LANES], outs, h,
                                  slice(c * LANES, (c + 1) * LANES)) for c in range(t // LANES)]
        bad.append(sum(flags[1:], flags[0]))

    for h in range(PEER_HEADS):
        @pl.when(jnp.max(bad[h]) > 0.0)
        def _(h=h):
            _route_head_exact(all_scores[2 * h], all_scores[2 * h + 1], fidx_ref[...], outs, h)


def _route(x1, lw, tb):
    n = x1.shape[0]
    fidx = jnp.broadcast_to(jnp.asarray(_cand_flat_index())[:, None], (CAND_ROWS, tb))
    out = jax.ShapeDtypeStruct((PEER_HEADS, N_KEYS, n), F32)
    ospec = pl.BlockSpec((PEER_HEADS, N_KEYS, tb), lambda i: (0, 0, i))
    out_bf = jax.ShapeDtypeStruct((PEER_HEADS, N_KEYS, n), BF)
    ospec_bf = ospec
    full2 = lambda a: pl.BlockSpec(a.shape, lambda i: (0, 0))
    full3 = lambda a: pl.BlockSpec(a.shape, lambda i: (0, 0, 0))
    return pl.pallas_call(
        _route_kernel,
        out_shape=(out_bf, out_bf, out, out),
        grid=(n // tb,),
        in_specs=[pl.BlockSpec((tb, D_MODEL), lambda i: (i, 0)), full2(lw["wq_hi"]), full2(lw["wq_lo"]),
                  full3(lw["keys_hi"]), full3(lw["keys_lo"]), full2(fidx)],
        out_specs=(ospec_bf, ospec_bf, ospec, ospec),
        compiler_params=_params(("arbitrary",)),
        name="peer_route",
    )(x1, lw["wq_hi"], lw["wq_lo"], lw["keys_hi"], lw["keys_lo"], fidx)


PEER_EXPERT_BLOCK = 1024


def _peer_kernel(x_ref, u_ref, vt_ref, r2_ref, e2_ref, c1_ref, e1_ref, lg_ref, lb_ref, o_ref, xb, acc, wt, rowbuf, act):
    s = pl.program_id(1)
    last = pl.num_programs(1) - 1
    tb = x_ref.shape[0]

    @pl.when(s == 0)
    def _():
        xb[...] = x_ref[...].T.astype(BF)
        acc[...] = jnp.zeros_like(acc)
        act[1] = jnp.zeros(act.shape[1:], BF)

    keys_per_step = PEER_EXPERT_BLOCK // N_KEYS
    first_key = pl.multiple_of(jnp.maximum(s - 1, 0) * keys_per_step, keys_per_step)

    for h in range(PEER_HEADS):
        for src, ref in enumerate((c1_ref, e1_ref)):
            blk = ref[h, pl.ds(first_key, keys_per_step), :]
            for j in range(keys_per_step):
                rowbuf[src, h, j] = jnp.broadcast_to(blk[j:j + 1], (SUBLANES, tb)).astype(BF)

    half = N_KEYS // 2
    group = keys_per_step // 2
    tiles = half // SUBLANES

    def spread(row):
        return jnp.broadcast_to(row[None], (tiles, SUBLANES, BF_CHUNK)).reshape(half, BF_CHUNK)

    for tc in range(tb // BF_CHUNK):
        cols = slice(tc * BF_CHUNK, (tc + 1) * BF_CHUNK)
        for jg in range(keys_per_step // group):
            for kh in range(2):
                keys = slice(kh * half, (kh + 1) * half)
                w = [jnp.zeros((half, BF_CHUNK), BF) for _ in range(group)]
                for h in range(PEER_HEADS):
                    r2 = r2_ref[h, keys, cols]
                    e2 = e2_ref[h, keys, cols]
                    for jj in range(group):
                        j = jg * group + jj
                        c1 = spread(rowbuf[0, h, j, :, cols])
                        e1 = spread(rowbuf[1, h, j, :, cols])
                        w[jj] = w[jj] + jnp.where(r2 < c1, e2 * e1, 0)
                for jj in range(group):
                    rows = slice((jg * group + jj) * N_KEYS + kh * half, (jg * group + jj) * N_KEYS + (kh + 1) * half)
                    wt[rows, cols] = w[jj]
    wt[...] = wt[...] * act[(s + 1) % 2]
    acc[...] += _dot(vt_ref[...], wt[...])
    act[s % 2] = jax.nn.gelu(_dot(u_ref[...], xb[...]).astype(BF))

    @pl.when(s == last)
    def _():
        r = DN_ALPHA * x_ref[...] + acc[...].T
        o_ref[...] = _layer_norm(r, lg_ref[...], lb_ref[...])


def _peer(x1, route, lw, tb):
    n = x1.shape[0]
    eb = PEER_EXPERT_BLOCK
    nblk = N_EXPERTS // eb
    rspec = pl.BlockSpec((PEER_HEADS, N_KEYS, tb), lambda i, e: (0, 0, i))
    rspec_bf = rspec
    vec = pl.BlockSpec((1, D_MODEL), lambda i, e: (0, 0))
    return pl.pallas_call(
        _peer_kernel,
        out_shape=jax.ShapeDtypeStruct((n, D_MODEL), F32),
        grid=(n // tb, nblk + 1),
        in_specs=[pl.BlockSpec((tb, D_MODEL), lambda i, s: (i, 0)),
                  pl.BlockSpec((eb, D_MODEL), lambda i, s: (jnp.minimum(s, nblk - 1), 0)),
                  pl.BlockSpec((D_MODEL, eb), lambda i, s: (0, jnp.maximum(s - 1, 0))),
                  rspec_bf, rspec_bf, rspec, rspec, vec, vec],
        out_specs=pl.BlockSpec((tb, D_MODEL), lambda i, s: (i, 0)),
        scratch_shapes=[pltpu.VMEM((D_MODEL, tb), BF), pltpu.VMEM((D_MODEL, tb), F32), pltpu.VMEM((eb, tb), BF),
                        pltpu.VMEM((2, PEER_HEADS, eb // N_KEYS, SUBLANES, tb), BF),
                        pltpu.VMEM((2, eb, tb), BF)],
        compiler_params=_params(("arbitrary", "arbitrary")),
        name="peer_experts",
    )(x1, lw["peer_u"], lw["peer_vt"], *route, lw["ln2_g"], lw["ln2_b"])


def _ple_kernel(x_ref, pe_ref, wg_ref, wp_ref, lg_ref, lb_ref, o_ref):
    x = x_ref[...]
    ple = jax.nn.sigmoid(_dot(x.astype(BF), wg_ref[...])) * _dot(pe_ref[...].astype(BF), wp_ref[...])
    o_ref[...] = _layer_norm(DN_ALPHA * x + ple, lg_ref[...], lb_ref[...])


def _ple(x2, pe, lw, tm):
    n = x2.shape[0]
    full = lambda a: pl.BlockSpec(a.shape, lambda i: (0, 0))
    ws = (lw["ple_w_g"], lw["ple_w_p"], lw["ln3_g"], lw["ln3_b"])
    return pl.pallas_call(
        _ple_kernel,
        out_shape=jax.ShapeDtypeStruct((n, D_MODEL), F32),
        grid=(n // tm,),
        in_specs=[pl.BlockSpec((tm, D_MODEL), lambda i: (i, 0)), pl.BlockSpec((tm, PLE_DIM), lambda i: (i, 0))]
        + [full(a) for a in ws],
        out_specs=pl.BlockSpec((tm, D_MODEL), lambda i: (i, 0)),
        compiler_params=_params(("arbitrary",)),
        name="ple",
    )(x2, pe, *ws)


def _time_major(u, bsz, t, bp):
    u = jnp.transpose(u.reshape(bsz, t, -1), (1, 0, 2))
    return jnp.pad(u, ((0, 0), (0, bp - bsz), (0, 0))).reshape(t * bp, -1)


def _batch_major(y, bsz, t, bp):
    return jnp.transpose(y.reshape(t, bp, -1)[:, :bsz], (1, 0, 2)).reshape(bsz * t, -1)


def _layer(x, pe, h0_re, h0_im, cache_k, cache_v, s0, pos, lw, bsz, t, prompt, n_rows):
    n = bsz * t
    h = _in_projection(x, lw["w_in"], 512, IN_COLS // 2)

    bp = max(8, bsz)
    u_tb = _time_major(h[:, OFF_U_SSM:OFF_U_SSM + SSM_WIDTH], bsz, t, bp)
    h0 = jnp.concatenate([h0_re.reshape(bsz, SSM_FLAT), h0_im.reshape(bsz, SSM_FLAT)], axis=1).astype(F32)
    h0 = jnp.pad(h0, ((0, bp - bsz), (0, 0)))
    y_tb, h_last = _s5_mixer(u_tb, h0, lw["s5"], t, bp, CHUNK)
    y_ssm = _batch_major(y_tb, bsz, t, bp)
    new_re = h_last[:bsz, :SSM_FLAT].reshape(bsz, SSM_GROUPS, SSM_STATE)
    new_im = h_last[:bsz, SSM_FLAT:].reshape(bsz, SSM_GROUPS, SSM_STATE)

    k_att = h[:, OFF_K_ATT:OFF_K_ATT + ATT_WIDTH].reshape(bsz, t, ATT_HEADS, ATT_HEAD_DIM)
    v_att = h[:, OFF_V_ATT:OFF_V_ATT + ATT_WIDTH].reshape(bsz, t, ATT_HEADS, ATT_HEAD_DIM)
    if prompt:
        y_att = _attn_prompt(h, lw["rel_bias"], bsz, t)
        new_k, new_v = k_att[:, -n_rows:], v_att[:, -n_rows:]
    else:
        r = cache_k.shape[1]
        y_att = _attn_sample(h, cache_k.reshape(bsz, r, ATT_WIDTH), cache_v.reshape(bsz, r, ATT_WIDTH),
                             lw["rel_bias"], bsz)
        new_k, new_v = k_att, v_att

    y_ret, s_pair = _retention(h, _pair_states(s0), _retention_tables(pos), lw["ret_gn_g"], bsz, t,
                               4 if prompt else 1)
    s_fin = _unpair_states(s_pair)

    tok = 512 if n % 512 == 0 else 256
    x1 = _merge(x, h, y_ssm, y_att, y_ret, lw, 256)
    route = _route(x1, lw, 256)
    x2 = _peer(x1, route, lw, tok)
    x3 = _ple(x2, pe, lw, 256)
    return x3, (new_re, new_im, new_k, new_v, s_fin)


def _layer_weights(li, w_in, ssm_lam_re, ssm_lam_im, ssm_b_re, ssm_b_im, ssm_c_re, ssm_c_im, ssm_log_dt, ssm_d,
                   ssm_w_glu, ssm_b_glu, att_rel_bias, ret_gn_g, w_br_ssm, w_br_att, w_br_ret, w_o, ln1_g, ln1_b,
                   peer_w_q, peer_sub_keys, peer_u, peer_v, ln2_g, ln2_b, ple_w_g, ple_w_p, ln3_g, ln3_b):
    vec = lambda a: a[li].astype(F32).reshape(1, -1)
    w = w_in[li]
    wq_hi, wq_lo = _split_bf16(peer_w_q[li].astype(F32))
    keys_hi, keys_lo = _split_bf16(peer_sub_keys[li].astype(F32).reshape(2 * PEER_HEADS, N_KEYS, PEER_HALF))
    return {
        "w_in": jnp.concatenate([w[:, OLD_GATE_START:], w[:, :OLD_GATE_START]], axis=1).astype(BF),
        "s5": _s5_tables(ssm_lam_re[li], ssm_lam_im[li], ssm_b_re[li], ssm_b_im[li], ssm_c_re[li], ssm_c_im[li],
                         ssm_log_dt[li], ssm_d[li], ssm_w_glu[li], ssm_b_glu[li]),
        "rel_bias": _rel_bias_tile(att_rel_bias[li]),
        "ret_gn_g": vec(ret_gn_g),
        "w_br_ssm": w_br_ssm[li].astype(BF), "w_br_att": w_br_att[li].astype(BF), "w_br_ret": w_br_ret[li].astype(BF),
        "w_o": w_o[li].astype(BF), "ln1_g": vec(ln1_g), "ln1_b": vec(ln1_b),
        "wq_hi": wq_hi, "wq_lo": wq_lo, "keys_hi": keys_hi, "keys_lo": keys_lo,
        "peer_u": peer_u[li].astype(BF), "peer_vt": peer_v[li].astype(BF).T,
        "ln2_g": vec(ln2_g), "ln2_b": vec(ln2_b),
        "ple_w_g": ple_w_g[li].astype(BF), "ple_w_p": ple_w_p[li].astype(BF), "ln3_g": vec(ln3_g), "ln3_b": vec(ln3_b),
    }


def kernel(x_prompt, x_sample, p_prompt, p_sample, state_ssm_re, state_ssm_im, cache_attn_k, cache_attn_v, state_ret, w_in, ssm_lam_re, ssm_lam_im, ssm_b_re, ssm_b_im, ssm_c_re, ssm_c_im, ssm_log_dt, ssm_d, ssm_w_glu, ssm_b_glu, att_rel_bias, ret_gn_g, w_br_ssm, w_br_att, w_br_ret, w_o, ln1_g, ln1_b, peer_w_q, peer_sub_keys, peer_u, peer_v, ln2_g, ln2_b, ple_w_g, ple_w_p, ln3_g, ln3_b):
    bp, tp, _ = x_prompt.shape
    bs, ts, _ = x_sample.shape
    n_rows = cache_attn_k.shape[2]
    pos_p = jnp.arange(tp)
    pos_s = PAST_LEN + jnp.arange(ts)
    y_p = x_prompt.reshape(bp * tp, D_MODEL).astype(F32)
    y_s = x_sample.reshape(bs * ts, D_MODEL).astype(F32)
    new_p, new_s = [], []
    for li in range(DEPTH):
        lw = _layer_weights(li, w_in, ssm_lam_re, ssm_lam_im, ssm_b_re, ssm_b_im, ssm_c_re, ssm_c_im, ssm_log_dt,
                            ssm_d, ssm_w_glu, ssm_b_glu, att_rel_bias, ret_gn_g, w_br_ssm, w_br_att, w_br_ret, w_o,
                            ln1_g, ln1_b, peer_w_q, peer_sub_keys, peer_u, peer_v, ln2_g, ln2_b, ple_w_g, ple_w_p,
                            ln3_g, ln3_b)
        zero_ssm = jnp.zeros((bp, SSM_GROUPS, SSM_STATE), F32)
        zero_ret = jnp.zeros((bp, RET_HEADS, RET_DIM, RET_DIM), F32)
        y_p, st = _layer(y_p, p_prompt[li].reshape(bp * tp, PLE_DIM), zero_ssm, zero_ssm, None, None, zero_ret,
                         pos_p, lw, bp, tp, True, n_rows)
        new_p.append(st)
        y_s, st = _layer(y_s, p_sample[li].reshape(bs * ts, PLE_DIM), state_ssm_re[li], state_ssm_im[li],
                         cache_attn_k[li], cache_attn_v[li], state_ret[li], pos_s, lw, bs, ts, False, n_rows)
        new_s.append(st)
    stack = lambda states, k: jnp.stack([s[k] for s in states])
    return (y_p.reshape(bp, tp, D_MODEL), y_s.reshape(bs, ts, D_MODEL),
            stack(new_p, 0), stack(new_p, 1), stack(new_p, 2), stack(new_p, 3), stack(new_p, 4),
            stack(new_s, 0), stack(new_s, 1), stack(new_s, 2), stack(new_s, 3), stack(new_s, 4))
```

```python
import functools

import jax
import jax.numpy as jnp
import numpy as np
from jax import lax
from jax.experimental import pallas as pl
from jax.experimental.pallas import tpu as pltpu

F32 = jnp.float32
BF = jnp.bfloat16

D_MODEL = 1024
DEPTH = 2
PAST_LEN = 2048
CHUNK = 64
PLE_DIM = 256
SSM_WIDTH = 256
SSM_GROUP = 16
SSM_GROUPS = 16
SSM_STATE = 64
SSM_FLAT = SSM_GROUPS * SSM_STATE
ATT_HEADS = 8
ATT_HEAD_DIM = 64
ATT_WIDTH = 512
ATT_PAST = 8 * CHUNK
ATT_BAND = ATT_PAST + CHUNK
REL_CLIP = 256
RET_HEADS = 4
RET_DIM = 64
ROPE_BASE = 10000.0
PEER_HEADS = 8
PEER_HALF = 128
N_KEYS = 128
N_EXPERTS = N_KEYS * N_KEYS
PEER_TOPK = 16
SUBLANES = 8
BF_CHUNK = 256
DN_ALPHA = (2 * DEPTH) ** 0.25
LN_EPS = 1e-5
LANES = 128
NEG_MASK = float(jnp.finfo(jnp.float32).min)

IN_COLS = 5888
OFF_GATES = 0
OFF_U_SSM = 3072
OFF_Q_ATT = 3328
OFF_K_ATT = 3840
OFF_V_ATT = 4352
OFF_Q_RET = 4864
OFF_K_RET = 5120
OFF_V_RET = 5376
OFF_G_RET = 5632
OLD_GATE_START = 2816

VMEM_LIMIT = 48 * 1024 * 1024


def _params(sem):
    return pltpu.CompilerParams(dimension_semantics=sem, vmem_limit_bytes=VMEM_LIMIT)


def _dot(a, b):
    return jnp.dot(a, b, preferred_element_type=F32)


def _dot_nt(a, b):
    return lax.dot_general(a, b, (((1,), (1,)), ((), ())), preferred_element_type=F32)


def _dot_tn(a, b):
    return lax.dot_general(a, b, (((0,), (0,)), ((), ())), preferred_element_type=F32)


def _layer_norm(r, g, b):
    mu = jnp.mean(r, axis=-1, keepdims=True)
    d = r - mu
    var = jnp.mean(d * d, axis=-1, keepdims=True)
    return d * lax.rsqrt(var + LN_EPS) * g + b


def _matmul_kernel(x_ref, w_ref, o_ref):
    o_ref[...] = _dot(x_ref[...].astype(BF), w_ref[...])


def _in_projection(x, w, tm, tn):
    m, k = x.shape
    n = w.shape[1]
    return pl.pallas_call(
        _matmul_kernel,
        out_shape=jax.ShapeDtypeStruct((m, n), F32),
        grid=(n // tn, m // tm),
        in_specs=[pl.BlockSpec((tm, k), lambda j, i: (i, 0)),
                  pl.BlockSpec((k, tn), lambda j, i: (0, j))],
        out_specs=pl.BlockSpec((tm, tn), lambda j, i: (i, j)),
        compiler_params=_params(("arbitrary", "arbitrary")),
        name="in_projection",
    )(x, w)


def _s5_kernel(u_ref, bm_ref, a_ref, cm_ref, d_ref, wg_ref, bg_ref, h0_ref, y_ref, hl_ref, hbuf, hst, *, steps, bp):
    @pl.when(pl.program_id(0) == 0)
    def _():
        hst[...] = h0_ref[...]

    u = u_ref[...]
    hbuf[...] = _dot(u.astype(BF), bm_ref[...])
    a_re = jnp.broadcast_to(a_ref[0:1, :], (bp, SSM_FLAT))
    a_im = jnp.broadcast_to(a_ref[1:2, :], (bp, SSM_FLAT))

    def step(t, carry):
        h_re, h_im = carry
        r0 = pl.multiple_of(t * bp, bp)
        n_re = a_re * h_re - a_im * h_im + hbuf[pl.ds(r0, bp), 0:SSM_FLAT]
        n_im = a_re * h_im + a_im * h_re + hbuf[pl.ds(r0, bp), SSM_FLAT:2 * SSM_FLAT]
        hbuf[pl.ds(r0, bp), 0:SSM_FLAT] = n_re
        hbuf[pl.ds(r0, bp), SSM_FLAT:2 * SSM_FLAT] = n_im
        return n_re, n_im

    h_re, h_im = lax.fori_loop(0, steps, step, (hst[:, 0:SSM_FLAT], hst[:, SSM_FLAT:2 * SSM_FLAT]))
    hst[:, 0:SSM_FLAT] = h_re
    hst[:, SSM_FLAT:2 * SSM_FLAT] = h_im
    hl_ref[...] = hst[...]
    y = _dot(hbuf[...].astype(BF), cm_ref[...]) + d_ref[...] * u
    z = jax.nn.gelu(y)
    y_ref[...] = z * jax.nn.sigmoid(_dot(z.astype(BF), wg_ref[...]) + bg_ref[...])


def _s5_mixer(u_tb, h0, sp, n_steps, bp, steps):
    rows = steps * bp
    kern = functools.partial(_s5_kernel, steps=steps, bp=bp)
    const = lambda c: (0, 0)
    return pl.pallas_call(
        kern,
        out_shape=(jax.ShapeDtypeStruct((n_steps * bp, SSM_WIDTH), F32),
                   jax.ShapeDtypeStruct((bp, 2 * SSM_FLAT), F32)),
        grid=(n_steps // steps,),
        in_specs=[pl.BlockSpec((rows, SSM_WIDTH), lambda c: (c, 0)),
                  pl.BlockSpec((SSM_WIDTH, 2 * SSM_FLAT), const),
                  pl.BlockSpec((2, SSM_FLAT), const),
                  pl.BlockSpec((2 * SSM_FLAT, SSM_WIDTH), const),
                  pl.BlockSpec((1, SSM_WIDTH), const),
                  pl.BlockSpec((SSM_WIDTH, SSM_WIDTH), const),
                  pl.BlockSpec((1, SSM_WIDTH), const),
                  pl.BlockSpec((bp, 2 * SSM_FLAT), const)],
        out_specs=(pl.BlockSpec((rows, SSM_WIDTH), lambda c: (c, 0)),
                   pl.BlockSpec((bp, 2 * SSM_FLAT), const)),
        scratch_shapes=[pltpu.VMEM((rows, 2 * SSM_FLAT), F32), pltpu.VMEM((bp, 2 * SSM_FLAT), F32)],
        compiler_params=_params(("arbitrary",)),
        name="s5_mixer",
    )(u_tb, sp["bmat"], sp["a"], sp["cmat"], sp["d"], sp["w_glu"], sp["b_glu"], h0)


def _s5_tables(lam_re, lam_im, b_re, b_im, c_re, c_im, log_dt, d_skip, w_glu, b_glu):
    lr, li = lam_re.astype(F32), lam_im.astype(F32)
    dt = jnp.exp(log_dt.astype(F32))[:, None]
    mag = jnp.exp(lr * dt)
    a_re, a_im = mag * jnp.cos(li * dt), mag * jnp.sin(li * dt)
    den = lr * lr + li * li
    k_re = ((a_re - 1.0) * lr + a_im * li) / den
    k_im = (a_im * lr - (a_re - 1.0) * li) / den
    br, bi = b_re.astype(F32), b_im.astype(F32)
    bbar_re = k_re[..., None] * br - k_im[..., None] * bi
    bbar_im = k_re[..., None] * bi + k_im[..., None] * br
    eye = jnp.eye(SSM_GROUPS, dtype=F32)

    def in_blockdiag(m):
        m = jnp.transpose(m, (0, 2, 1))
        return (m[:, :, None, :] * eye[:, None, :, None]).reshape(SSM_WIDTH, SSM_FLAT)

    def out_blockdiag(m):
        m = jnp.transpose(m, (0, 2, 1))
        return (m[:, :, None, :] * eye[:, None, :, None]).reshape(SSM_FLAT, SSM_WIDTH)

    bmat = jnp.concatenate([in_blockdiag(bbar_re), in_blockdiag(bbar_im)], axis=1)
    cmat = jnp.concatenate([out_blockdiag(c_re.astype(F32)), -out_blockdiag(c_im.astype(F32))], axis=0)
    a = jnp.stack([a_re.reshape(SSM_FLAT), a_im.reshape(SSM_FLAT)])
    return {"bmat": bmat.astype(BF), "cmat": cmat.astype(BF), "a": a,
            "d": d_skip.astype(F32).reshape(1, SSM_WIDTH), "w_glu": w_glu.astype(BF),
            "b_glu": b_glu.astype(F32).reshape(1, SSM_WIDTH)}


def _attend_chunk(q, kc, vc, bias_ref, valid):
    first = lax.broadcasted_iota(jnp.int32, (1, LANES), 1) < ATT_HEAD_DIM
    outs = []
    for hh in range(2):
        qm = jnp.where(first if hh == 0 else jnp.logical_not(first), q, 0.0).astype(BF)
        s = _dot_nt(qm, kc) + bias_ref[hh]
        if valid is not None:
            s = jnp.where(valid, s, NEG_MASK)
        m = jnp.max(s, axis=-1, keepdims=True)
        p = jnp.exp(s - m)
        l = jnp.sum(p, axis=-1, keepdims=True)
        outs.append(_dot(p.astype(BF), vc) / l)
    return jnp.where(first, outs[0], outs[1])


ATT_QROWS = 4 * CHUNK
ATT_WINDOW = ATT_PAST + ATT_QROWS


def _attn_prompt_kernel(q_ref, k0_ref, k1_ref, k2_ref, v0_ref, v1_ref, v2_ref, bias_ref, o_ref):
    tb = pl.program_id(2)
    kwin = jnp.concatenate([k0_ref[...], k1_ref[...], k2_ref[...]], axis=0).astype(BF)
    vwin = jnp.concatenate([v0_ref[...], v1_ref[...], v2_ref[...]], axis=0).astype(BF)
    first = lax.broadcasted_iota(jnp.int32, (1, LANES), 1) < ATT_HEAD_DIM
    q = q_ref[...] * (ATT_HEAD_DIM ** -0.5)
    q2 = jnp.concatenate([jnp.where(first, q, 0.0), jnp.where(first, 0.0, q)], axis=0).astype(BF)
    s = _dot_nt(q2, kwin) + jnp.concatenate([bias_ref[0], bias_ref[1]], axis=0)
    col = lax.broadcasted_iota(jnp.int32, (1, ATT_WINDOW), 1)
    s = jnp.where(col + (tb * ATT_QROWS - ATT_PAST) >= 0, s, NEG_MASK)
    p = jnp.exp(s - jnp.max(s, axis=-1, keepdims=True))
    o = _dot(p.astype(BF), vwin) / jnp.sum(p, axis=-1, keepdims=True)
    o_ref[...] = jnp.where(first, o[:ATT_QROWS], o[ATT_QROWS:])


def _attn_prompt(h, bias, bsz, t):
    rows = ATT_QROWS
    tbn = t // rows
    qc, kc, vc = OFF_Q_ATT // LANES, OFF_K_ATT // LANES, OFF_V_ATT // LANES
    band = jnp.concatenate(
        [jnp.pad(bias, ((0, 0), (0, 0), (c * CHUNK, ATT_WINDOW - ATT_BAND - c * CHUNK)), constant_values=NEG_MASK)
         for c in range(rows // CHUNK)], axis=1)

    def past(col0, back):
        return pl.BlockSpec((rows, LANES), lambda b, hp, tb: (b * tbn + jnp.maximum(tb - back, 0), col0 + hp))

    return pl.pallas_call(
        _attn_prompt_kernel,
        out_shape=jax.ShapeDtypeStruct((bsz * t, ATT_WIDTH), F32),
        grid=(bsz, ATT_HEADS // 2, tbn),
        in_specs=[past(qc, 0), past(kc, 2), past(kc, 1), past(kc, 0), past(vc, 2), past(vc, 1), past(vc, 0),
                  pl.BlockSpec((2, rows, ATT_WINDOW), lambda b, hp, tb: (hp, 0, 0))],
        out_specs=pl.BlockSpec((rows, LANES), lambda b, hp, tb: (b * tbn + tb, hp)),
        compiler_params=_params(("arbitrary", "arbitrary", "arbitrary")),
        name="attn_prompt",
    )(h, h, h, h, h, h, h, band)


def _attn_sample_kernel(q_ref, k_ref, v_ref, ck_ref, cv_ref, bias_ref, o_ref):
    kwin = jnp.concatenate([ck_ref[0], k_ref[...]], axis=0).astype(BF)
    vwin = jnp.concatenate([cv_ref[0], v_ref[...]], axis=0).astype(BF)
    q = q_ref[...] * (ATT_HEAD_DIM ** -0.5)
    o_ref[...] = _attend_chunk(q, kwin, vwin, bias_ref, None)


def _attn_sample(h, cache_k, cache_v, bias, bsz):
    qc, kc, vc = OFF_Q_ATT // LANES, OFF_K_ATT // LANES, OFF_V_ATT // LANES
    new = lambda col0: pl.BlockSpec((CHUNK, LANES), lambda b, hp: (b, col0 + hp))
    old = pl.BlockSpec((1, ATT_PAST, LANES), lambda b, hp: (b, 0, hp))
    return pl.pallas_call(
        _attn_sample_kernel,
        out_shape=jax.ShapeDtypeStruct((bsz * CHUNK, ATT_WIDTH), F32),
        grid=(bsz, ATT_HEADS // 2),
        in_specs=[new(qc), new(kc), new(vc), old, old,
                  pl.BlockSpec((2, CHUNK, ATT_BAND), lambda b, hp: (hp, 0, 0))],
        out_specs=pl.BlockSpec((CHUNK, LANES), lambda b, hp: (b, hp)),
        compiler_params=_params(("arbitrary", "arbitrary")),
        name="attn_sample",
    )(h, h, h, cache_k, cache_v, bias)


def _rel_bias_tile(table):
    dist = jnp.arange(-(CHUNK - 1), ATT_BAND)
    per_dist = table[jnp.clip(dist, -REL_CLIP, REL_CLIP) + REL_CLIP].astype(F32).T
    rev = per_dist[:, ::-1]
    n = dist.shape[0]
    return jnp.stack([rev[:, CHUNK - 1 - i:n - i] for i in range(CHUNK)], axis=1)


def _ret_kernel(q_ref, k_ref, v_ref, g_ref, cos_ref, sin_ref, intra_ref, qw_ref, kw_ref, dec_ref, bd_ref, gn_ref,
                s0_ref, y_ref, sf_ref, state, *, cpb):
    @pl.when(pl.program_id(1) == 0)
    def _():
        state[...] = s0_ref[0]

    lane = lax.broadcasted_iota(jnp.int32, (1, LANES), 1)
    first = lane < RET_DIM
    low_half = (lane % RET_DIM) < (RET_DIM // 2)

    for c in range(cpb):
        rows = slice(c * CHUNK, (c + 1) * CHUNK)
        cos = cos_ref[rows, :]
        sin = sin_ref[rows, :]

        def rope(x):
            swapped = jnp.where(low_half, pltpu.roll(x, LANES - RET_DIM // 2, 1), pltpu.roll(x, RET_DIM // 2, 1))
            return x * cos + swapped * sin

        for hp in range(RET_HEADS // 2):
            cols = slice(hp * LANES, (hp + 1) * LANES)
            q = rope(q_ref[rows, cols]) * (RET_DIM ** -0.5)
            k = rope(k_ref[rows, cols])
            kb = k.astype(BF)
            vb = v_ref[rows, cols].astype(BF)
            intra = []
            for hh in range(2):
                qm = jnp.where(first if hh == 0 else jnp.logical_not(first), q, 0.0).astype(BF)
                s = _dot_nt(qm, kb) * intra_ref[2 * hp + hh]
                intra.append(_dot(s.astype(BF), vb))
            s_prev = state[hp]
            o = jnp.where(first, intra[0], intra[1]) + _dot((q * qw_ref[hp]).astype(BF), s_prev.astype(BF))
            kv = _dot_tn((k * kw_ref[hp]).astype(BF), vb)
            state[hp] = dec_ref[hp] * s_prev + bd_ref[...] * kv
            sum_a = jnp.sum(jnp.where(first, o, 0.0), axis=-1, keepdims=True)
            sum_b = jnp.sum(o, axis=-1, keepdims=True) - sum_a
            d = o - jnp.where(first, sum_a, sum_b) * (1.0 / RET_DIM)
            sq = d * d
            var_a = jnp.sum(jnp.where(first, sq, 0.0), axis=-1, keepdims=True)
            var_b = jnp.sum(sq, axis=-1, keepdims=True) - var_a
            var = jnp.where(first, var_a, var_b) * (1.0 / RET_DIM)
            g = g_ref[rows, cols]
            y_ref[rows, cols] = d * lax.rsqrt(var + LN_EPS) * gn_ref[:, cols] * (g * jax.nn.sigmoid(g))
    sf_ref[0] = state[...]


def _retention(h, s0, rt, gn, bsz, t, cpb):
    rows = cpb * CHUNK
    tbn = t // rows
    width = RET_HEADS * RET_DIM
    col = lambda off: pl.BlockSpec((rows, width), lambda b, tb: (b * tbn + tb, off // width))
    tab = pl.BlockSpec((rows, LANES), lambda b, tb: (tb, 0))
    const3 = lambda shape: pl.BlockSpec(shape, lambda b, tb: (0, 0, 0))
    st = pl.BlockSpec((1, 2, LANES, LANES), lambda b, tb: (b, 0, 0, 0))
    return pl.pallas_call(
        functools.partial(_ret_kernel, cpb=cpb),
        out_shape=(jax.ShapeDtypeStruct((bsz * t, width), F32),
                   jax.ShapeDtypeStruct((bsz, 2, LANES, LANES), F32)),
        grid=(bsz, tbn),
        in_specs=[col(OFF_Q_RET), col(OFF_K_RET), col(OFF_V_RET), col(OFF_G_RET), tab, tab,
                  const3((RET_HEADS, CHUNK, CHUNK)), const3((2, CHUNK, LANES)), const3((2, CHUNK, LANES)),
                  const3((2, LANES, LANES)), pl.BlockSpec((LANES, LANES), lambda b, tb: (0, 0)),
                  pl.BlockSpec((1, width), lambda b, tb: (0, 0)), st],
        out_specs=(pl.BlockSpec((rows, width), lambda b, tb: (b * tbn + tb, 0)), st),
        scratch_shapes=[pltpu.VMEM((2, LANES, LANES), F32)],
        compiler_params=_params(("arbitrary", "arbitrary")),
        name="retention",
    )(h, h, h, h, rt["cos"], rt["sin"], rt["intra"], rt["qw"], rt["kw"], rt["dec"], rt["bd"], gn, s0)


def _retention_tables(pos):
    lg = jnp.log1p(-(2.0 ** (-5.0 - jnp.arange(RET_HEADS, dtype=F32))))
    i = jnp.arange(CHUNK, dtype=F32)
    intra = jnp.exp(lg[:, None, None] * jnp.abs(i[:, None] - i[None, :]))
    lane_head = jnp.arange(2 * LANES) // RET_DIM
    lg_lane = lg[lane_head]
    qw = jnp.exp(lg_lane[None, :] * (i[:, None] + 1.0))
    kw = jnp.exp(lg_lane[None, :] * (CHUNK - 1.0 - i)[:, None])
    pair = lambda m: jnp.stack([m[:, :LANES], m[:, LANES:]])
    same_head = (jnp.arange(LANES)[:, None] // RET_DIM) == (jnp.arange(LANES)[None, :] // RET_DIM)
    bd = same_head.astype(F32)
    dec_lane = jnp.exp(lg_lane * CHUNK)
    dec = jnp.stack([bd * dec_lane[:LANES][:, None], bd * dec_lane[LANES:][:, None]])
    half = RET_DIM // 2
    freqs = ROPE_BASE ** (-jnp.arange(half, dtype=F32) / half)
    ang = pos.astype(F32)[:, None] * freqs[None]
    cos = jnp.tile(jnp.cos(ang), (1, LANES // half))
    sin = jnp.tile(jnp.concatenate([-jnp.sin(ang), jnp.sin(ang)], axis=1), (1, LANES // RET_DIM))
    return {"intra": intra, "qw": pair(qw), "kw": pair(kw), "dec": dec, "bd": bd, "cos": cos, "sin": sin}


def _pair_states(s):
    b = s.shape[0]
    s = s.astype(F32).reshape(b, 2, 2, RET_DIM, RET_DIM)
    out = jnp.zeros((b, 2, 2, RET_DIM, 2, RET_DIM), F32)
    out = out.at[:, :, 0, :, 0, :].set(s[:, :, 0]).at[:, :, 1, :, 1, :].set(s[:, :, 1])
    return out.reshape(b, 2, LANES, LANES)


def _unpair_states(s):
    b = s.shape[0]
    s = s.reshape(b, 2, 2, RET_DIM, 2, RET_DIM)
    return jnp.stack([s[:, :, 0, :, 0, :], s[:, :, 1, :, 1, :]], axis=2).reshape(b, RET_HEADS, RET_DIM, RET_DIM)


def _merge_kernel(x_ref, g1_ref, g2_ref, g3_ref, ys_ref, ya_ref, yr_ref, ws_ref, wa_ref, wr_ref, wo_ref,
                  lg_ref, lb_ref, o_ref):
    merged = (jax.nn.sigmoid(g1_ref[...]) * _dot(ys_ref[...].astype(BF), ws_ref[...])
              + jax.nn.sigmoid(g2_ref[...]) * _dot(ya_ref[...].astype(BF), wa_ref[...])
              + jax.nn.sigmoid(g3_ref[...]) * _dot(yr_ref[...].astype(BF), wr_ref[...]))
    r = DN_ALPHA * x_ref[...] + _dot(merged.astype(BF), wo_ref[...])
    o_ref[...] = _layer_norm(r, lg_ref[...], lb_ref[...])


def _merge(x, h, y_ssm, y_att, y_ret, lw, tm):
    n = x.shape[0]
    row = lambda w: pl.BlockSpec((tm, w), lambda i: (i, 0))
    gate = lambda j: pl.BlockSpec((tm, D_MODEL), lambda i: (i, j))
    full = lambda a: pl.BlockSpec(a.shape, lambda i: (0, 0))
    ws = (lw["w_br_ssm"], lw["w_br_att"], lw["w_br_ret"], lw["w_o"], lw["ln1_g"], lw["ln1_b"])
    return pl.pallas_call(
        _merge_kernel,
        out_shape=jax.ShapeDtypeStruct((n, D_MODEL), F32),
        grid=(n // tm,),
        in_specs=[row(D_MODEL), gate(0), gate(1), gate(2), row(SSM_WIDTH), row(ATT_WIDTH),
                  row(RET_HEADS * RET_DIM)] + [full(a) for a in ws],
        out_specs=row(D_MODEL),
        compiler_params=_params(("arbitrary",)),
        name="merge",
    )(x, h, h, h, y_ssm, y_att, y_ret, *ws)


CAND_ROWS = 80


def _cand_flat_index():
    f = [b for b in range(16)]
    for a in range(1, 8):
        f += [a * 16 + b for b in range(8)]
    f += [a * 16 for a in range(8, 16)]
    return np.asarray(f, np.float32)


def _split_bf16(x):
    hi = x.astype(BF)
    return hi, (x - hi.astype(F32)).astype(BF)


def _take16(work, order):
    t = work.shape[1]
    row16 = lax.broadcasted_iota(jnp.int32, (PEER_TOPK, t), 0)
    rank = jnp.full(work.shape, float(PEER_TOPK), F32)
    vals = jnp.zeros((PEER_TOPK, t), F32)
    for r in range(PEER_TOPK):
        m = jnp.max(work, axis=0, keepdims=True)
        first = jnp.min(jnp.where(work == m, order, 1e9), axis=0, keepdims=True)
        sel = order == first
        rank = jnp.where(sel, float(r), rank)
        work = jnp.where(sel, -jnp.inf, work)
        vals = jnp.where(row16 == r, m, vals)
    return rank, vals


def _route_head_exact(s1, s2, fidx, outs, h):
    r2_ref, e2_ref, c1_ref, e1_ref = outs
    key_idx = lax.broadcasted_iota(jnp.int32, s1.shape, 0).astype(F32)
    rank1, v1 = _take16(s1, key_idx)
    rank2, v2 = _take16(s2, key_idx)
    blocks = [v1[0:1] + v2]
    for a in range(1, 8):
        blocks.append(v1[a:a + 1] + v2[0:8])
    blocks.append(v1[8:16] + v2[0:1])
    cand = jnp.concatenate(blocks, axis=0)
    rank_c, _ = _take16(cand, fidx)
    chosen = jnp.where(rank_c < float(PEER_TOPK), 1.0, 0.0)
    counts = [jnp.sum(chosen[0:16], axis=0, keepdims=True)]
    for a in range(1, 8):
        counts.append(jnp.sum(chosen[8 + 8 * a:16 + 8 * a], axis=0, keepdims=True))
    for a in range(8, 16):
        counts.append(chosen[64 + a:65 + a])
    c1 = jnp.zeros(s1.shape, F32)
    for a in range(PEER_TOPK):
        c1 = jnp.where(rank1 == float(a), counts[a], c1)
    z = jnp.sum(chosen * jnp.exp(cand - cand[0:1]), axis=0, keepdims=True)
    r2_ref[h] = rank2.astype(BF)
    e2_ref[h] = jnp.exp(s2 - v2[0:1]).astype(BF)
    c1_ref[h] = c1
    e1_ref[h] = jnp.exp(s1 - v1[0:1]) / z


def _batcher_pairs(n):
    pairs = []

    def merge(lo, m, r):
        step = r * 2
        if step < m:
            merge(lo, m, step)
            merge(lo + r, m, step)
            pairs.extend((i, i + r) for i in range(lo + r, lo + m - r, step))
        else:
            pairs.append((lo, lo + r))

    def sort(lo, m):
        if m > 1:
            sort(lo, m // 2)
            sort(lo + m // 2, m // 2)
            merge(lo, m, 1)

    sort(0, n)
    return pairs


SORT16_PAIRS = _batcher_pairs(PEER_TOPK)


def _exchange(x, i, j):
    a, b = x[i], x[j]
    if b is None:
        return
    if a is None:
        x[i], x[j] = b, None
    else:
        x[i], x[j] = jnp.maximum(a, b), jnp.minimum(a, b)


def _top16_values(tiles):
    x = list(tiles) + [None] * (PEER_TOPK - len(tiles))
    for i, j in SORT16_PAIRS:
        _exchange(x, i, j)
    for shift in (4, 2, 1):
        y = [None if v is None else pltpu.roll(v, shift, 0) for v in x]
        z = []
        for i in range(PEER_TOPK):
            a, b = x[i], y[PEER_TOPK - 1 - i]
            z.append(b if a is None else a if b is None else jnp.maximum(a, b))
        for stride in (8, 4, 2, 1):
            for i in range(PEER_TOPK):
                if i & stride == 0:
                    _exchange(z, i, i + stride)
        x = z
    return x


def _count_greater(v, x):
    b3 = v[7] > x
    b2 = jnp.where(b3, v[11], v[3]) > x
    b1 = jnp.where(b3, jnp.where(b2, v[13], v[9]), jnp.where(b2, v[5], v[1])) > x
    b0 = jnp.where(b3, jnp.where(b2, jnp.where(b1, v[14], v[12]), jnp.where(b1, v[10], v[8])),
                   jnp.where(b2, jnp.where(b1, v[6], v[4]), jnp.where(b1, v[2], v[0]))) > x
    lo = (jnp.where(b3, 8.0, 0.0) + jnp.where(b2, 4.0, 0.0)) + (jnp.where(b1, 2.0, 0.0) + jnp.where(b0, 1.0, 0.0))
    return jnp.where(v[15] > x, float(PEER_TOPK), lo)


def _column_sum(tiles):
    parts = list(tiles)
    while len(parts) > 1:
        parts = [parts[i] + parts[i + 1] for i in range(0, len(parts) - 1, 2)] + parts[len(parts) & ~1:]
    total = parts[0]
    for shift in (4, 2, 1):
        total = total + pltpu.roll(total, shift, 0)
    return total


def _route_head_fast(s1, s2, outs, h, cols):
    r2_ref, e2_ref, c1_ref, e1_ref = outs
    n_tiles = N_KEYS // SUBLANES
    t1 = [s1[i * SUBLANES:(i + 1) * SUBLANES] for i in range(n_tiles)]
    t2 = [s2[i * SUBLANES:(i + 1) * SUBLANES] for i in range(n_tiles)]
    v1 = _top16_values(t1)
    v2 = _top16_values(t2)
    sub = lax.broadcasted_iota(jnp.int32, t1[0].shape, 0)

    def by_sublane(vals):
        out = vals[0]
        for b in range(1, SUBLANES):
            out = jnp.where(sub == b, vals[b], out)
        return out

    v2_lo, v2_hi, v1_hi = by_sublane(v2[0:8]), by_sublane(v2[8:16]), by_sublane(v1[8:16])
    cand = [v1[0] + v2_lo, v1[0] + v2_hi] + [v1[a] + v2_lo for a in range(1, 8)] + [v1_hi + v2[0]]
    tau = _top16_values(cand)[PEER_TOPK - 1]
    chosen = [jnp.where(c >= tau, 1.0, 0.0) for c in cand]
    counts = [_column_sum(chosen[0:2])] + [_column_sum([chosen[1 + a]]) for a in range(1, 8)]
    counts += [jnp.broadcast_to(chosen[9][a:a + 1], sub.shape) for a in range(SUBLANES)]
    z = _column_sum([ch * jnp.exp(c - cand[0][0:1]) for ch, c in zip(chosen, cand)])
    inv_z = 1.0 / z

    rank2 = [_count_greater(v2, x) for x in t2]
    c1 = []
    for x in t1:
        c = jnp.zeros_like(x)
        for a in range(PEER_TOPK):
            c = jnp.where(x == v1[a], counts[a], c)
        c1.append(c)
    r2_ref[h, :, cols] = jnp.concatenate(rank2, axis=0).astype(BF)
    e2_ref[h, :, cols] = jnp.concatenate([jnp.exp(x - v2[0]) for x in t2], axis=0).astype(BF)
    c1_ref[h, :, cols] = jnp.concatenate(c1, axis=0)
    e1_ref[h, :, cols] = jnp.concatenate([jnp.exp(x - v1[0]) * inv_z for x in t1], axis=0)

    no_tie_rank_sum = float(sum(range(PEER_TOPK)) + (N_KEYS - PEER_TOPK) * PEER_TOPK)
    bad = jnp.where(_column_sum(rank2) != no_tie_rank_sum, 1.0, 0.0)
    bad = bad + jnp.where(_column_sum([jnp.where(x >= v1[PEER_TOPK - 1], 1.0, 0.0) for x in t1]) != PEER_TOPK, 1.0, 0.0)
    for a in range(PEER_TOPK - 1):
        bad = bad + jnp.where(v1[a] == v1[a + 1], 1.0, 0.0)
    return bad + jnp.where(_column_sum(chosen) != PEER_TOPK, 1.0, 0.0)


def _route_kernel(x_ref, wh_ref, wl_ref, kh_ref, kl_ref, fidx_ref, r2_ref, e2_ref, c1_ref, e1_ref):
    x = x_ref[...]
    t = x.shape[0]
    xh, xl = _split_bf16(x)
    q = _dot(xh, wh_ref[...]) + (_dot(xl, wh_ref[...]) + _dot(xh, wl_ref[...]))
    outs = (r2_ref, e2_ref, c1_ref, e1_ref)

    def scores(hs):
        qh, ql = _split_bf16(q[:, hs * PEER_HALF:(hs + 1) * PEER_HALF])
        return _dot_nt(kh_ref[hs], qh) + (_dot_nt(kl_ref[hs], qh) + _dot_nt(kh_ref[hs], ql))

    all_scores = [scores(hs) for hs in range(2 * PEER_HEADS)]
    bad = []
    for h in range(PEER_HEADS):
        flags = [_route_head_fast(all_scores[2 * h][:, c * LANES:(c + 1) * LANES],
                                  all_scores[2 * h + 1][:, c * LANES:(c + 1) * LANES], outs, h,
                                  slice(c * LANES, (c + 1) * LANES)) for c in range(t // LANES)]
        bad.append(sum(flags[1:], flags[0]))

    for h in range(PEER_HEADS):
        @pl.when(jnp.max(bad[h]) > 0.0)
        def _(h=h):
            _route_head_exact(all_scores[2 * h], all_scores[2 * h + 1], fidx_ref[...], outs, h)


def _route(x1, lw, tb):
    n = x1.shape[0]
    fidx = jnp.broadcast_to(jnp.asarray(_cand_flat_index())[:, None], (CAND_ROWS, tb))
    out = jax.ShapeDtypeStruct((PEER_HEADS, N_KEYS, n), F32)
    ospec = pl.BlockSpec((PEER_HEADS, N_KEYS, tb), lambda i: (0, 0, i))
    out_bf = jax.ShapeDtypeStruct((PEER_HEADS, N_KEYS, n), BF)
    ospec_bf = ospec
    full2 = lambda a: pl.BlockSpec(a.shape, lambda i: (0, 0))
    full3 = lambda a: pl.BlockSpec(a.shape, lambda i: (0, 0, 0))
    return pl.pallas_call(
        _route_kernel,
        out_shape=(out_bf, out_bf, out, out),
        grid=(n // tb,),
        in_specs=[pl.BlockSpec((tb, D_MODEL), lambda i: (i, 0)), full2(lw["wq_hi"]), full2(lw["wq_lo"]),
                  full3(lw["keys_hi"]), full3(lw["keys_lo"]), full2(fidx)],
        out_specs=(ospec_bf, ospec_bf, ospec, ospec),
        compiler_params=_params(("arbitrary",)),
        name="peer_route",
    )(x1, lw["wq_hi"], lw["wq_lo"], lw["keys_hi"], lw["keys_lo"], fidx)


PEER_EXPERT_BLOCK = 1024


PEER_ACT_BLOCK = 2048


def _peer_act_kernel(x_ref, u_ref, o_ref):
    xb = x_ref[...].astype(BF)
    sub = 512
    for k in range(u_ref.shape[0] // sub):
        o_ref[k * sub:(k + 1) * sub, :] = jax.nn.gelu(_dot_nt(u_ref[k * sub:(k + 1) * sub, :], xb).astype(BF))


def _peer_act(x1, u, tb):
    n = x1.shape[0]
    eb = PEER_ACT_BLOCK
    return pl.pallas_call(
        _peer_act_kernel,
        out_shape=jax.ShapeDtypeStruct((N_EXPERTS, n), BF),
        grid=(n // tb, N_EXPERTS // eb),
        in_specs=[pl.BlockSpec((tb, D_MODEL), lambda i, e: (i, 0)),
                  pl.BlockSpec((eb, D_MODEL), lambda i, e: (e, 0))],
        out_specs=pl.BlockSpec((eb, tb), lambda i, e: (e, i)),
        compiler_params=_params(("arbitrary", "arbitrary")),
        name="peer_act",
    )(x1, u)


def _peer_kernel(x_ref, a_ref, vt_ref, r2_ref, e2_ref, c1_ref, e1_ref, lg_ref, lb_ref, o_ref, acc, wt, rowbuf):
    s = pl.program_id(1)
    last = pl.num_programs(1) - 1
    tb = x_ref.shape[0]

    @pl.when(s == 0)
    def _():
        acc[...] = jnp.zeros_like(acc)

    keys_per_step = PEER_EXPERT_BLOCK // N_KEYS
    first_key = pl.multiple_of(s * keys_per_step, keys_per_step)

    for h in range(PEER_HEADS):
        for src, ref in enumerate((c1_ref, e1_ref)):
            blk = ref[h, pl.ds(first_key, keys_per_step), :]
            for j in range(keys_per_step):
                rowbuf[src, h, j] = jnp.broadcast_to(blk[j:j + 1], (SUBLANES, tb)).astype(BF)

    half = N_KEYS // 2
    group = keys_per_step // 2
    tiles = half // SUBLANES

    def spread(row):
        return jnp.broadcast_to(row[None], (tiles, SUBLANES, BF_CHUNK)).reshape(half, BF_CHUNK)

    for tc in range(tb // BF_CHUNK):
        cols = slice(tc * BF_CHUNK, (tc + 1) * BF_CHUNK)
        for jg in range(keys_per_step // group):
            for kh in range(2):
                keys = slice(kh * half, (kh + 1) * half)
                w = [jnp.zeros((half, BF_CHUNK), BF) for _ in range(group)]
                for h in range(PEER_HEADS):
                    r2 = r2_ref[h, keys, cols]
                    e2 = e2_ref[h, keys, cols]
                    for jj in range(group):
                        j = jg * group + jj
                        c1 = spread(rowbuf[0, h, j, :, cols])
                        e1 = spread(rowbuf[1, h, j, :, cols])
                        w[jj] = w[jj] + jnp.where(r2 < c1, e2 * e1, 0)
                for jj in range(group):
                    rows = slice((jg * group + jj) * N_KEYS + kh * half, (jg * group + jj) * N_KEYS + (kh + 1) * half)
                    wt[rows, cols] = w[jj] * a_ref[rows, cols]
    acc[...] += _dot(vt_ref[...], wt[...])

    @pl.when(s == last)
    def _():
        r = DN_ALPHA * x_ref[...] + acc[...].T
        o_ref[...] = _layer_norm(r, lg_ref[...], lb_ref[...])


def _peer(x1, route, lw, tb):
    n = x1.shape[0]
    eb = PEER_EXPERT_BLOCK
    act = _peer_act(x1, lw["peer_u"], tb)
    rspec = pl.BlockSpec((PEER_HEADS, N_KEYS, tb), lambda i, e: (0, 0, i))
    vec = pl.BlockSpec((1, D_MODEL), lambda i, e: (0, 0))
    return pl.pallas_call(
        _peer_kernel,
        out_shape=jax.ShapeDtypeStruct((n, D_MODEL), F32),
        grid=(n // tb, N_EXPERTS // eb),
        in_specs=[pl.BlockSpec((tb, D_MODEL), lambda i, s: (i, 0)),
                  pl.BlockSpec((eb, tb), lambda i, s: (s, i)),
                  pl.BlockSpec((D_MODEL, eb), lambda i, s: (0, s)),
                  rspec, rspec, rspec, rspec, vec, vec],
        out_specs=pl.BlockSpec((tb, D_MODEL), lambda i, s: (i, 0)),
        scratch_shapes=[pltpu.VMEM((D_MODEL, tb), F32), pltpu.VMEM((eb, tb), BF),
                        pltpu.VMEM((2, PEER_HEADS, eb // N_KEYS, SUBLANES, tb), BF)],
        compiler_params=_params(("arbitrary", "arbitrary")),
        name="peer_experts",
    )(x1, act, lw["peer_vt"], *route, lw["ln2_g"], lw["ln2_b"])


def _ple_kernel(x_ref, pe_ref, wg_ref, wp_ref, lg_ref, lb_ref, o_ref):
    x = x_ref[...]
    ple = jax.nn.sigmoid(_dot(x.astype(BF), wg_ref[...])) * _dot(pe_ref[...].astype(BF), wp_ref[...])
    o_ref[...] = _layer_norm(DN_ALPHA * x + ple, lg_ref[...], lb_ref[...])


def _ple(x2, pe, lw, tm):
    n = x2.shape[0]
    full = lambda a: pl.BlockSpec(a.shape, lambda i: (0, 0))
    ws = (lw["ple_w_g"], lw["ple_w_p"], lw["ln3_g"], lw["ln3_b"])
    return pl.pallas_call(
        _ple_kernel,
        out_shape=jax.ShapeDtypeStruct((n, D_MODEL), F32),
        grid=(n // tm,),
        in_specs=[pl.BlockSpec((tm, D_MODEL), lambda i: (i, 0)), pl.BlockSpec((tm, PLE_DIM), lambda i: (i, 0))]
        + [full(a) for a in ws],
        out_specs=pl.BlockSpec((tm, D_MODEL), lambda i: (i, 0)),
        compiler_params=_params(("arbitrary",)),
        name="ple",
    )(x2, pe, *ws)


def _time_major(u, bsz, t, bp):
    u = jnp.transpose(u.reshape(bsz, t, -1), (1, 0, 2))
    return jnp.pad(u, ((0, 0), (0, bp - bsz), (0, 0))).reshape(t * bp, -1)


def _batch_major(y, bsz, t, bp):
    return jnp.transpose(y.reshape(t, bp, -1)[:, :bsz], (1, 0, 2)).reshape(bsz * t, -1)


def _layer(x, pe, h0_re, h0_im, cache_k, cache_v, s0, pos, lw, bsz, t, prompt, n_rows):
    n = bsz * t
    h = _in_projection(x, lw["w_in"], 512, IN_COLS // 2)

    bp = max(8, bsz)
    u_tb = _time_major(h[:, OFF_U_SSM:OFF_U_SSM + SSM_WIDTH], bsz, t, bp)
    h0 = jnp.concatenate([h0_re.reshape(bsz, SSM_FLAT), h0_im.reshape(bsz, SSM_FLAT)], axis=1).astype(F32)
    h0 = jnp.pad(h0, ((0, bp - bsz), (0, 0)))
    y_tb, h_last = _s5_mixer(u_tb, h0, lw["s5"], t, bp, CHUNK)
    y_ssm = _batch_major(y_tb, bsz, t, bp)
    new_re = h_last[:bsz, :SSM_FLAT].reshape(bsz, SSM_GROUPS, SSM_STATE)
    new_im = h_last[:bsz, SSM_FLAT:].reshape(bsz, SSM_GROUPS, SSM_STATE)

    k_att = h[:, OFF_K_ATT:OFF_K_ATT + ATT_WIDTH].reshape(bsz, t, ATT_HEADS, ATT_HEAD_DIM)
    v_att = h[:, OFF_V_ATT:OFF_V_ATT + ATT_WIDTH].reshape(bsz, t, ATT_HEADS, ATT_HEAD_DIM)
    if prompt:
        y_att = _attn_prompt(h, lw["rel_bias"], bsz, t)
        new_k, new_v = k_att[:, -n_rows:], v_att[:, -n_rows:]
    else:
        r = cache_k.shape[1]
        y_att = _attn_sample(h, cache_k.reshape(bsz, r, ATT_WIDTH), cache_v.reshape(bsz, r, ATT_WIDTH),
                             lw["rel_bias"], bsz)
        new_k, new_v = k_att, v_att

    y_ret, s_pair = _retention(h, _pair_states(s0), _retention_tables(pos), lw["ret_gn_g"], bsz, t,
                               4 if prompt else 1)
    s_fin = _unpair_states(s_pair)

    tok = 512 if n % 512 == 0 else 256
    x1 = _merge(x, h, y_ssm, y_att, y_ret, lw, 256)
    route = _route(x1, lw, 256)
    x2 = _peer(x1, route, lw, tok)
    x3 = _ple(x2, pe, lw, 256)
    return x3, (new_re, new_im, new_k, new_v, s_fin)


def _layer_weights(li, w_in, ssm_lam_re, ssm_lam_im, ssm_b_re, ssm_b_im, ssm_c_re, ssm_c_im, ssm_log_dt, ssm_d,
                   ssm_w_glu, ssm_b_glu, att_rel_bias, ret_gn_g, w_br_ssm, w_br_att, w_br_ret, w_o, ln1_g, ln1_b,
                   peer_w_q, peer_sub_keys, peer_u, peer_v, ln2_g, ln2_b, ple_w_g, ple_w_p, ln3_g, ln3_b):
    vec = lambda a: a[li].astype(F32).reshape(1, -1)
    w = w_in[li]
    wq_hi, wq_lo = _split_bf16(peer_w_q[li].astype(F32))
    keys_hi, keys_lo = _split_bf16(peer_sub_keys[li].astype(F32).reshape(2 * PEER_HEADS, N_KEYS, PEER_HALF))
    return {
        "w_in": jnp.concatenate([w[:, OLD_GATE_START:], w[:, :OLD_GATE_START]], axis=1).astype(BF),
        "s5": _s5_tables(ssm_lam_re[li], ssm_lam_im[li], ssm_b_re[li], ssm_b_im[li], ssm_c_re[li], ssm_c_im[li],
                         ssm_log_dt[li], ssm_d[li], ssm_w_glu[li], ssm_b_glu[li]),
        "rel_bias": _rel_bias_tile(att_rel_bias[li]),
        "ret_gn_g": vec(ret_gn_g),
        "w_br_ssm": w_br_ssm[li].astype(BF), "w_br_att": w_br_att[li].astype(BF), "w_br_ret": w_br_ret[li].astype(BF),
        "w_o": w_o[li].astype(BF), "ln1_g": vec(ln1_g), "ln1_b": vec(ln1_b),
        "wq_hi": wq_hi, "wq_lo": wq_lo, "keys_hi": keys_hi, "keys_lo": keys_lo,
        "peer_u": peer_u[li].astype(BF), "peer_vt": peer_v[li].astype(BF).T,
        "ln2_g": vec(ln2_g), "ln2_b": vec(ln2_b),
        "ple_w_g": ple_w_g[li].astype(BF), "ple_w_p": ple_w_p[li].astype(BF), "ln3_g": vec(ln3_g), "ln3_b": vec(ln3_b),
    }


def kernel(x_prompt, x_sample, p_prompt, p_sample, state_ssm_re, state_ssm_im, cache_attn_k, cache_attn_v, state_ret, w_in, ssm_lam_re, ssm_lam_im, ssm_b_re, ssm_b_im, ssm_c_re, ssm_c_im, ssm_log_dt, ssm_d, ssm_w_glu, ssm_b_glu, att_rel_bias, ret_gn_g, w_br_ssm, w_br_att, w_br_ret, w_o, ln1_g, ln1_b, peer_w_q, peer_sub_keys, peer_u, peer_v, ln2_g, ln2_b, ple_w_g, ple_w_p, ln3_g, ln3_b):
    bp, tp, _ = x_prompt.shape
    bs, ts, _ = x_sample.shape
    n_rows = cache_attn_k.shape[2]
    pos_p = jnp.arange(tp)
    pos_s = PAST_LEN + jnp.arange(ts)
    y_p = x_prompt.reshape(bp * tp, D_MODEL).astype(F32)
    y_s = x_sample.reshape(bs * ts, D_MODEL).astype(F32)
    new_p, new_s = [], []
    for li in range(DEPTH):
        lw = _layer_weights(li, w_in, ssm_lam_re, ssm_lam_im, ssm_b_re, ssm_b_im, ssm_c_re, ssm_c_im, ssm_log_dt,
                            ssm_d, ssm_w_glu, ssm_b_glu, att_rel_bias, ret_gn_g, w_br_ssm, w_br_att, w_br_ret, w_o,
                            ln1_g, ln1_b, peer_w_q, peer_sub_keys, peer_u, peer_v, ln2_g, ln2_b, ple_w_g, ple_w_p,
                            ln3_g, ln3_b)
        zero_ssm = jnp.zeros((bp, SSM_GROUPS, SSM_STATE), F32)
        zero_ret = jnp.zeros((bp, RET_HEADS, RET_DIM, RET_DIM), F32)
        y_p, st = _layer(y_p, p_prompt[li].reshape(bp * tp, PLE_DIM), zero_ssm, zero_ssm, None, None, zero_ret,
                         pos_p, lw, bp, tp, True, n_rows)
        new_p.append(st)
        y_s, st = _layer(y_s, p_sample[li].reshape(bs * ts, PLE_DIM), state_ssm_re[li], state_ssm_im[li],
                         cache_attn_k[li], cache_attn_v[li], state_ret[li], pos_s, lw, bs, ts, False, n_rows)
        new_s.append(st)
    stack = lambda states, k: jnp.stack([s[k] for s in states])
    return (y_p.reshape(bp, tp, D_MODEL), y_s.reshape(bs, ts, D_MODEL),
            stack(new_p, 0), stack(new_p, 1), stack(new_p, 2), stack(new_p, 3), stack(new_p, 4),
            stack(new_s, 0), stack(new_s, 1), stack(new_s, 2), stack(new_s, 3), stack(new_s, 4))
```

```python
import functools

import jax
import jax.numpy as jnp
import numpy as np
from jax import lax
from jax.experimental import pallas as pl
from jax.experimental.pallas import tpu as pltpu

F32 = jnp.float32
BF = jnp.bfloat16

D_MODEL = 1024
DEPTH = 2
PAST_LEN = 2048
CHUNK = 64
PLE_DIM = 256
SSM_WIDTH = 256
SSM_GROUP = 16
SSM_GROUPS = 16
SSM_STATE = 64
SSM_FLAT = SSM_GROUPS * SSM_STATE
ATT_HEADS = 8
ATT_HEAD_DIM = 64
ATT_WIDTH = 512
ATT_PAST = 8 * CHUNK
ATT_BAND = ATT_PAST + CHUNK
REL_CLIP = 256
RET_HEADS = 4
RET_DIM = 64
ROPE_BASE = 10000.0
PEER_HEADS = 8
PEER_HALF = 128
N_KEYS = 128
N_EXPERTS = N_KEYS * N_KEYS
PEER_TOPK = 16
SUBLANES = 8
BF_CHUNK = 256
DN_ALPHA = (2 * DEPTH) ** 0.25
LN_EPS = 1e-5
LANES = 128
NEG_MASK = float(jnp.finfo(jnp.float32).min)

IN_COLS = 5888
OFF_GATES = 0
OFF_U_SSM = 3072
OFF_Q_ATT = 3328
OFF_K_ATT = 3840
OFF_V_ATT = 4352
OFF_Q_RET = 4864
OFF_K_RET = 5120
OFF_V_RET = 5376
OFF_G_RET = 5632
OLD_GATE_START = 2816

VMEM_LIMIT = 48 * 1024 * 1024


def _params(sem):
    return pltpu.CompilerParams(dimension_semantics=sem, vmem_limit_bytes=VMEM_LIMIT)


def _dot(a, b):
    return jnp.dot(a, b, preferred_element_type=F32)


def _dot_nt(a, b):
    return lax.dot_general(a, b, (((1,), (1,)), ((), ())), preferred_element_type=F32)


def _dot_tn(a, b):
    return lax.dot_general(a, b, (((0,), (0,)), ((), ())), preferred_element_type=F32)


def _layer_norm(r, g, b):
    mu = jnp.mean(r, axis=-1, keepdims=True)
    d = r - mu
    var = jnp.mean(d * d, axis=-1, keepdims=True)
    return d * lax.rsqrt(var + LN_EPS) * g + b


def _matmul_kernel(x_ref, w_ref, o_ref):
    o_ref[...] = _dot(x_ref[...].astype(BF), w_ref[...])


def _in_projection(x, w, tm, tn):
    m, k = x.shape
    n = w.shape[1]
    return pl.pallas_call(
        _matmul_kernel,
        out_shape=jax.ShapeDtypeStruct((m, n), F32),
        grid=(n // tn, m // tm),
        in_specs=[pl.BlockSpec((tm, k), lambda j, i: (i, 0)),
                  pl.BlockSpec((k, tn), lambda j, i: (0, j))],
        out_specs=pl.BlockSpec((tm, tn), lambda j, i: (i, j)),
        compiler_params=_params(("arbitrary", "arbitrary")),
        name="in_projection",
    )(x, w)


def _s5_kernel(u_ref, bm_ref, a_ref, cm_ref, d_ref, wg_ref, bg_ref, h0_ref, y_ref, hl_ref, hbuf, hst, *, steps, bp):
    @pl.when(pl.program_id(0) == 0)
    def _():
        hst[...] = h0_ref[...]

    u = u_ref[...]
    hbuf[...] = _dot(u.astype(BF), bm_ref[...])
    a_re = jnp.broadcast_to(a_ref[0:1, :], (bp, SSM_FLAT))
    a_im = jnp.broadcast_to(a_ref[1:2, :], (bp, SSM_FLAT))

    def step(t, carry):
        h_re, h_im = carry
        r0 = pl.multiple_of(t * bp, bp)
        n_re = a_re * h_re - a_im * h_im + hbuf[pl.ds(r0, bp), 0:SSM_FLAT]
        n_im = a_re * h_im + a_im * h_re + hbuf[pl.ds(r0, bp), SSM_FLAT:2 * SSM_FLAT]
        hbuf[pl.ds(r0, bp), 0:SSM_FLAT] = n_re
        hbuf[pl.ds(r0, bp), SSM_FLAT:2 * SSM_FLAT] = n_im
        return n_re, n_im

    h_re, h_im = lax.fori_loop(0, steps, step, (hst[:, 0:SSM_FLAT], hst[:, SSM_FLAT:2 * SSM_FLAT]))
    hst[:, 0:SSM_FLAT] = h_re
    hst[:, SSM_FLAT:2 * SSM_FLAT] = h_im
    hl_ref[...] = hst[...]
    y = _dot(hbuf[...].astype(BF), cm_ref[...]) + d_ref[...] * u
    z = jax.nn.gelu(y)
    y_ref[...] = z * jax.nn.sigmoid(_dot(z.astype(BF), wg_ref[...]) + bg_ref[...])


def _s5_mixer(u_tb, h0, sp, n_steps, bp, steps):
    rows = steps * bp
    kern = functools.partial(_s5_kernel, steps=steps, bp=bp)
    const = lambda c: (0, 0)
    return pl.pallas_call(
        kern,
        out_shape=(jax.ShapeDtypeStruct((n_steps * bp, SSM_WIDTH), F32),
                   jax.ShapeDtypeStruct((bp, 2 * SSM_FLAT), F32)),
        grid=(n_steps // steps,),
        in_specs=[pl.BlockSpec((rows, SSM_WIDTH), lambda c: (c, 0)),
                  pl.BlockSpec((SSM_WIDTH, 2 * SSM_FLAT), const),
                  pl.BlockSpec((2, SSM_FLAT), const),
                  pl.BlockSpec((2 * SSM_FLAT, SSM_WIDTH), const),
                  pl.BlockSpec((1, SSM_WIDTH), const),
                  pl.BlockSpec((SSM_WIDTH, SSM_WIDTH), const),
                  pl.BlockSpec((1, SSM_WIDTH), const),
                  pl.BlockSpec((bp, 2 * SSM_FLAT), const)],
        out_specs=(pl.BlockSpec((rows, SSM_WIDTH), lambda c: (c, 0)),
                   pl.BlockSpec((bp, 2 * SSM_FLAT), const)),
        scratch_shapes=[pltpu.VMEM((rows, 2 * SSM_FLAT), F32), pltpu.VMEM((bp, 2 * SSM_FLAT), F32)],
        compiler_params=_params(("arbitrary",)),
        name="s5_mixer",
    )(u_tb, sp["bmat"], sp["a"], sp["cmat"], sp["d"], sp["w_glu"], sp["b_glu"], h0)


def _s5_tables(lam_re, lam_im, b_re, b_im, c_re, c_im, log_dt, d_skip, w_glu, b_glu):
    lr, li = lam_re.astype(F32), lam_im.astype(F32)
    dt = jnp.exp(log_dt.astype(F32))[:, None]
    mag = jnp.exp(lr * dt)
    a_re, a_im = mag * jnp.cos(li * dt), mag * jnp.sin(li * dt)
    den = lr * lr + li * li
    k_re = ((a_re - 1.0) * lr + a_im * li) / den
    k_im = (a_im * lr - (a_re - 1.0) * li) / den
    br, bi = b_re.astype(F32), b_im.astype(F32)
    bbar_re = k_re[..., None] * br - k_im[..., None] * bi
    bbar_im = k_re[..., None] * bi + k_im[..., None] * br
    eye = jnp.eye(SSM_GROUPS, dtype=F32)

    def in_blockdiag(m):
        m = jnp.transpose(m, (0, 2, 1))
        return (m[:, :, None, :] * eye[:, None, :, None]).reshape(SSM_WIDTH, SSM_FLAT)

    def out_blockdiag(m):
        m = jnp.transpose(m, (0, 2, 1))
        return (m[:, :, None, :] * eye[:, None, :, None]).reshape(SSM_FLAT, SSM_WIDTH)

    bmat = jnp.concatenate([in_blockdiag(bbar_re), in_blockdiag(bbar_im)], axis=1)
    cmat = jnp.concatenate([out_blockdiag(c_re.astype(F32)), -out_blockdiag(c_im.astype(F32))], axis=0)
    a = jnp.stack([a_re.reshape(SSM_FLAT), a_im.reshape(SSM_FLAT)])
    return {"bmat": bmat.astype(BF), "cmat": cmat.astype(BF), "a": a,
            "d": d_skip.astype(F32).reshape(1, SSM_WIDTH), "w_glu": w_glu.astype(BF),
            "b_glu": b_glu.astype(F32).reshape(1, SSM_WIDTH)}


def _attend_chunk(q, kc, vc, bias_ref, valid):
    first = lax.broadcasted_iota(jnp.int32, (1, LANES), 1) < ATT_HEAD_DIM
    outs = []
    for hh in range(2):
        qm = jnp.where(first if hh == 0 else jnp.logical_not(first), q, 0.0).astype(BF)
        s = _dot_nt(qm, kc) + bias_ref[hh]
        if valid is not None:
            s = jnp.where(valid, s, NEG_MASK)
        m = jnp.max(s, axis=-1, keepdims=True)
        p = jnp.exp(s - m)
        l = jnp.sum(p, axis=-1, keepdims=True)
        outs.append(_dot(p.astype(BF), vc) / l)
    return jnp.where(first, outs[0], outs[1])


ATT_QROWS = 4 * CHUNK
ATT_WINDOW = ATT_PAST + ATT_QROWS


def _attn_prompt_kernel(q_ref, k0_ref, k1_ref, k2_ref, v0_ref, v1_ref, v2_ref, bias_ref, o_ref):
    tb = pl.program_id(2)
    kwin = jnp.concatenate([k0_ref[...], k1_ref[...], k2_ref[...]], axis=0).astype(BF)
    vwin = jnp.concatenate([v0_ref[...], v1_ref[...], v2_ref[...]], axis=0).astype(BF)
    first = lax.broadcasted_iota(jnp.int32, (1, LANES), 1) < ATT_HEAD_DIM
    q = q_ref[...] * (ATT_HEAD_DIM ** -0.5)
    q2 = jnp.concatenate([jnp.where(first, q, 0.0), jnp.where(first, 0.0, q)], axis=0).astype(BF)
    s = _dot_nt(q2, kwin) + jnp.concatenate([bias_ref[0], bias_ref[1]], axis=0)
    col = lax.broadcasted_iota(jnp.int32, (1, ATT_WINDOW), 1)
    s = jnp.where(col + (tb * ATT_QROWS - ATT_PAST) >= 0, s, NEG_MASK)
    p = jnp.exp(s - jnp.max(s, axis=-1, keepdims=True))
    o = _dot(p.astype(BF), vwin) / jnp.sum(p, axis=-1, keepdims=True)
    o_ref[...] = jnp.where(first, o[:ATT_QROWS], o[ATT_QROWS:])


def _attn_prompt(h, bias, bsz, t):
    rows = ATT_QROWS
    tbn = t // rows
    qc, kc, vc = OFF_Q_ATT // LANES, OFF_K_ATT // LANES, OFF_V_ATT // LANES
    band = jnp.concatenate(
        [jnp.pad(bias, ((0, 0), (0, 0), (c * CHUNK, ATT_WINDOW - ATT_BAND - c * CHUNK)), constant_values=NEG_MASK)
         for c in range(rows // CHUNK)], axis=1)

    def past(col0, back):
        return pl.BlockSpec((rows, LANES), lambda b, hp, tb: (b * tbn + jnp.maximum(tb - back, 0), col0 + hp))

    return pl.pallas_call(
        _attn_prompt_kernel,
        out_shape=jax.ShapeDtypeStruct((bsz * t, ATT_WIDTH), F32),
        grid=(bsz, ATT_HEADS // 2, tbn),
        in_specs=[past(qc, 0), past(kc, 2), past(kc, 1), past(kc, 0), past(vc, 2), past(vc, 1), past(vc, 0),
                  pl.BlockSpec((2, rows, ATT_WINDOW), lambda b, hp, tb: (hp, 0, 0))],
        out_specs=pl.BlockSpec((rows, LANES), lambda b, hp, tb: (b * tbn + tb, hp)),
        compiler_params=_params(("arbitrary", "arbitrary", "arbitrary")),
        name="attn_prompt",
    )(h, h, h, h, h, h, h, band)


def _attn_sample_kernel(q_ref, k_ref, v_ref, ck_ref, cv_ref, bias_ref, o_ref):
    kwin = jnp.concatenate([ck_ref[0], k_ref[...]], axis=0).astype(BF)
    vwin = jnp.concatenate([cv_ref[0], v_ref[...]], axis=0).astype(BF)
    q = q_ref[...] * (ATT_HEAD_DIM ** -0.5)
    o_ref[...] = _attend_chunk(q, kwin, vwin, bias_ref, None)


def _attn_sample(h, cache_k, cache_v, bias, bsz):
    qc, kc, vc = OFF_Q_ATT // LANES, OFF_K_ATT // LANES, OFF_V_ATT // LANES
    new = lambda col0: pl.BlockSpec((CHUNK, LANES), lambda b, hp: (b, col0 + hp))
    old = pl.BlockSpec((1, ATT_PAST, LANES), lambda b, hp: (b, 0, hp))
    return pl.pallas_call(
        _attn_sample_kernel,
        out_shape=jax.ShapeDtypeStruct((bsz * CHUNK, ATT_WIDTH), F32),
        grid=(bsz, ATT_HEADS // 2),
        in_specs=[new(qc), new(kc), new(vc), old, old,
                  pl.BlockSpec((2, CHUNK, ATT_BAND), lambda b, hp: (hp, 0, 0))],
        out_specs=pl.BlockSpec((CHUNK, LANES), lambda b, hp: (b, hp)),
        compiler_params=_params(("arbitrary", "arbitrary")),
        name="attn_sample",
    )(h, h, h, cache_k, cache_v, bias)


def _rel_bias_tile(table):
    dist = jnp.arange(-(CHUNK - 1), ATT_BAND)
    per_dist = table[jnp.clip(dist, -REL_CLIP, REL_CLIP) + REL_CLIP].astype(F32).T
    rev = per_dist[:, ::-1]
    n = dist.shape[0]
    return jnp.stack([rev[:, CHUNK - 1 - i:n - i] for i in range(CHUNK)], axis=1)


def _ret_kernel(q_ref, k_ref, v_ref, g_ref, cos_ref, sin_ref, intra_ref, qw_ref, kw_ref, dec_ref, bd_ref, gn_ref,
                s0_ref, y_ref, sf_ref, state, *, cpb):
    @pl.when(pl.program_id(1) == 0)
    def _():
        state[...] = s0_ref[0]

    lane = lax.broadcasted_iota(jnp.int32, (1, LANES), 1)
    first = lane < RET_DIM
    low_half = (lane % RET_DIM) < (RET_DIM // 2)

    for c in range(cpb):
        rows = slice(c * CHUNK, (c + 1) * CHUNK)
        cos = cos_ref[rows, :]
        sin = sin_ref[rows, :]

        def rope(x):
            swapped = jnp.where(low_half, pltpu.roll(x, LANES - RET_DIM // 2, 1), pltpu.roll(x, RET_DIM // 2, 1))
            return x * cos + swapped * sin

        for hp in range(RET_HEADS // 2):
            cols = slice(hp * LANES, (hp + 1) * LANES)
            q = rope(q_ref[rows, cols]) * (RET_DIM ** -0.5)
            k = rope(k_ref[rows, cols])
            kb = k.astype(BF)
            vb = v_ref[rows, cols].astype(BF)
            intra = []
            for hh in range(2):
                qm = jnp.where(first if hh == 0 else jnp.logical_not(first), q, 0.0).astype(BF)
                s = _dot_nt(qm, kb) * intra_ref[2 * hp + hh]
                intra.append(_dot(s.astype(BF), vb))
            s_prev = state[hp]
            o = jnp.where(first, intra[0], intra[1]) + _dot((q * qw_ref[hp]).astype(BF), s_prev.astype(BF))
            kv = _dot_tn((k * kw_ref[hp]).astype(BF), vb)
            state[hp] = dec_ref[hp] * s_prev + bd_ref[...] * kv
            sum_a = jnp.sum(jnp.where(first, o, 0.0), axis=-1, keepdims=True)
            sum_b = jnp.sum(o, axis=-1, keepdims=True) - sum_a
            d = o - jnp.where(first, sum_a, sum_b) * (1.0 / RET_DIM)
            sq = d * d
            var_a = jnp.sum(jnp.where(first, sq, 0.0), axis=-1, keepdims=True)
            var_b = jnp.sum(sq, axis=-1, keepdims=True) - var_a
            var = jnp.where(first, var_a, var_b) * (1.0 / RET_DIM)
            g = g_ref[rows, cols]
            y_ref[rows, cols] = d * lax.rsqrt(var + LN_EPS) * gn_ref[:, cols] * (g * jax.nn.sigmoid(g))
    sf_ref[0] = state[...]


def _retention(h, s0, rt, gn, bsz, t, cpb):
    rows = cpb * CHUNK
    tbn = t // rows
    width = RET_HEADS * RET_DIM
    col = lambda off: pl.BlockSpec((rows, width), lambda b, tb: (b * tbn + tb, off // width))
    tab = pl.BlockSpec((rows, LANES), lambda b, tb: (tb, 0))
    const3 = lambda shape: pl.BlockSpec(shape, lambda b, tb: (0, 0, 0))
    st = pl.BlockSpec((1, 2, LANES, LANES), lambda b, tb: (b, 0, 0, 0))
    return pl.pallas_call(
        functools.partial(_ret_kernel, cpb=cpb),
        out_shape=(jax.ShapeDtypeStruct((bsz * t, width), F32),
                   jax.ShapeDtypeStruct((bsz, 2, LANES, LANES), F32)),
        grid=(bsz, tbn),
        in_specs=[col(OFF_Q_RET), col(OFF_K_RET), col(OFF_V_RET), col(OFF_G_RET), tab, tab,
                  const3((RET_HEADS, CHUNK, CHUNK)), const3((2, CHUNK, LANES)), const3((2, CHUNK, LANES)),
                  const3((2, LANES, LANES)), pl.BlockSpec((LANES, LANES), lambda b, tb: (0, 0)),
                  pl.BlockSpec((1, width), lambda b, tb: (0, 0)), st],
        out_specs=(pl.BlockSpec((rows, width), lambda b, tb: (b * tbn + tb, 0)), st),
        scratch_shapes=[pltpu.VMEM((2, LANES, LANES), F32)],
        compiler_params=_params(("arbitrary", "arbitrary")),
        name="retention",
    )(h, h, h, h, rt["cos"], rt["sin"], rt["intra"], rt["qw"], rt["kw"], rt["dec"], rt["bd"], gn, s0)


def _retention_tables(pos):
    lg = jnp.log1p(-(2.0 ** (-5.0 - jnp.arange(RET_HEADS, dtype=F32))))
    i = jnp.arange(CHUNK, dtype=F32)
    intra = jnp.exp(lg[:, None, None] * jnp.abs(i[:, None] - i[None, :]))
    lane_head = jnp.arange(2 * LANES) // RET_DIM
    lg_lane = lg[lane_head]
    qw = jnp.exp(lg_lane[None, :] * (i[:, None] + 1.0))
    kw = jnp.exp(lg_lane[None, :] * (CHUNK - 1.0 - i)[:, None])
    pair = lambda m: jnp.stack([m[:, :LANES], m[:, LANES:]])
    same_head = (jnp.arange(LANES)[:, None] // RET_DIM) == (jnp.arange(LANES)[None, :] // RET_DIM)
    bd = same_head.astype(F32)
    dec_lane = jnp.exp(lg_lane * CHUNK)
    dec = jnp.stack([bd * dec_lane[:LANES][:, None], bd * dec_lane[LANES:][:, None]])
    half = RET_DIM // 2
    freqs = ROPE_BASE ** (-jnp.arange(half, dtype=F32) / half)
    ang = pos.astype(F32)[:, None] * freqs[None]
    cos = jnp.tile(jnp.cos(ang), (1, LANES // half))
    sin = jnp.tile(jnp.concatenate([-jnp.sin(ang), jnp.sin(ang)], axis=1), (1, LANES // RET_DIM))
    return {"intra": intra, "qw": pair(qw), "kw": pair(kw), "dec": dec, "bd": bd, "cos": cos, "sin": sin}


def _pair_states(s):
    b = s.shape[0]
    s = s.astype(F32).reshape(b, 2, 2, RET_DIM, RET_DIM)
    out = jnp.zeros((b, 2, 2, RET_DIM, 2, RET_DIM), F32)
    out = out.at[:, :, 0, :, 0, :].set(s[:, :, 0]).at[:, :, 1, :, 1, :].set(s[:, :, 1])
    return out.reshape(b, 2, LANES, LANES)


def _unpair_states(s):
    b = s.shape[0]
    s = s.reshape(b, 2, 2, RET_DIM, 2, RET_DIM)
    return jnp.stack([s[:, :, 0, :, 0, :], s[:, :, 1, :, 1, :]], axis=2).reshape(b, RET_HEADS, RET_DIM, RET_DIM)


def _merge_kernel(x_ref, g1_ref, g2_ref, g3_ref, ys_ref, ya_ref, yr_ref, ws_ref, wa_ref, wr_ref, wo_ref,
                  lg_ref, lb_ref, o_ref):
    merged = (jax.nn.sigmoid(g1_ref[...]) * _dot(ys_ref[...].astype(BF), ws_ref[...])
              + jax.nn.sigmoid(g2_ref[...]) * _dot(ya_ref[...].astype(BF), wa_ref[...])
              + jax.nn.sigmoid(g3_ref[...]) * _dot(yr_ref[...].astype(BF), wr_ref[...]))
    r = DN_ALPHA * x_ref[...] + _dot(merged.astype(BF), wo_ref[...])
    o_ref[...] = _layer_norm(r, lg_ref[...], lb_ref[...])


def _merge(x, h, y_ssm, y_att, y_ret, lw, tm):
    n = x.shape[0]
    row = lambda w: pl.BlockSpec((tm, w), lambda i: (i, 0))
    gate = lambda j: pl.BlockSpec((tm, D_MODEL), lambda i: (i, j))
    full = lambda a: pl.BlockSpec(a.shape, lambda i: (0, 0))
    ws = (lw["w_br_ssm"], lw["w_br_att"], lw["w_br_ret"], lw["w_o"], lw["ln1_g"], lw["ln1_b"])
    return pl.pallas_call(
        _merge_kernel,
        out_shape=jax.ShapeDtypeStruct((n, D_MODEL), F32),
        grid=(n // tm,),
        in_specs=[row(D_MODEL), gate(0), gate(1), gate(2), row(SSM_WIDTH), row(ATT_WIDTH),
                  row(RET_HEADS * RET_DIM)] + [full(a) for a in ws],
        out_specs=row(D_MODEL),
        compiler_params=_params(("arbitrary",)),
        name="merge",
    )(x, h, h, h, y_ssm, y_att, y_ret, *ws)


CAND_ROWS = 80


def _cand_flat_index():
    f = [b for b in range(16)]
    for a in range(1, 8):
        f += [a * 16 + b for b in range(8)]
    f += [a * 16 for a in range(8, 16)]
    return np.asarray(f, np.float32)


def _split_bf16(x):
    hi = x.astype(BF)
    return hi, (x - hi.astype(F32)).astype(BF)


def _take16(work, order):
    t = work.shape[1]
    row16 = lax.broadcasted_iota(jnp.int32, (PEER_TOPK, t), 0)
    rank = jnp.full(work.shape, float(PEER_TOPK), F32)
    vals = jnp.zeros((PEER_TOPK, t), F32)
    for r in range(PEER_TOPK):
        m = jnp.max(work, axis=0, keepdims=True)
        first = jnp.min(jnp.where(work == m, order, 1e9), axis=0, keepdims=True)
        sel = order == first
        rank = jnp.where(sel, float(r), rank)
        work = jnp.where(sel, -jnp.inf, work)
        vals = jnp.where(row16 == r, m, vals)
    return rank, vals


def _route_head_exact(s1, s2, fidx, outs, h):
    r2_ref, e2_ref, c1_ref, e1_ref = outs
    key_idx = lax.broadcasted_iota(jnp.int32, s1.shape, 0).astype(F32)
    rank1, v1 = _take16(s1, key_idx)
    rank2, v2 = _take16(s2, key_idx)
    blocks = [v1[0:1] + v2]
    for a in range(1, 8):
        blocks.append(v1[a:a + 1] + v2[0:8])
    blocks.append(v1[8:16] + v2[0:1])
    cand = jnp.concatenate(blocks, axis=0)
    rank_c, _ = _take16(cand, fidx)
    chosen = jnp.where(rank_c < float(PEER_TOPK), 1.0, 0.0)
    counts = [jnp.sum(chosen[0:16], axis=0, keepdims=True)]
    for a in range(1, 8):
        counts.append(jnp.sum(chosen[8 + 8 * a:16 + 8 * a], axis=0, keepdims=True))
    for a in range(8, 16):
        counts.append(chosen[64 + a:65 + a])
    c1 = jnp.zeros(s1.shape, F32)
    for a in range(PEER_TOPK):
        c1 = jnp.where(rank1 == float(a), counts[a], c1)
    z = jnp.sum(chosen * jnp.exp(cand - cand[0:1]), axis=0, keepdims=True)
    r2_ref[h] = rank2.astype(BF)
    e2_ref[h] = jnp.exp(s2 - v2[0:1]).astype(BF)
    c1_ref[h] = c1
    e1_ref[h] = jnp.exp(s1 - v1[0:1]) / z


def _batcher_pairs(n):
    pairs = []

    def merge(lo, m, r):
        step = r * 2
        if step < m:
            merge(lo, m, step)
            merge(lo + r, m, step)
            pairs.extend((i, i + r) for i in range(lo + r, lo + m - r, step))
        else:
            pairs.append((lo, lo + r))

    def sort(lo, m):
        if m > 1:
            sort(lo, m // 2)
            sort(lo + m // 2, m // 2)
            merge(lo, m, 1)

    sort(0, n)
    return pairs


SORT16_PAIRS = _batcher_pairs(PEER_TOPK)


def _exchange(x, i, j):
    a, b = x[i], x[j]
    if b is None:
        return
    if a is None:
        x[i], x[j] = b, None
    else:
        x[i], x[j] = jnp.maximum(a, b), jnp.minimum(a, b)


def _top16_values(tiles):
    x = list(tiles) + [None] * (PEER_TOPK - len(tiles))
    for i, j in SORT16_PAIRS:
        _exchange(x, i, j)
    for shift in (4, 2, 1):
        y = [None if v is None else pltpu.roll(v, shift, 0) for v in x]
        z = []
        for i in range(PEER_TOPK):
            a, b = x[i], y[PEER_TOPK - 1 - i]
            z.append(b if a is None else a if b is None else jnp.maximum(a, b))
        for stride in (8, 4, 2, 1):
            for i in range(PEER_TOPK):
                if i & stride == 0:
                    _exchange(z, i, i + stride)
        x = z
    return x


def _count_greater(v, x):
    b3 = v[7] > x
    b2 = jnp.where(b3, v[11], v[3]) > x
    b1 = jnp.where(b3, jnp.where(b2, v[13], v[9]), jnp.where(b2, v[5], v[1])) > x
    b0 = jnp.where(b3, jnp.where(b2, jnp.where(b1, v[14], v[12]), jnp.where(b1, v[10], v[8])),
                   jnp.where(b2, jnp.where(b1, v[6], v[4]), jnp.where(b1, v[2], v[0]))) > x
    lo = (jnp.where(b3, 8.0, 0.0) + jnp.where(b2, 4.0, 0.0)) + (jnp.where(b1, 2.0, 0.0) + jnp.where(b0, 1.0, 0.0))
    return jnp.where(v[15] > x, float(PEER_TOPK), lo)


def _column_sum(tiles):
    parts = list(tiles)
    while len(parts) > 1:
        parts = [parts[i] + parts[i + 1] for i in range(0, len(parts) - 1, 2)] + parts[len(parts) & ~1:]
    total = parts[0]
    for shift in (4, 2, 1):
        total = total + pltpu.roll(total, shift, 0)
    return total


def _route_head_fast(s1, s2, outs, h, cols):
    r2_ref, e2_ref, c1_ref, e1_ref = outs
    n_tiles = N_KEYS // SUBLANES
    t1 = [s1[i * SUBLANES:(i + 1) * SUBLANES] for i in range(n_tiles)]
    t2 = [s2[i * SUBLANES:(i + 1) * SUBLANES] for i in range(n_tiles)]
    v1 = _top16_values(t1)
    v2 = _top16_values(t2)
    sub = lax.broadcasted_iota(jnp.int32, t1[0].shape, 0)

    def by_sublane(vals):
        out = vals[0]
        for b in range(1, SUBLANES):
            out = jnp.where(sub == b, vals[b], out)
        return out

    v2_lo, v2_hi, v1_hi = by_sublane(v2[0:8]), by_sublane(v2[8:16]), by_sublane(v1[8:16])
    cand = [v1[0] + v2_lo, v1[0] + v2_hi] + [v1[a] + v2_lo for a in range(1, 8)] + [v1_hi + v2[0]]
    tau = _top16_values(cand)[PEER_TOPK - 1]
    chosen = [jnp.where(c >= tau, 1.0, 0.0) for c in cand]
    counts = [_column_sum(chosen[0:2])] + [_column_sum([chosen[1 + a]]) for a in range(1, 8)]
    counts += [jnp.broadcast_to(chosen[9][a:a + 1], sub.shape) for a in range(SUBLANES)]
    z = _column_sum([ch * jnp.exp(c - cand[0][0:1]) for ch, c in zip(chosen, cand)])
    inv_z = 1.0 / z

    rank2 = [_count_greater(v2, x) for x in t2]
    c1 = []
    for x in t1:
        c = jnp.zeros_like(x)
        for a in range(PEER_TOPK):
            c = jnp.where(x == v1[a], counts[a], c)
        c1.append(c)
    r2_ref[h, :, cols] = jnp.concatenate(rank2, axis=0).astype(BF)
    e2_ref[h, :, cols] = jnp.concatenate([jnp.exp(x - v2[0]) for x in t2], axis=0).astype(BF)
    c1_ref[h, :, cols] = jnp.concatenate(c1, axis=0)
    e1_ref[h, :, cols] = jnp.concatenate([jnp.exp(x - v1[0]) * inv_z for x in t1], axis=0)

    no_tie_rank_sum = float(sum(range(PEER_TOPK)) + (N_KEYS - PEER_TOPK) * PEER_TOPK)
    bad = jnp.where(_column_sum(rank2) != no_tie_rank_sum, 1.0, 0.0)
    bad = bad + jnp.where(_column_sum([jnp.where(x >= v1[PEER_TOPK - 1], 1.0, 0.0) for x in t1]) != PEER_TOPK, 1.0, 0.0)
    for a in range(PEER_TOPK - 1):
        bad = bad + jnp.where(v1[a] == v1[a + 1], 1.0, 0.0)
    return bad + jnp.where(_column_sum(chosen) != PEER_TOPK, 1.0, 0.0)


def _route_kernel(x_ref, wh_ref, wl_ref, kh_ref, kl_ref, fidx_ref, r2_ref, e2_ref, c1_ref, e1_ref):
    x = x_ref[...]
    t = x.shape[0]
    xh, xl = _split_bf16(x)
    q = _dot(xh, wh_ref[...]) + (_dot(xl, wh_ref[...]) + _dot(xh, wl_ref[...]))
    outs = (r2_ref, e2_ref, c1_ref, e1_ref)

    def scores(hs):
        qh, ql = _split_bf16(q[:, hs * PEER_HALF:(hs + 1) * PEER_HALF])
        return _dot_nt(kh_ref[hs], qh) + (_dot_nt(kl_ref[hs], qh) + _dot_nt(kh_ref[hs], ql))

    all_scores = [scores(hs) for hs in range(2 * PEER_HEADS)]
    bad = []
    for h in range(PEER_HEADS):
        flags = [_route_head_fast(all_scores[2 * h][:, c * LANES:(c + 1) * LANES],
                                  all_scores[2 * h + 1][:, c * LANES:(c + 1) * LANES], outs, h,
                                  slice(c * LANES, (c + 1) * LANES)) for c in range(t // LANES)]
        bad.append(sum(flags[1:], flags[0]))

    for h in range(PEER_HEADS):
        @pl.when(jnp.max(bad[h]) > 0.0)
        def _(h=h):
            _route_head_exact(all_scores[2 * h], all_scores[2 * h + 1], fidx_ref[...], outs, h)


def _route(x1, lw, tb):
    n = x1.shape[0]
    fidx = jnp.broadcast_to(jnp.asarray(_cand_flat_index())[:, None], (CAND_ROWS, tb))
    out = jax.ShapeDtypeStruct((PEER_HEADS, N_KEYS, n), F32)
    ospec = pl.BlockSpec((PEER_HEADS, N_KEYS, tb), lambda i: (0, 0, i))
    out_bf = jax.ShapeDtypeStruct((PEER_HEADS, N_KEYS, n), BF)
    ospec_bf = ospec
    full2 = lambda a: pl.BlockSpec(a.shape, lambda i: (0, 0))
    full3 = lambda a: pl.BlockSpec(a.shape, lambda i: (0, 0, 0))
    return pl.pallas_call(
        _route_kernel,
        out_shape=(out_bf, out_bf, out, out),
        grid=(n // tb,),
        in_specs=[pl.BlockSpec((tb, D_MODEL), lambda i: (i, 0)), full2(lw["wq_hi"]), full2(lw["wq_lo"]),
                  full3(lw["keys_hi"]), full3(lw["keys_lo"]), full2(fidx)],
        out_specs=(ospec_bf, ospec_bf, ospec, ospec),
        compiler_params=_params(("arbitrary",)),
        name="peer_route",
    )(x1, lw["wq_hi"], lw["wq_lo"], lw["keys_hi"], lw["keys_lo"], fidx)


PEER_EXPERT_BLOCK = 1024


def _peer_kernel(x_ref, u_ref, vt_ref, r2_ref, e2_ref, c1_ref, e1_ref, lg_ref, lb_ref, o_ref, xb, acc, wt, rowbuf, act):
    s = pl.program_id(1)
    last = pl.num_programs(1) - 1
    tb = x_ref.shape[0]

    @pl.when(s == 0)
    def _():
        xb[...] = x_ref[...].T.astype(BF)
        acc[...] = jnp.zeros_like(acc)
        act[1] = jnp.zeros(act.shape[1:], BF)

    keys_per_step = PEER_EXPERT_BLOCK // N_KEYS
    first_key = pl.multiple_of(jnp.maximum(s - 1, 0) * keys_per_step, keys_per_step)

    for h in range(PEER_HEADS):
        for src, ref in enumerate((c1_ref, e1_ref)):
            blk = ref[h, pl.ds(first_key, keys_per_step), :]
            for j in range(keys_per_step):
                rowbuf[src, h, j] = jnp.broadcast_to(blk[j:j + 1], (SUBLANES, tb)).astype(BF)

    half = N_KEYS // 2
    group = keys_per_step // 2
    tiles = half // SUBLANES

    def spread(row):
        return jnp.broadcast_to(row[None], (tiles, SUBLANES, BF_CHUNK)).reshape(half, BF_CHUNK)

    for tc in range(tb // BF_CHUNK):
        cols = slice(tc * BF_CHUNK, (tc + 1) * BF_CHUNK)
        for jg in range(keys_per_step // group):
            for kh in range(2):
                keys = slice(kh * half, (kh + 1) * half)
                w = [jnp.zeros((half, BF_CHUNK), BF) for _ in range(group)]
                for h in range(PEER_HEADS):
                    r2 = r2_ref[h, keys, cols]
                    e2 = e2_ref[h, keys, cols]
                    for jj in range(group):
                        j = jg * group + jj
                        c1 = spread(rowbuf[0, h, j, :, cols])
                        e1 = spread(rowbuf[1, h, j, :, cols])
                        w[jj] = w[jj] + jnp.where(r2 < c1, e2 * e1, 0)
                for jj in range(group):
                    rows = slice((jg * group + jj) * N_KEYS + kh * half, (jg * group + jj) * N_KEYS + (kh + 1) * half)
                    wt[rows, cols] = w[jj]
    wt[...] = wt[...] * act[(s + 1) % 2]
    acc[...] += _dot(vt_ref[...], wt[...])
    act[s % 2] = jax.nn.gelu(_dot(u_ref[...], xb[...]).astype(BF))

    @pl.when(s == last)
    def _():
        r = DN_ALPHA * x_ref[...] + acc[...].T
        o_ref[...] = _layer_norm(r, lg_ref[...], lb_ref[...])


def _peer(x1, route, lw, tb):
    n = x1.shape[0]
    eb = PEER_EXPERT_BLOCK
    nblk = N_EXPERTS // eb
    rspec = pl.BlockSpec((PEER_HEADS, N_KEYS, tb), lambda i, e: (0, 0, i))
    vec = pl.BlockSpec((1, D_MODEL), lambda i, e: (0, 0))
    return pl.pallas_call(
        _peer_kernel,
        out_shape=jax.ShapeDtypeStruct((n, D_MODEL), F32),
        grid=(n // tb, nblk + 1),
        in_specs=[pl.BlockSpec((tb, D_MODEL), lambda i, s: (i, 0)),
                  pl.BlockSpec((eb, D_MODEL), lambda i, s: (jnp.minimum(s, nblk - 1), 0)),
                  pl.BlockSpec((D_MODEL, eb), lambda i, s: (0, jnp.maximum(s - 1, 0))),
                  rspec, rspec, rspec, rspec, vec, vec],
        out_specs=pl.BlockSpec((tb, D_MODEL), lambda i, s: (i, 0)),
        scratch_shapes=[pltpu.VMEM((D_MODEL, tb), BF), pltpu.VMEM((D_MODEL, tb), F32), pltpu.VMEM((eb, tb), BF),
                        pltpu.VMEM((2, PEER_HEADS, eb // N_KEYS, SUBLANES, tb), BF),
                        pltpu.VMEM((2, eb, tb), BF)],
        compiler_params=_params(("arbitrary", "arbitrary")),
        name="peer_experts",
    )(x1, lw["peer_u"], lw["peer_vt"], *route, lw["ln2_g"], lw["ln2_b"])


def _ple_kernel(x_ref, pe_ref, wg_ref, wp_ref, lg_ref, lb_ref, o_ref):
    x = x_ref[...]
    ple = jax.nn.sigmoid(_dot(x.astype(BF), wg_ref[...])) * _dot(pe_ref[...].astype(BF), wp_ref[...])
    o_ref[...] = _layer_norm(DN_ALPHA * x + ple, lg_ref[...], lb_ref[...])


def _ple(x2, pe, lw, tm):
    n = x2.shape[0]
    full = lambda a: pl.BlockSpec(a.shape, lambda i: (0, 0))
    ws = (lw["ple_w_g"], lw["ple_w_p"], lw["ln3_g"], lw["ln3_b"])
    return pl.pallas_call(
        _ple_kernel,
        out_shape=jax.ShapeDtypeStruct((n, D_MODEL), F32),
        grid=(n // tm,),
        in_specs=[pl.BlockSpec((tm, D_MODEL), lambda i: (i, 0)), pl.BlockSpec((tm, PLE_DIM), lambda i: (i, 0))]
        + [full(a) for a in ws],
        out_specs=pl.BlockSpec((tm, D_MODEL), lambda i: (i, 0)),
        compiler_params=_params(("arbitrary",)),
        name="ple",
    )(x2, pe, *ws)


def _time_major(u, bsz, t, bp):
    u = jnp.transpose(u.reshape(bsz, t, -1), (1, 0, 2))
    return jnp.pad(u, ((0, 0), (0, bp - bsz), (0, 0))).reshape(t * bp, -1)


def _batch_major(y, bsz, t, bp):
    return jnp.transpose(y.reshape(t, bp, -1)[:, :bsz], (1, 0, 2)).reshape(bsz * t, -1)


def _layer(x, pe, h0_re, h0_im, cache_k, cache_v, s0, pos, lw, bsz, t, prompt, n_rows):
    n = bsz * t
    h = _in_projection(x, lw["w_in"], 512, IN_COLS // 2)

    bp = max(8, bsz)
    u_tb = _time_major(h[:, OFF_U_SSM:OFF_U_SSM + SSM_WIDTH], bsz, t, bp)
    h0 = jnp.concatenate([h0_re.reshape(bsz, SSM_FLAT), h0_im.reshape(bsz, SSM_FLAT)], axis=1).astype(F32)
    h0 = jnp.pad(h0, ((0, bp - bsz), (0, 0)))
    y_tb, h_last = _s5_mixer(u_tb, h0, lw["s5"], t, bp, CHUNK)
    y_ssm = _batch_major(y_tb, bsz, t, bp)
    new_re = h_last[:bsz, :SSM_FLAT].reshape(bsz, SSM_GROUPS, SSM_STATE)
    new_im = h_last[:bsz, SSM_FLAT:].reshape(bsz, SSM_GROUPS, SSM_STATE)

    k_att = h[:, OFF_K_ATT:OFF_K_ATT + ATT_WIDTH].reshape(bsz, t, ATT_HEADS, ATT_HEAD_DIM)
    v_att = h[:, OFF_V_ATT:OFF_V_ATT + ATT_WIDTH].reshape(bsz, t, ATT_HEADS, ATT_HEAD_DIM)
    if prompt:
        y_att = _attn_prompt(h, lw["rel_bias"], bsz, t)
        new_k, new_v = k_att[:, -n_rows:], v_att[:, -n_rows:]
    else:
        r = cache_k.shape[1]
        y_att = _attn_sample(h, cache_k.reshape(bsz, r, ATT_WIDTH), cache_v.reshape(bsz, r, ATT_WIDTH),
                             lw["rel_bias"], bsz)
        new_k, new_v = k_att, v_att

    y_ret, s_pair = _retention(h, _pair_states(s0), _retention_tables(pos), lw["ret_gn_g"], bsz, t,
                               4 if prompt else 1)
    s_fin = _unpair_states(s_pair)

    tok = 512 if n % 512 == 0 else 256
    x1 = _merge(x, h, y_ssm, y_att, y_ret, lw, 256)
    route = _route(x1, lw, 256)
    x2 = _peer(x1, route, lw, tok)
    x3 = _ple(x2, pe, lw, 256)
    return x3, (new_re, new_im, new_k, new_v, s_fin)


def _layer_weights(li, w_in, ssm_lam_re, ssm_lam_im, ssm_b_re, ssm_b_im, ssm_c_re, ssm_c_im, ssm_log_dt, ssm_d,
                   ssm_w_glu, ssm_b_glu, att_rel_bias, ret_gn_g, w_br_ssm, w_br_att, w_br_ret, w_o, ln1_g, ln1_b,
                   peer_w_q, peer_sub_keys, peer_u, peer_v, ln2_g, ln2_b, ple_w_g, ple_w_p, ln3_g, ln3_b):
    vec = lambda a: a[li].astype(F32).reshape(1, -1)
    w = w_in[li]
    wq_hi, wq_lo = _split_bf16(peer_w_q[li].astype(F32))
    keys_hi, keys_lo = _split_bf16(peer_sub_keys[li].astype(F32).reshape(2 * PEER_HEADS, N_KEYS, PEER_HALF))
    return {
        "w_in": jnp.concatenate([w[:, OLD_GATE_START:], w[:, :OLD_GATE_START]], axis=1).astype(BF),
        "s5": _s5_tables(ssm_lam_re[li], ssm_lam_im[li], ssm_b_re[li], ssm_b_im[li], ssm_c_re[li], ssm_c_im[li],
                         ssm_log_dt[li], ssm_d[li], ssm_w_glu[li], ssm_b_glu[li]),
        "rel_bias": _rel_bias_tile(att_rel_bias[li]),
        "ret_gn_g": vec(ret_gn_g),
        "w_br_ssm": w_br_ssm[li].astype(BF), "w_br_att": w_br_att[li].astype(BF), "w_br_ret": w_br_ret[li].astype(BF),
        "w_o": w_o[li].astype(BF), "ln1_g": vec(ln1_g), "ln1_b": vec(ln1_b),
        "wq_hi": wq_hi, "wq_lo": wq_lo, "keys_hi": keys_hi, "keys_lo": keys_lo,
        "peer_u": peer_u[li].astype(BF), "peer_vt": peer_v[li].astype(BF).T,
        "ln2_g": vec(ln2_g), "ln2_b": vec(ln2_b),
        "ple_w_g": ple_w_g[li].astype(BF), "ple_w_p": ple_w_p[li].astype(BF), "ln3_g": vec(ln3_g), "ln3_b": vec(ln3_b),
    }


def kernel(x_prompt, x_sample, p_prompt, p_sample, state_ssm_re, state_ssm_im, cache_attn_k, cache_attn_v, state_ret, w_in, ssm_lam_re, ssm_lam_im, ssm_b_re, ssm_b_im, ssm_c_re, ssm_c_im, ssm_log_dt, ssm_d, ssm_w_glu, ssm_b_glu, att_rel_bias, ret_gn_g, w_br_ssm, w_br_att, w_br_ret, w_o, ln1_g, ln1_b, peer_w_q, peer_sub_keys, peer_u, peer_v, ln2_g, ln2_b, ple_w_g, ple_w_p, ln3_g, ln3_b):
    bp, tp, _ = x_prompt.shape
    bs, ts, _ = x_sample.shape
    n_rows = cache_attn_k.shape[2]
    pos_p = jnp.arange(tp)
    pos_s = PAST_LEN + jnp.arange(ts)
    y_p = x_prompt.reshape(bp * tp, D_MODEL).astype(F32)
    y_s = x_sample.reshape(bs * ts, D_MODEL).astype(F32)
    new_p, new_s = [], []
    for li in range(DEPTH):
        lw = _layer_weights(li, w_in, ssm_lam_re, ssm_lam_im, ssm_b_re, ssm_b_im, ssm_c_re, ssm_c_im, ssm_log_dt,
                            ssm_d, ssm_w_glu, ssm_b_glu, att_rel_bias, ret_gn_g, w_br_ssm, w_br_att, w_br_ret, w_o,
                            ln1_g, ln1_b, peer_w_q, peer_sub_keys, peer_u, peer_v, ln2_g, ln2_b, ple_w_g, ple_w_p,
                            ln3_g, ln3_b)
        zero_ssm = jnp.zeros((bp, SSM_GROUPS, SSM_STATE), F32)
        zero_ret = jnp.zeros((bp, RET_HEADS, RET_DIM, RET_DIM), F32)
        y_p, st = _layer(y_p, p_prompt[li].reshape(bp * tp, PLE_DIM), zero_ssm, zero_ssm, None, None, zero_ret,
                         pos_p, lw, bp, tp, True, n_rows)
        new_p.append(st)
        y_s, st = _layer(y_s, p_sample[li].reshape(bs * ts, PLE_DIM), state_ssm_re[li], state_ssm_im[li],
                         cache_attn_k[li], cache_attn_v[li], state_ret[li], pos_s, lw, bs, ts, False, n_rows)
        new_s.append(st)
    stack = lambda states, k: jnp.stack([s[k] for s in states])
    return (y_p.reshape(bp, tp, D_MODEL), y_s.reshape(bs, ts, D_MODEL),
            stack(new_p, 0), stack(new_p, 1), stack(new_p, 2), stack(new_p, 3), stack(new_p, 4),
            stack(new_s, 0), stack(new_s, 1), stack(new_s, 2), stack(new_s, 3), stack(new_s, 4))
```

```python
import functools

import jax
import jax.numpy as jnp
import numpy as np
from jax import lax
from jax.experimental import pallas as pl
from jax.experimental.pallas import tpu as pltpu

F32 = jnp.float32
BF = jnp.bfloat16

D_MODEL = 1024
DEPTH = 2
PAST_LEN = 2048
CHUNK = 64
PLE_DIM = 256
SSM_WIDTH = 256
SSM_GROUP = 16
SSM_GROUPS = 16
SSM_STATE = 64
SSM_FLAT = SSM_GROUPS * SSM_STATE
ATT_HEADS = 8
ATT_HEAD_DIM = 64
ATT_WIDTH = 512
ATT_PAST = 8 * CHUNK
ATT_BAND = ATT_PAST + CHUNK
REL_CLIP = 256
RET_HEADS = 4
RET_DIM = 64
ROPE_BASE = 10000.0
PEER_HEADS = 8
PEER_HALF = 128
N_KEYS = 128
N_EXPERTS = N_KEYS * N_KEYS
PEER_TOPK = 16
SUBLANES = 8
BF_CHUNK = 256
DN_ALPHA = (2 * DEPTH) ** 0.25
LN_EPS = 1e-5
LANES = 128
NEG_MASK = float(jnp.finfo(jnp.float32).min)

IN_COLS = 5888
OFF_GATES = 0
OFF_U_SSM = 3072
OFF_Q_ATT = 3328
OFF_K_ATT = 3840
OFF_V_ATT = 4352
OFF_Q_RET = 4864
OFF_K_RET = 5120
OFF_V_RET = 5376
OFF_G_RET = 5632
OLD_GATE_START = 2816

VMEM_LIMIT = 48 * 1024 * 1024


def _params(sem):
    return pltpu.CompilerParams(dimension_semantics=sem, vmem_limit_bytes=VMEM_LIMIT)


def _dot(a, b):
    return jnp.dot(a, b, preferred_element_type=F32)


def _dot_nt(a, b):
    return lax.dot_general(a, b, (((1,), (1,)), ((), ())), preferred_element_type=F32)


def _dot_tn(a, b):
    return lax.dot_general(a, b, (((0,), (0,)), ((), ())), preferred_element_type=F32)


def _layer_norm(r, g, b):
    mu = jnp.mean(r, axis=-1, keepdims=True)
    d = r - mu
    var = jnp.mean(d * d, axis=-1, keepdims=True)
    return d * lax.rsqrt(var + LN_EPS) * g + b


def _matmul_kernel(x_ref, w_ref, o_ref):
    o_ref[...] = _dot(x_ref[...].astype(BF), w_ref[...])


def _in_projection(x, w, tm, tn):
    m, k = x.shape
    n = w.shape[1]
    return pl.pallas_call(
        _matmul_kernel,
        out_shape=jax.ShapeDtypeStruct((m, n), F32),
        grid=(n // tn, m // tm),
        in_specs=[pl.BlockSpec((tm, k), lambda j, i: (i, 0)),
                  pl.BlockSpec((k, tn), lambda j, i: (0, j))],
        out_specs=pl.BlockSpec((tm, tn), lambda j, i: (i, j)),
        compiler_params=_params(("arbitrary", "arbitrary")),
        name="in_projection",
    )(x, w)


def _s5_kernel(u_ref, bm_ref, a_ref, cm_ref, d_ref, wg_ref, bg_ref, h0_ref, y_ref, hl_ref, hbuf, hst, *, steps, bp):
    @pl.when(pl.program_id(0) == 0)
    def _():
        hst[...] = h0_ref[...]

    u = u_ref[...]
    hbuf[...] = _dot(u.astype(BF), bm_ref[...])
    a_re = jnp.broadcast_to(a_ref[0:1, :], (bp, SSM_FLAT))
    a_im = jnp.broadcast_to(a_ref[1:2, :], (bp, SSM_FLAT))

    def step(t, carry):
        h_re, h_im = carry
        r0 = pl.multiple_of(t * bp, bp)
        n_re = a_re * h_re - a_im * h_im + hbuf[pl.ds(r0, bp), 0:SSM_FLAT]
        n_im = a_re * h_im + a_im * h_re + hbuf[pl.ds(r0, bp), SSM_FLAT:2 * SSM_FLAT]
        hbuf[pl.ds(r0, bp), 0:SSM_FLAT] = n_re
        hbuf[pl.ds(r0, bp), SSM_FLAT:2 * SSM_FLAT] = n_im
        return n_re, n_im

    h_re, h_im = lax.fori_loop(0, steps, step, (hst[:, 0:SSM_FLAT], hst[:, SSM_FLAT:2 * SSM_FLAT]))
    hst[:, 0:SSM_FLAT] = h_re
    hst[:, SSM_FLAT:2 * SSM_FLAT] = h_im
    hl_ref[...] = hst[...]
    y = _dot(hbuf[...].astype(BF), cm_ref[...]) + d_ref[...] * u
    z = jax.nn.gelu(y)
    y_ref[...] = z * jax.nn.sigmoid(_dot(z.astype(BF), wg_ref[...]) + bg_ref[...])


def _s5_mixer(u_tb, h0, sp, n_steps, bp, steps):
    rows = steps * bp
    kern = functools.partial(_s5_kernel, steps=steps, bp=bp)
    const = lambda c: (0, 0)
    return pl.pallas_call(
        kern,
        out_shape=(jax.ShapeDtypeStruct((n_steps * bp, SSM_WIDTH), F32),
                   jax.ShapeDtypeStruct((bp, 2 * SSM_FLAT), F32)),
        grid=(n_steps // steps,),
        in_specs=[pl.BlockSpec((rows, SSM_WIDTH), lambda c: (c, 0)),
                  pl.BlockSpec((SSM_WIDTH, 2 * SSM_FLAT), const),
                  pl.BlockSpec((2, SSM_FLAT), const),
                  pl.BlockSpec((2 * SSM_FLAT, SSM_WIDTH), const),
                  pl.BlockSpec((1, SSM_WIDTH), const),
                  pl.BlockSpec((SSM_WIDTH, SSM_WIDTH), const),
                  pl.BlockSpec((1, SSM_WIDTH), const),
                  pl.BlockSpec((bp, 2 * SSM_FLAT), const)],
        out_specs=(pl.BlockSpec((rows, SSM_WIDTH), lambda c: (c, 0)),
                   pl.BlockSpec((bp, 2 * SSM_FLAT), const)),
        scratch_shapes=[pltpu.VMEM((rows, 2 * SSM_FLAT), F32), pltpu.VMEM((bp, 2 * SSM_FLAT), F32)],
        compiler_params=_params(("arbitrary",)),
        name="s5_mixer",
    )(u_tb, sp["bmat"], sp["a"], sp["cmat"], sp["d"], sp["w_glu"], sp["b_glu"], h0)


def _s5_tables(lam_re, lam_im, b_re, b_im, c_re, c_im, log_dt, d_skip, w_glu, b_glu):
    lr, li = lam_re.astype(F32), lam_im.astype(F32)
    dt = jnp.exp(log_dt.astype(F32))[:, None]
    mag = jnp.exp(lr * dt)
    a_re, a_im = mag * jnp.cos(li * dt), mag * jnp.sin(li * dt)
    den = lr * lr + li * li
    k_re = ((a_re - 1.0) * lr + a_im * li) / den
    k_im = (a_im * lr - (a_re - 1.0) * li) / den
    br, bi = b_re.astype(F32), b_im.astype(F32)
    bbar_re = k_re[..., None] * br - k_im[..., None] * bi
    bbar_im = k_re[..., None] * bi + k_im[..., None] * br
    eye = jnp.eye(SSM_GROUPS, dtype=F32)

    def in_blockdiag(m):
        m = jnp.transpose(m, (0, 2, 1))
        return (m[:, :, None, :] * eye[:, None, :, None]).reshape(SSM_WIDTH, SSM_FLAT)

    def out_blockdiag(m):
        m = jnp.transpose(m, (0, 2, 1))
        return (m[:, :, None, :] * eye[:, None, :, None]).reshape(SSM_FLAT, SSM_WIDTH)

    bmat = jnp.concatenate([in_blockdiag(bbar_re), in_blockdiag(bbar_im)], axis=1)
    cmat = jnp.concatenate([out_blockdiag(c_re.astype(F32)), -out_blockdiag(c_im.astype(F32))], axis=0)
    a = jnp.stack([a_re.reshape(SSM_FLAT), a_im.reshape(SSM_FLAT)])
    return {"bmat": bmat.astype(BF), "cmat": cmat.astype(BF), "a": a,
            "d": d_skip.astype(F32).reshape(1, SSM_WIDTH), "w_glu": w_glu.astype(BF),
            "b_glu": b_glu.astype(F32).reshape(1, SSM_WIDTH)}


def _attend_chunk(q, kc, vc, bias_ref, valid):
    first = lax.broadcasted_iota(jnp.int32, (1, LANES), 1) < ATT_HEAD_DIM
    outs = []
    for hh in range(2):
        qm = jnp.where(first if hh == 0 else jnp.logical_not(first), q, 0.0).astype(BF)
        s = _dot_nt(qm, kc) + bias_ref[hh]
        if valid is not None:
            s = jnp.where(valid, s, NEG_MASK)
        m = jnp.max(s, axis=-1, keepdims=True)
        p = jnp.exp(s - m)
        l = jnp.sum(p, axis=-1, keepdims=True)
        outs.append(_dot(p.astype(BF), vc) / l)
    return jnp.where(first, outs[0], outs[1])


ATT_QROWS = 4 * CHUNK
ATT_WINDOW = ATT_PAST + ATT_QROWS


def _attn_prompt_kernel(q_ref, k0_ref, k1_ref, k2_ref, v0_ref, v1_ref, v2_ref, bias_ref, o_ref):
    tb = pl.program_id(2)
    kwin = jnp.concatenate([k0_ref[...], k1_ref[...], k2_ref[...]], axis=0).astype(BF)
    vwin = jnp.concatenate([v0_ref[...], v1_ref[...], v2_ref[...]], axis=0).astype(BF)
    first = lax.broadcasted_iota(jnp.int32, (1, LANES), 1) < ATT_HEAD_DIM
    q = q_ref[...] * (ATT_HEAD_DIM ** -0.5)
    q2 = jnp.concatenate([jnp.where(first, q, 0.0), jnp.where(first, 0.0, q)], axis=0).astype(BF)
    s = _dot_nt(q2, kwin) + jnp.concatenate([bias_ref[0], bias_ref[1]], axis=0)
    col = lax.broadcasted_iota(jnp.int32, (1, ATT_WINDOW), 1)
    s = jnp.where(col + (tb * ATT_QROWS - ATT_PAST) >= 0, s, NEG_MASK)
    p = jnp.exp(s - jnp.max(s, axis=-1, keepdims=True))
    o = _dot(p.astype(BF), vwin) / jnp.sum(p, axis=-1, keepdims=True)
    o_ref[...] = jnp.where(first, o[:ATT_QROWS], o[ATT_QROWS:])


def _attn_prompt(h, bias, bsz, t):
    rows = ATT_QROWS
    tbn = t // rows
    qc, kc, vc = OFF_Q_ATT // LANES, OFF_K_ATT // LANES, OFF_V_ATT // LANES
    band = jnp.concatenate(
        [jnp.pad(bias, ((0, 0), (0, 0), (c * CHUNK, ATT_WINDOW - ATT_BAND - c * CHUNK)), constant_values=NEG_MASK)
         for c in range(rows // CHUNK)], axis=1)

    def past(col0, back):
        return pl.BlockSpec((rows, LANES), lambda b, hp, tb: (b * tbn + jnp.maximum(tb - back, 0), col0 + hp))

    return pl.pallas_call(
        _attn_prompt_kernel,
        out_shape=jax.ShapeDtypeStruct((bsz * t, ATT_WIDTH), F32),
        grid=(bsz, ATT_HEADS // 2, tbn),
        in_specs=[past(qc, 0), past(kc, 2), past(kc, 1), past(kc, 0), past(vc, 2), past(vc, 1), past(vc, 0),
                  pl.BlockSpec((2, rows, ATT_WINDOW), lambda b, hp, tb: (hp, 0, 0))],
        out_specs=pl.BlockSpec((rows, LANES), lambda b, hp, tb: (b * tbn + tb, hp)),
        compiler_params=_params(("arbitrary", "arbitrary", "arbitrary")),
        name="attn_prompt",
    )(h, h, h, h, h, h, h, band)


def _attn_sample_kernel(q_ref, k_ref, v_ref, ck_ref, cv_ref, bias_ref, o_ref):
    kwin = jnp.concatenate([ck_ref[0], k_ref[...]], axis=0).astype(BF)
    vwin = jnp.concatenate([cv_ref[0], v_ref[...]], axis=0).astype(BF)
    q = q_ref[...] * (ATT_HEAD_DIM ** -0.5)
    o_ref[...] = _attend_chunk(q, kwin, vwin, bias_ref, None)


def _attn_sample(h, cache_k, cache_v, bias, bsz):
    qc, kc, vc = OFF_Q_ATT // LANES, OFF_K_ATT // LANES, OFF_V_ATT // LANES
    new = lambda col0: pl.BlockSpec((CHUNK, LANES), lambda b, hp: (b, col0 + hp))
    old = pl.BlockSpec((1, ATT_PAST, LANES), lambda b, hp: (b, 0, hp))
    return pl.pallas_call(
        _attn_sample_kernel,
        out_shape=jax.ShapeDtypeStruct((bsz * CHUNK, ATT_WIDTH), F32),
        grid=(bsz, ATT_HEADS // 2),
        in_specs=[new(qc), new(kc), new(vc), old, old,
                  pl.BlockSpec((2, CHUNK, ATT_BAND), lambda b, hp: (hp, 0, 0))],
        out_specs=pl.BlockSpec((CHUNK, LANES), lambda b, hp: (b, hp)),
        compiler_params=_params(("arbitrary", "arbitrary")),
        name="attn_sample",
    )(h, h, h, cache_k, cache_v, bias)


def _rel_bias_tile(table):
    dist = jnp.arange(-(CHUNK - 1), ATT_BAND)
    per_dist = table[jnp.clip(dist, -REL_CLIP, REL_CLIP) + REL_CLIP].astype(F32).T
    rev = per_dist[:, ::-1]
    n = dist.shape[0]
    return jnp.stack([rev[:, CHUNK - 1 - i:n - i] for i in range(CHUNK)], axis=1)


def _ret_kernel(q_ref, k_ref, v_ref, g_ref, cos_ref, sin_ref, intra_ref, qw_ref, kw_ref, dec_ref, bd_ref, gn_ref,
                s0_ref, y_ref, sf_ref, state, *, cpb):
    @pl.when(pl.program_id(1) == 0)
    def _():
        state[...] = s0_ref[0]

    lane = lax.broadcasted_iota(jnp.int32, (1, LANES), 1)
    first = lane < RET_DIM
    low_half = (lane % RET_DIM) < (RET_DIM // 2)

    for c in range(cpb):
        rows = slice(c * CHUNK, (c + 1) * CHUNK)
        cos = cos_ref[rows, :]
        sin = sin_ref[rows, :]

        def rope(x):
            swapped = jnp.where(low_half, pltpu.roll(x, LANES - RET_DIM // 2, 1), pltpu.roll(x, RET_DIM // 2, 1))
            return x * cos + swapped * sin

        for hp in range(RET_HEADS // 2):
            cols = slice(hp * LANES, (hp + 1) * LANES)
            q = rope(q_ref[rows, cols]) * (RET_DIM ** -0.5)
            k = rope(k_ref[rows, cols])
            kb = k.astype(BF)
            vb = v_ref[rows, cols].astype(BF)
            intra = []
            for hh in range(2):
                qm = jnp.where(first if hh == 0 else jnp.logical_not(first), q, 0.0).astype(BF)
                s = _dot_nt(qm, kb) * intra_ref[2 * hp + hh]
                intra.append(_dot(s.astype(BF), vb))
            s_prev = state[hp]
            o = jnp.where(first, intra[0], intra[1]) + _dot((q * qw_ref[hp]).astype(BF), s_prev.astype(BF))
            kv = _dot_tn((k * kw_ref[hp]).astype(BF), vb)
            state[hp] = dec_ref[hp] * s_prev + bd_ref[...] * kv
            sum_a = jnp.sum(jnp.where(first, o, 0.0), axis=-1, keepdims=True)
            sum_b = jnp.sum(o, axis=-1, keepdims=True) - sum_a
            d = o - jnp.where(first, sum_a, sum_b) * (1.0 / RET_DIM)
            sq = d * d
            var_a = jnp.sum(jnp.where(first, sq, 0.0), axis=-1, keepdims=True)
            var_b = jnp.sum(sq, axis=-1, keepdims=True) - var_a
            var = jnp.where(first, var_a, var_b) * (1.0 / RET_DIM)
            g = g_ref[rows, cols]
            y_ref[rows, cols] = d * lax.rsqrt(var + LN_EPS) * gn_ref[:, cols] * (g * jax.nn.sigmoid(g))
    sf_ref[0] = state[...]


def _retention(h, s0, rt, gn, bsz, t, cpb):
    rows = cpb * CHUNK
    tbn = t // rows
    width = RET_HEADS * RET_DIM
    col = lambda off: pl.BlockSpec((rows, width), lambda b, tb: (b * tbn + tb, off // width))
    tab = pl.BlockSpec((rows, LANES), lambda b, tb: (tb, 0))
    const3 = lambda shape: pl.BlockSpec(shape, lambda b, tb: (0, 0, 0))
    st = pl.BlockSpec((1, 2, LANES, LANES), lambda b, tb: (b, 0, 0, 0))
    return pl.pallas_call(
        functools.partial(_ret_kernel, cpb=cpb),
        out_shape=(jax.ShapeDtypeStruct((bsz * t, width), F32),
                   jax.ShapeDtypeStruct((bsz, 2, LANES, LANES), F32)),
        grid=(bsz, tbn),
        in_specs=[col(OFF_Q_RET), col(OFF_K_RET), col(OFF_V_RET), col(OFF_G_RET), tab, tab,
                  const3((RET_HEADS, CHUNK, CHUNK)), const3((2, CHUNK, LANES)), const3((2, CHUNK, LANES)),
                  const3((2, LANES, LANES)), pl.BlockSpec((LANES, LANES), lambda b, tb: (0, 0)),
                  pl.BlockSpec((1, width), lambda b, tb: (0, 0)), st],
        out_specs=(pl.BlockSpec((rows, width), lambda b, tb: (b * tbn + tb, 0)), st),
        scratch_shapes=[pltpu.VMEM((2, LANES, LANES), F32)],
        compiler_params=_params(("arbitrary", "arbitrary")),
        name="retention",
    )(h, h, h, h, rt["cos"], rt["sin"], rt["intra"], rt["qw"], rt["kw"], rt["dec"], rt["bd"], gn, s0)


def _retention_tables(pos):
    lg = jnp.log1p(-(2.0 ** (-5.0 - jnp.arange(RET_HEADS, dtype=F32))))
    i = jnp.arange(CHUNK, dtype=F32)
    intra = jnp.exp(lg[:, None, None] * jnp.abs(i[:, None] - i[None, :]))
    lane_head = jnp.arange(2 * LANES) // RET_DIM
    lg_lane = lg[lane_head]
    qw = jnp.exp(lg_lane[None, :] * (i[:, None] + 1.0))
    kw = jnp.exp(lg_lane[None, :] * (CHUNK - 1.0 - i)[:, None])
    pair = lambda m: jnp.stack([m[:, :LANES], m[:, LANES:]])
    same_head = (jnp.arange(LANES)[:, None] // RET_DIM) == (jnp.arange(LANES)[None, :] // RET_DIM)
    bd = same_head.astype(F32)
    dec_lane = jnp.exp(lg_lane * CHUNK)
    dec = jnp.stack([bd * dec_lane[:LANES][:, None], bd * dec_lane[LANES:][:, None]])
    half = RET_DIM // 2
    freqs = ROPE_BASE ** (-jnp.arange(half, dtype=F32) / half)
    ang = pos.astype(F32)[:, None] * freqs[None]
    cos = jnp.tile(jnp.cos(ang), (1, LANES // half))
    sin = jnp.tile(jnp.concatenate([-jnp.sin(ang), jnp.sin(ang)], axis=1), (1, LANES // RET_DIM))
    return {"intra": intra, "qw": pair(qw), "kw": pair(kw), "dec": dec, "bd": bd, "cos": cos, "sin": sin}


def _pair_states(s):
    b = s.shape[0]
    s = s.astype(F32).reshape(b, 2, 2, RET_DIM, RET_DIM)
    out = jnp.zeros((b, 2, 2, RET_DIM, 2, RET_DIM), F32)
    out = out.at[:, :, 0, :, 0, :].set(s[:, :, 0]).at[:, :, 1, :, 1, :].set(s[:, :, 1])
    return out.reshape(b, 2, LANES, LANES)


def _unpair_states(s):
    b = s.shape[0]
    s = s.reshape(b, 2, 2, RET_DIM, 2, RET_DIM)
    return jnp.stack([s[:, :, 0, :, 0, :], s[:, :, 1, :, 1, :]], axis=2).reshape(b, RET_HEADS, RET_DIM, RET_DIM)


def _merge_kernel(x_ref, g1_ref, g2_ref, g3_ref, ys_ref, ya_ref, yr_ref, ws_ref, wa_ref, wr_ref, wo_ref,
                  lg_ref, lb_ref, o_ref):
    merged = (jax.nn.sigmoid(g1_ref[...]) * _dot(ys_ref[...].astype(BF), ws_ref[...])
              + jax.nn.sigmoid(g2_ref[...]) * _dot(ya_ref[...].astype(BF), wa_ref[...])
              + jax.nn.sigmoid(g3_ref[...]) * _dot(yr_ref[...].astype(BF), wr_ref[...]))
    r = DN_ALPHA * x_ref[...] + _dot(merged.astype(BF), wo_ref[...])
    o_ref[...] = _layer_norm(r, lg_ref[...], lb_ref[...])


def _merge(x, h, y_ssm, y_att, y_ret, lw, tm):
    n = x.shape[0]
    row = lambda w: pl.BlockSpec((tm, w), lambda i: (i, 0))
    gate = lambda j: pl.BlockSpec((tm, D_MODEL), lambda i: (i, j))
    full = lambda a: pl.BlockSpec(a.shape, lambda i: (0, 0))
    ws = (lw["w_br_ssm"], lw["w_br_att"], lw["w_br_ret"], lw["w_o"], lw["ln1_g"], lw["ln1_b"])
    return pl.pallas_call(
        _merge_kernel,
        out_shape=jax.ShapeDtypeStruct((n, D_MODEL), F32),
        grid=(n // tm,),
        in_specs=[row(D_MODEL), gate(0), gate(1), gate(2), row(SSM_WIDTH), row(ATT_WIDTH),
                  row(RET_HEADS * RET_DIM)] + [full(a) for a in ws],
        out_specs=row(D_MODEL),
        compiler_params=_params(("arbitrary",)),
        name="merge",
    )(x, h, h, h, y_ssm, y_att, y_ret, *ws)


CAND_ROWS = 80


def _cand_flat_index():
    f = [b for b in range(16)]
    for a in range(1, 8):
        f += [a * 16 + b for b in range(8)]
    f += [a * 16 for a in range(8, 16)]
    return np.asarray(f, np.float32)


def _split_bf16(x):
    hi = x.astype(BF)
    return hi, (x - hi.astype(F32)).astype(BF)


def _take16(work, order):
    t = work.shape[1]
    row16 = lax.broadcasted_iota(jnp.int32, (PEER_TOPK, t), 0)
    rank = jnp.full(work.shape, float(PEER_TOPK), F32)
    vals = jnp.zeros((PEER_TOPK, t), F32)
    for r in range(PEER_TOPK):
        m = jnp.max(work, axis=0, keepdims=True)
        first = jnp.min(jnp.where(work == m, order, 1e9), axis=0, keepdims=True)
        sel = order == first
        rank = jnp.where(sel, float(r), rank)
        work = jnp.where(sel, -jnp.inf, work)
        vals = jnp.where(row16 == r, m, vals)
    return rank, vals


def _route_head_exact(s1, s2, fidx, outs, h):
    r2_ref, e2_ref, c1_ref, e1_ref = outs
    key_idx = lax.broadcasted_iota(jnp.int32, s1.shape, 0).astype(F32)
    rank1, v1 = _take16(s1, key_idx)
    rank2, v2 = _take16(s2, key_idx)
    blocks = [v1[0:1] + v2]
    for a in range(1, 8):
        blocks.append(v1[a:a + 1] + v2[0:8])
    blocks.append(v1[8:16] + v2[0:1])
    cand = jnp.concatenate(blocks, axis=0)
    rank_c, _ = _take16(cand, fidx)
    chosen = jnp.where(rank_c < float(PEER_TOPK), 1.0, 0.0)
    counts = [jnp.sum(chosen[0:16], axis=0, keepdims=True)]
    for a in range(1, 8):
        counts.append(jnp.sum(chosen[8 + 8 * a:16 + 8 * a], axis=0, keepdims=True))
    for a in range(8, 16):
        counts.append(chosen[64 + a:65 + a])
    c1 = jnp.zeros(s1.shape, F32)
    for a in range(PEER_TOPK):
        c1 = jnp.where(rank1 == float(a), counts[a], c1)
    z = jnp.sum(chosen * jnp.exp(cand - cand[0:1]), axis=0, keepdims=True)
    r2_ref[h] = rank2.astype(BF)
    e2_ref[h] = jnp.exp(s2 - v2[0:1]).astype(BF)
    c1_ref[h] = c1
    e1_ref[h] = jnp.exp(s1 - v1[0:1]) / z


def _batcher_pairs(n):
    pairs = []

    def merge(lo, m, r):
        step = r * 2
        if step < m:
            merge(lo, m, step)
            merge(lo + r, m, step)
            pairs.extend((i, i + r) for i in range(lo + r, lo + m - r, step))
        else:
            pairs.append((lo, lo + r))

    def sort(lo, m):
        if m > 1:
            sort(lo, m // 2)
            sort(lo + m // 2, m // 2)
            merge(lo, m, 1)

    sort(0, n)
    return pairs


SORT16_PAIRS = _batcher_pairs(PEER_TOPK)


def _exchange(x, i, j):
    a, b = x[i], x[j]
    if b is None:
        return
    if a is None:
        x[i], x[j] = b, None
    else:
        x[i], x[j] = jnp.maximum(a, b), jnp.minimum(a, b)


def _top16_values(tiles):
    x = list(tiles) + [None] * (PEER_TOPK - len(tiles))
    for i, j in SORT16_PAIRS:
        _exchange(x, i, j)
    for shift in (4, 2, 1):
        y = [None if v is None else pltpu.roll(v, shift, 0) for v in x]
        z = []
        for i in range(PEER_TOPK):
            a, b = x[i], y[PEER_TOPK - 1 - i]
            z.append(b if a is None else a if b is None else jnp.maximum(a, b))
        for stride in (8, 4, 2, 1):
            for i in range(PEER_TOPK):
                if i & stride == 0:
                    _exchange(z, i, i + stride)
        x = z
    return x


def _count_greater(v, x):
    b3 = v[7] > x
    b2 = jnp.where(b3, v[11], v[3]) > x
    b1 = jnp.where(b3, jnp.where(b2, v[13], v[9]), jnp.where(b2, v[5], v[1])) > x
    b0 = jnp.where(b3, jnp.where(b2, jnp.where(b1, v[14], v[12]), jnp.where(b1, v[10], v[8])),
                   jnp.where(b2, jnp.where(b1, v[6], v[4]), jnp.where(b1, v[2], v[0]))) > x
    lo = (jnp.where(b3, 8.0, 0.0) + jnp.where(b2, 4.0, 0.0)) + (jnp.where(b1, 2.0, 0.0) + jnp.where(b0, 1.0, 0.0))
    return jnp.where(v[15] > x, float(PEER_TOPK), lo)


def _column_sum(tiles):
    parts = list(tiles)
    while len(parts) > 1:
        parts = [parts[i] + parts[i + 1] for i in range(0, len(parts) - 1, 2)] + parts[len(parts) & ~1:]
    total = parts[0]
    for shift in (4, 2, 1):
        total = total + pltpu.roll(total, shift, 0)
    return total


def _route_head_fast(s1, s2, outs, h, cols):
    r2_ref, e2_ref, c1_ref, e1_ref = outs
    n_tiles = N_KEYS // SUBLANES
    t1 = [s1[i * SUBLANES:(i + 1) * SUBLANES] for i in range(n_tiles)]
    t2 = [s2[i * SUBLANES:(i + 1) * SUBLANES] for i in range(n_tiles)]
    v1 = _top16_values(t1)
    v2 = _top16_values(t2)
    sub = lax.broadcasted_iota(jnp.int32, t1[0].shape, 0)

    def by_sublane(vals):
        out = vals[0]
        for b in range(1, SUBLANES):
            out = jnp.where(sub == b, vals[b], out)
        return out

    v2_lo, v2_hi, v1_hi = by_sublane(v2[0:8]), by_sublane(v2[8:16]), by_sublane(v1[8:16])
    cand = [v1[0] + v2_lo, v1[0] + v2_hi] + [v1[a] + v2_lo for a in range(1, 8)] + [v1_hi + v2[0]]
    tau = _top16_values(cand)[PEER_TOPK - 1]
    chosen = [jnp.where(c >= tau, 1.0, 0.0) for c in cand]
    counts = [_column_sum(chosen[0:2])] + [_column_sum([chosen[1 + a]]) for a in range(1, 8)]
    counts += [jnp.broadcast_to(chosen[9][a:a + 1], sub.shape) for a in range(SUBLANES)]
    z = _column_sum([ch * jnp.exp(c - cand[0][0:1]) for ch, c in zip(chosen, cand)])
    inv_z = 1.0 / z

    rank2 = [_count_greater(v2, x) for x in t2]
    c1 = []
    for x in t1:
        c = jnp.zeros_like(x)
        for a in range(PEER_TOPK):
            c = jnp.where(x == v1[a], counts[a], c)
        c1.append(c)
    r2_ref[h, :, cols] = jnp.concatenate(rank2, axis=0).astype(BF)
    e2_ref[h, :, cols] = jnp.concatenate([jnp.exp(x - v2[0]) for x in t2], axis=0).astype(BF)
    c1_ref[h, :, cols] = jnp.concatenate(c1, axis=0)
    e1_ref[h, :, cols] = jnp.concatenate([jnp.exp(x - v1[0]) * inv_z for x in t1], axis=0)

    no_tie_rank_sum = float(sum(range(PEER_TOPK)) + (N_KEYS - PEER_TOPK) * PEER_TOPK)
    bad = jnp.where(_column_sum(rank2) != no_tie_rank_sum, 1.0, 0.0)
    bad = bad + jnp.where(_column_sum([jnp.where(x >= v1[PEER_TOPK - 1], 1.0, 0.0) for x in t1]) != PEER_TOPK, 1.0, 0.0)
    for a in range(PEER_TOPK - 1):
        bad = bad + jnp.where(v1[a] == v1[a + 1], 1.0, 0.0)
    return bad + jnp.where(_column_sum(chosen) != PEER_TOPK, 1.0, 0.0)


def _route_kernel(x_ref, wh_ref, wl_ref, kh_ref, kl_ref, fidx_ref, r2_ref, e2_ref, c1_ref, e1_ref):
    x = x_ref[...]
    t = x.shape[0]
    xh, xl = _split_bf16(x)
    q = _dot(xh, wh_ref[...]) + (_dot(xl, wh_ref[...]) + _dot(xh, wl_ref[...]))
    outs = (r2_ref, e2_ref, c1_ref, e1_ref)

    def scores(hs):
        qh, ql = _split_bf16(q[:, hs * PEER_HALF:(hs + 1) * PEER_HALF])
        return _dot_nt(kh_ref[hs], qh) + (_dot_nt(kl_ref[hs], qh) + _dot_nt(kh_ref[hs], ql))

    all_scores = [scores(hs) for hs in range(2 * PEER_HEADS)]
    bad = []
    for h in range(PEER_HEADS):
        flags = [_route_head_fast(all_scores[2 * h][:, c * LANES:(c + 1) * LANES],
                                  all_scores[2 * h + 1][:, c * LANES:(c + 1) * LANES], outs, h,
                                  slice(c * LANES, (c + 1) * LANES)) for c in range(t // LANES)]
        bad.append(sum(flags[1:], flags[0]))

    for h in range(PEER_HEADS):
        @pl.when(jnp.max(bad[h]) > 0.0)
        def _(h=h):
            _route_head_exact(all_scores[2 * h], all_scores[2 * h + 1], fidx_ref[...], outs, h)


def _route(x1, lw, tb):
    n = x1.shape[0]
    fidx = jnp.broadcast_to(jnp.asarray(_cand_flat_index())[:, None], (CAND_ROWS, tb))
    out = jax.ShapeDtypeStruct((PEER_HEADS, N_KEYS, n), F32)
    out_bf = jax.ShapeDtypeStruct((PEER_HEADS, N_KEYS, n), BF)
    ospec = pl.BlockSpec((PEER_HEADS, N_KEYS, tb), lambda i: (0, 0, i))
    full2 = lambda a: pl.BlockSpec(a.shape, lambda i: (0, 0))
    full3 = lambda a: pl.BlockSpec(a.shape, lambda i: (0, 0, 0))
    return pl.pallas_call(
        _route_kernel,
        out_shape=(out_bf, out_bf, out, out),
        grid=(n // tb,),
        in_specs=[pl.BlockSpec((tb, D_MODEL), lambda i: (i, 0)), full2(lw["wq_hi"]), full2(lw["wq_lo"]),
                  full3(lw["keys_hi"]), full3(lw["keys_lo"]), full2(fidx)],
        out_specs=(ospec, ospec, ospec, ospec),
        compiler_params=_params(("arbitrary",)),
        name="peer_route",
    )(x1, lw["wq_hi"], lw["wq_lo"], lw["keys_hi"], lw["keys_lo"], fidx)


PEER_EXPERT_BLOCK = 1024


def _peer_kernel(x_ref, u_ref, vt_ref, r2_ref, e2_ref, c1_ref, e1_ref, lg_ref, lb_ref, o_ref, xb, acc, wt, rowbuf, act):
    s = pl.program_id(1)
    last = pl.num_programs(1) - 1
    tb = x_ref.shape[0]

    @pl.when(s == 0)
    def _():
        xb[...] = x_ref[...].T.astype(BF)
        acc[...] = jnp.zeros_like(acc)

    keys_per_step = PEER_EXPERT_BLOCK // N_KEYS
    half = N_KEYS // 2
    group = keys_per_step // 2
    tiles = half // SUBLANES

    def spread(row):
        return jnp.broadcast_to(row[None], (tiles, SUBLANES, BF_CHUNK)).reshape(half, BF_CHUNK)

    @pl.when(s > 0)
    def _():
        first_key = pl.multiple_of((s - 1) * keys_per_step, keys_per_step)
        for h in range(PEER_HEADS):
            for src, ref in enumerate((c1_ref, e1_ref)):
                blk = ref[h, pl.ds(first_key, keys_per_step), :]
                for j in range(keys_per_step):
                    rowbuf[src, h, j] = jnp.broadcast_to(blk[j:j + 1], (SUBLANES, tb)).astype(BF)

        for tc in range(tb // BF_CHUNK):
            cols = slice(tc * BF_CHUNK, (tc + 1) * BF_CHUNK)
            for jg in range(keys_per_step // group):
                for kh in range(2):
                    keys = slice(kh * half, (kh + 1) * half)
                    w = [jnp.zeros((half, BF_CHUNK), BF) for _ in range(group)]
                    for h in range(PEER_HEADS):
                        r2 = r2_ref[h, keys, cols]
                        e2 = e2_ref[h, keys, cols]
                        for jj in range(group):
                            j = jg * group + jj
                            c1 = spread(rowbuf[0, h, j, :, cols])
                            e1 = spread(rowbuf[1, h, j, :, cols])
                            w[jj] = w[jj] + jnp.where(r2 < c1, e2 * e1, 0)
                    for jj in range(group):
                        lo = (jg * group + jj) * N_KEYS + kh * half
                        wt[lo:lo + half, cols] = w[jj]
        wt[...] = wt[...] * act[(s + 1) % 2]
        acc[...] += _dot(vt_ref[...], wt[...])

    @pl.when(s < last)
    def _():
        act[s % 2] = jax.nn.gelu(_dot(u_ref[...], xb[...]).astype(BF))

    @pl.when(s == last)
    def _():
        r = DN_ALPHA * x_ref[...] + acc[...].T
        o_ref[...] = _layer_norm(r, lg_ref[...], lb_ref[...])


def _peer(x1, route, lw, tb):
    n = x1.shape[0]
    eb = PEER_EXPERT_BLOCK
    nblk = N_EXPERTS // eb
    rspec = pl.BlockSpec((PEER_HEADS, N_KEYS, tb), lambda i, e: (0, 0, i))
    vec = pl.BlockSpec((1, D_MODEL), lambda i, e: (0, 0))
    return pl.pallas_call(
        _peer_kernel,
        out_shape=jax.ShapeDtypeStruct((n, D_MODEL), F32),
        grid=(n // tb, nblk + 1),
        in_specs=[pl.BlockSpec((tb, D_MODEL), lambda i, s: (i, 0)),
                  pl.BlockSpec((eb, D_MODEL), lambda i, s: (jnp.minimum(s, nblk - 1), 0)),
                  pl.BlockSpec((D_MODEL, eb), lambda i, s: (0, jnp.maximum(s - 1, 0))),
                  rspec, rspec, rspec, rspec, vec, vec],
        out_specs=pl.BlockSpec((tb, D_MODEL), lambda i, s: (i, 0)),
        scratch_shapes=[pltpu.VMEM((D_MODEL, tb), BF), pltpu.VMEM((D_MODEL, tb), F32), pltpu.VMEM((eb, tb), BF),
                        pltpu.VMEM((2, PEER_HEADS, eb // N_KEYS, SUBLANES, tb), BF),
                        pltpu.VMEM((2, eb, tb), BF)],
        compiler_params=_params(("arbitrary", "arbitrary")),
        name="peer_experts",
    )(x1, lw["peer_u"], lw["peer_vt"], *route, lw["ln2_g"], lw["ln2_b"])


def _ple_kernel(x_ref, pe_ref, wg_ref, wp_ref, lg_ref, lb_ref, o_ref):
    x = x_ref[...]
    ple = jax.nn.sigmoid(_dot(x.astype(BF), wg_ref[...])) * _dot(pe_ref[...].astype(BF), wp_ref[...])
    o_ref[...] = _layer_norm(DN_ALPHA * x + ple, lg_ref[...], lb_ref[...])


def _ple(x2, pe, lw, tm):
    n = x2.shape[0]
    full = lambda a: pl.BlockSpec(a.shape, lambda i: (0, 0))
    ws = (lw["ple_w_g"], lw["ple_w_p"], lw["ln3_g"], lw["ln3_b"])
    return pl.pallas_call(
        _ple_kernel,
        out_shape=jax.ShapeDtypeStruct((n, D_MODEL), F32),
        grid=(n // tm,),
        in_specs=[pl.BlockSpec((tm, D_MODEL), lambda i: (i, 0)), pl.BlockSpec((tm, PLE_DIM), lambda i: (i, 0))]
        + [full(a) for a in ws],
        out_specs=pl.BlockSpec((tm, D_MODEL), lambda i: (i, 0)),
        compiler_params=_params(("arbitrary",)),
        name="ple",
    )(x2, pe, *ws)


def _time_major(u, bsz, t, bp):
    u = jnp.transpose(u.reshape(bsz, t, -1), (1, 0, 2))
    return jnp.pad(u, ((0, 0), (0, bp - bsz), (0, 0))).reshape(t * bp, -1)


def _batch_major(y, bsz, t, bp):
    return jnp.transpose(y.reshape(t, bp, -1)[:, :bsz], (1, 0, 2)).reshape(bsz * t, -1)


def _layer(x, pe, h0_re, h0_im, cache_k, cache_v, s0, pos, lw, bsz, t, prompt, n_rows):
    n = bsz * t
    h = _in_projection(x, lw["w_in"], 512, IN_COLS // 2)

    bp = max(8, bsz)
    u_tb = _time_major(h[:, OFF_U_SSM:OFF_U_SSM + SSM_WIDTH], bsz, t, bp)
    h0 = jnp.concatenate([h0_re.reshape(bsz, SSM_FLAT), h0_im.reshape(bsz, SSM_FLAT)], axis=1).astype(F32)
    h0 = jnp.pad(h0, ((0, bp - bsz), (0, 0)))
    y_tb, h_last = _s5_mixer(u_tb, h0, lw["s5"], t, bp, CHUNK)
    y_ssm = _batch_major(y_tb, bsz, t, bp)
    new_re = h_last[:bsz, :SSM_FLAT].reshape(bsz, SSM_GROUPS, SSM_STATE)
    new_im = h_last[:bsz, SSM_FLAT:].reshape(bsz, SSM_GROUPS, SSM_STATE)

    k_att = h[:, OFF_K_ATT:OFF_K_ATT + ATT_WIDTH].reshape(bsz, t, ATT_HEADS, ATT_HEAD_DIM)
    v_att = h[:, OFF_V_ATT:OFF_V_ATT + ATT_WIDTH].reshape(bsz, t, ATT_HEADS, ATT_HEAD_DIM)
    if prompt:
        y_att = _attn_prompt(h, lw["rel_bias"], bsz, t)
        new_k, new_v = k_att[:, -n_rows:], v_att[:, -n_rows:]
    else:
        r = cache_k.shape[1]
        y_att = _attn_sample(h, cache_k.reshape(bsz, r, ATT_WIDTH), cache_v.reshape(bsz, r, ATT_WIDTH),
                             lw["rel_bias"], bsz)
        new_k, new_v = k_att, v_att

    y_ret, s_pair = _retention(h, _pair_states(s0), _retention_tables(pos), lw["ret_gn_g"], bsz, t,
                               4 if prompt else 1)
    s_fin = _unpair_states(s_pair)

    tok = 512 if n % 512 == 0 else 256
    x1 = _merge(x, h, y_ssm, y_att, y_ret, lw, 256)
    route = _route(x1, lw, 256)
    x2 = _peer(x1, route, lw, tok)
    x3 = _ple(x2, pe, lw, 256)
    return x3, (new_re, new_im, new_k, new_v, s_fin)


def _layer_weights(li, w_in, ssm_lam_re, ssm_lam_im, ssm_b_re, ssm_b_im, ssm_c_re, ssm_c_im, ssm_log_dt, ssm_d,
                   ssm_w_glu, ssm_b_glu, att_rel_bias, ret_gn_g, w_br_ssm, w_br_att, w_br_ret, w_o, ln1_g, ln1_b,
                   peer_w_q, peer_sub_keys, peer_u, peer_v, ln2_g, ln2_b, ple_w_g, ple_w_p, ln3_g, ln3_b):
    vec = lambda a: a[li].astype(F32).reshape(1, -1)
    w = w_in[li]
    wq_hi, wq_lo = _split_bf16(peer_w_q[li].astype(F32))
    keys_hi, keys_lo = _split_bf16(peer_sub_keys[li].astype(F32).reshape(2 * PEER_HEADS, N_KEYS, PEER_HALF))
    return {
        "w_in": jnp.concatenate([w[:, OLD_GATE_START:], w[:, :OLD_GATE_START]], axis=1).astype(BF),
        "s5": _s5_tables(ssm_lam_re[li], ssm_lam_im[li], ssm_b_re[li], ssm_b_im[li], ssm_c_re[li], ssm_c_im[li],
                         ssm_log_dt[li], ssm_d[li], ssm_w_glu[li], ssm_b_glu[li]),
        "rel_bias": _rel_bias_tile(att_rel_bias[li]),
        "ret_gn_g": vec(ret_gn_g),
        "w_br_ssm": w_br_ssm[li].astype(BF), "w_br_att": w_br_att[li].astype(BF), "w_br_ret": w_br_ret[li].astype(BF),
        "w_o": w_o[li].astype(BF), "ln1_g": vec(ln1_g), "ln1_b": vec(ln1_b),
        "wq_hi": wq_hi, "wq_lo": wq_lo, "keys_hi": keys_hi, "keys_lo": keys_lo,
        "peer_u": peer_u[li].astype(BF), "peer_vt": peer_v[li].astype(BF).T,
        "ln2_g": vec(ln2_g), "ln2_b": vec(ln2_b),
        "ple_w_g": ple_w_g[li].astype(BF), "ple_w_p": ple_w_p[li].astype(BF), "ln3_g": vec(ln3_g), "ln3_b": vec(ln3_b),
    }


def kernel(x_prompt, x_sample, p_prompt, p_sample, state_ssm_re, state_ssm_im, cache_attn_k, cache_attn_v, state_ret, w_in, ssm_lam_re, ssm_lam_im, ssm_b_re, ssm_b_im, ssm_c_re, ssm_c_im, ssm_log_dt, ssm_d, ssm_w_glu, ssm_b_glu, att_rel_bias, ret_gn_g, w_br_ssm, w_br_att, w_br_ret, w_o, ln1_g, ln1_b, peer_w_q, peer_sub_keys, peer_u, peer_v, ln2_g, ln2_b, ple_w_g, ple_w_p, ln3_g, ln3_b):
    bp, tp, _ = x_prompt.shape
    bs, ts, _ = x_sample.shape
    n_rows = cache_attn_k.shape[2]
    pos_p = jnp.arange(tp)
    pos_s = PAST_LEN + jnp.arange(ts)
    y_p = x_prompt.reshape(bp * tp, D_MODEL).astype(F32)
    y_s = x_sample.reshape(bs * ts, D_MODEL).astype(F32)
    new_p, new_s = [], []
    for li in range(DEPTH):
        lw = _layer_weights(li, w_in, ssm_lam_re, ssm_lam_im, ssm_b_re, ssm_b_im, ssm_c_re, ssm_c_im, ssm_log_dt,
                            ssm_d, ssm_w_glu, ssm_b_glu, att_rel_bias, ret_gn_g, w_br_ssm, w_br_att, w_br_ret, w_o,
                            ln1_g, ln1_b, peer_w_q, peer_sub_keys, peer_u, peer_v, ln2_g, ln2_b, ple_w_g, ple_w_p,
                            ln3_g, ln3_b)
        zero_ssm = jnp.zeros((bp, SSM_GROUPS, SSM_STATE), F32)
        zero_ret = jnp.zeros((bp, RET_HEADS, RET_DIM, RET_DIM), F32)
        y_p, st = _layer(y_p, p_prompt[li].reshape(bp * tp, PLE_DIM), zero_ssm, zero_ssm, None, None, zero_ret,
                         pos_p, lw, bp, tp, True, n_rows)
        new_p.append(st)
        y_s, st = _layer(y_s, p_sample[li].reshape(bs * ts, PLE_DIM), state_ssm_re[li], state_ssm_im[li],
                         cache_attn_k[li], cache_attn_v[li], state_ret[li], pos_s, lw, bs, ts, False, n_rows)
        new_s.append(st)
    stack = lambda states, k: jnp.stack([s[k] for s in states])
    return (y_p.reshape(bp, tp, D_MODEL), y_s.reshape(bs, ts, D_MODEL),
            stack(new_p, 0), stack(new_p, 1), stack(new_p, 2), stack(new_p, 3), stack(new_p, 4),
            stack(new_s, 0), stack(new_s, 1), stack(new_s, 2), stack(new_s, 3), stack(new_s, 4))
```

```python
import functools

import jax
import jax.numpy as jnp
import numpy as np
from jax import lax
from jax.experimental import pallas as pl
from jax.experimental.pallas import tpu as pltpu

F32 = jnp.float32
BF = jnp.bfloat16

D_MODEL = 1024
DEPTH = 2
PAST_LEN = 2048
CHUNK = 64
PLE_DIM = 256
SSM_WIDTH = 256
SSM_GROUP = 16
SSM_GROUPS = 16
SSM_STATE = 64
SSM_FLAT = SSM_GROUPS * SSM_STATE
ATT_HEADS = 8
ATT_HEAD_DIM = 64
ATT_WIDTH = 512
ATT_PAST = 8 * CHUNK
ATT_BAND = ATT_PAST + CHUNK
REL_CLIP = 256
RET_HEADS = 4
RET_DIM = 64
ROPE_BASE = 10000.0
PEER_HEADS = 8
PEER_HALF = 128
N_KEYS = 128
N_EXPERTS = N_KEYS * N_KEYS
PEER_TOPK = 16
SUBLANES = 8
BF_CHUNK = 256
DN_ALPHA = (2 * DEPTH) ** 0.25
LN_EPS = 1e-5
LANES = 128
NEG_MASK = float(jnp.finfo(jnp.float32).min)

IN_COLS = 5888
OFF_GATES = 0
OFF_U_SSM = 3072
OFF_Q_ATT = 3328
OFF_K_ATT = 3840
OFF_V_ATT = 4352
OFF_Q_RET = 4864
OFF_K_RET = 5120
OFF_V_RET = 5376
OFF_G_RET = 5632
OLD_GATE_START = 2816

VMEM_LIMIT = 48 * 1024 * 1024


def _params(sem):
    return pltpu.CompilerParams(dimension_semantics=sem, vmem_limit_bytes=VMEM_LIMIT)


def _dot(a, b):
    return jnp.dot(a, b, preferred_element_type=F32)


def _dot_nt(a, b):
    return lax.dot_general(a, b, (((1,), (1,)), ((), ())), preferred_element_type=F32)


def _dot_tn(a, b):
    return lax.dot_general(a, b, (((0,), (0,)), ((), ())), preferred_element_type=F32)


def _layer_norm(r, g, b):
    mu = jnp.mean(r, axis=-1, keepdims=True)
    d = r - mu
    var = jnp.mean(d * d, axis=-1, keepdims=True)
    return d * lax.rsqrt(var + LN_EPS) * g + b


def _matmul_kernel(x_ref, w_ref, o_ref):
    o_ref[...] = _dot(x_ref[...].astype(BF), w_ref[...])


def _in_projection(x, w, tm, tn):
    m, k = x.shape
    n = w.shape[1]
    return pl.pallas_call(
        _matmul_kernel,
        out_shape=jax.ShapeDtypeStruct((m, n), F32),
        grid=(n // tn, m // tm),
        in_specs=[pl.BlockSpec((tm, k), lambda j, i: (i, 0)),
                  pl.BlockSpec((k, tn), lambda j, i: (0, j))],
        out_specs=pl.BlockSpec((tm, tn), lambda j, i: (i, j)),
        compiler_params=_params(("arbitrary", "arbitrary")),
        name="in_projection",
    )(x, w)


def _s5_kernel(u_ref, bm_ref, a_ref, cm_ref, d_ref, wg_ref, bg_ref, h0_ref, y_ref, hl_ref, hbuf, hst, *, steps, bp):
    @pl.when(pl.program_id(0) == 0)
    def _():
        hst[...] = h0_ref[...]

    u = u_ref[...]
    hbuf[...] = _dot(u.astype(BF), bm_ref[...])
    a_re = jnp.broadcast_to(a_ref[0:1, :], (bp, SSM_FLAT))
    a_im = jnp.broadcast_to(a_ref[1:2, :], (bp, SSM_FLAT))

    def step(t, carry):
        h_re, h_im = carry
        r0 = pl.multiple_of(t * bp, bp)
        n_re = a_re * h_re - a_im * h_im + hbuf[pl.ds(r0, bp), 0:SSM_FLAT]
        n_im = a_re * h_im + a_im * h_re + hbuf[pl.ds(r0, bp), SSM_FLAT:2 * SSM_FLAT]
        hbuf[pl.ds(r0, bp), 0:SSM_FLAT] = n_re
        hbuf[pl.ds(r0, bp), SSM_FLAT:2 * SSM_FLAT] = n_im
        return n_re, n_im

    h_re, h_im = lax.fori_loop(0, steps, step, (hst[:, 0:SSM_FLAT], hst[:, SSM_FLAT:2 * SSM_FLAT]))
    hst[:, 0:SSM_FLAT] = h_re
    hst[:, SSM_FLAT:2 * SSM_FLAT] = h_im
    hl_ref[...] = hst[...]
    y = _dot(hbuf[...].astype(BF), cm_ref[...]) + d_ref[...] * u
    z = jax.nn.gelu(y)
    y_ref[...] = z * jax.nn.sigmoid(_dot(z.astype(BF), wg_ref[...]) + bg_ref[...])


def _s5_mixer(u_tb, h0, sp, n_steps, bp, steps):
    rows = steps * bp
    kern = functools.partial(_s5_kernel, steps=steps, bp=bp)
    const = lambda c: (0, 0)
    return pl.pallas_call(
        kern,
        out_shape=(jax.ShapeDtypeStruct((n_steps * bp, SSM_WIDTH), F32),
                   jax.ShapeDtypeStruct((bp, 2 * SSM_FLAT), F32)),
        grid=(n_steps // steps,),
        in_specs=[pl.BlockSpec((rows, SSM_WIDTH), lambda c: (c, 0)),
                  pl.BlockSpec((SSM_WIDTH, 2 * SSM_FLAT), const),
                  pl.BlockSpec((2, SSM_FLAT), const),
                  pl.BlockSpec((2 * SSM_FLAT, SSM_WIDTH), const),
                  pl.BlockSpec((1, SSM_WIDTH), const),
                  pl.BlockSpec((SSM_WIDTH, SSM_WIDTH), const),
                  pl.BlockSpec((1, SSM_WIDTH), const),
                  pl.BlockSpec((bp, 2 * SSM_FLAT), const)],
        out_specs=(pl.BlockSpec((rows, SSM_WIDTH), lambda c: (c, 0)),
                   pl.BlockSpec((bp, 2 * SSM_FLAT), const)),
        scratch_shapes=[pltpu.VMEM((rows, 2 * SSM_FLAT), F32), pltpu.VMEM((bp, 2 * SSM_FLAT), F32)],
        compiler_params=_params(("arbitrary",)),
        name="s5_mixer",
    )(u_tb, sp["bmat"], sp["a"], sp["cmat"], sp["d"], sp["w_glu"], sp["b_glu"], h0)


def _s5_tables(lam_re, lam_im, b_re, b_im, c_re, c_im, log_dt, d_skip, w_glu, b_glu):
    lr, li = lam_re.astype(F32), lam_im.astype(F32)
    dt = jnp.exp(log_dt.astype(F32))[:, None]
    mag = jnp.exp(lr * dt)
    a_re, a_im = mag * jnp.cos(li * dt), mag * jnp.sin(li * dt)
    den = lr * lr + li * li
    k_re = ((a_re - 1.0) * lr + a_im * li) / den
    k_im = (a_im * lr - (a_re - 1.0) * li) / den
    br, bi = b_re.astype(F32), b_im.astype(F32)
    bbar_re = k_re[..., None] * br - k_im[..., None] * bi
    bbar_im = k_re[..., None] * bi + k_im[..., None] * br
    eye = jnp.eye(SSM_GROUPS, dtype=F32)

    def in_blockdiag(m):
        m = jnp.transpose(m, (0, 2, 1))
        return (m[:, :, None, :] * eye[:, None, :, None]).reshape(SSM_WIDTH, SSM_FLAT)

    def out_blockdiag(m):
        m = jnp.transpose(m, (0, 2, 1))
        return (m[:, :, None, :] * eye[:, None, :, None]).reshape(SSM_FLAT, SSM_WIDTH)

    bmat = jnp.concatenate([in_blockdiag(bbar_re), in_blockdiag(bbar_im)], axis=1)
    cmat = jnp.concatenate([out_blockdiag(c_re.astype(F32)), -out_blockdiag(c_im.astype(F32))], axis=0)
    a = jnp.stack([a_re.reshape(SSM_FLAT), a_im.reshape(SSM_FLAT)])
    return {"bmat": bmat.astype(BF), "cmat": cmat.astype(BF), "a": a,
            "d": d_skip.astype(F32).reshape(1, SSM_WIDTH), "w_glu": w_glu.astype(BF),
            "b_glu": b_glu.astype(F32).reshape(1, SSM_WIDTH)}


ATT_QROWS = 4 * CHUNK
ATT_WINDOW = ATT_PAST + ATT_QROWS


def _attn_prompt_kernel(q_ref, k0_ref, k1_ref, k2_ref, v0_ref, v1_ref, v2_ref, bias_ref, o_ref):
    tb = pl.program_id(2)
    kwin = jnp.concatenate([k0_ref[...], k1_ref[...], k2_ref[...]], axis=0).astype(BF)
    vwin = jnp.concatenate([v0_ref[...], v1_ref[...], v2_ref[...]], axis=0).astype(BF)
    first = lax.broadcasted_iota(jnp.int32, (1, LANES), 1) < ATT_HEAD_DIM
    q = q_ref[...] * (ATT_HEAD_DIM ** -0.5)
    q2 = jnp.concatenate([jnp.where(first, q, 0.0), jnp.where(first, 0.0, q)], axis=0).astype(BF)
    s = _dot_nt(q2, kwin) + jnp.concatenate([bias_ref[0], bias_ref[1]], axis=0)
    col = lax.broadcasted_iota(jnp.int32, (1, ATT_WINDOW), 1)
    s = jnp.where(col + (tb * ATT_QROWS - ATT_PAST) >= 0, s, NEG_MASK)
    p = jnp.exp(s - jnp.max(s, axis=-1, keepdims=True))
    o = _dot(p.astype(BF), vwin) / jnp.sum(p, axis=-1, keepdims=True)
    o_ref[...] = jnp.where(first, o[:ATT_QROWS], o[ATT_QROWS:])


def _attn_prompt(h, bias, bsz, t):
    rows = ATT_QROWS
    tbn = t // rows
    qc, kc, vc = OFF_Q_ATT // LANES, OFF_K_ATT // LANES, OFF_V_ATT // LANES
    band = jnp.concatenate(
        [jnp.pad(bias, ((0, 0), (0, 0), (c * CHUNK, ATT_WINDOW - ATT_BAND - c * CHUNK)), constant_values=NEG_MASK)
         for c in range(rows // CHUNK)], axis=1)

    def past(col0, back):
        return pl.BlockSpec((rows, LANES), lambda b, hp, tb: (b * tbn + jnp.maximum(tb - back, 0), col0 + hp))

    return pl.pallas_call(
        _attn_prompt_kernel,
        out_shape=jax.ShapeDtypeStruct((bsz * t, ATT_WIDTH), F32),
        grid=(bsz, ATT_HEADS // 2, tbn),
        in_specs=[past(qc, 0), past(kc, 2), past(kc, 1), past(kc, 0), past(vc, 2), past(vc, 1), past(vc, 0),
                  pl.BlockSpec((2, rows, ATT_WINDOW), lambda b, hp, tb: (hp, 0, 0))],
        out_specs=pl.BlockSpec((rows, LANES), lambda b, hp, tb: (b * tbn + tb, hp)),
        compiler_params=_params(("arbitrary", "arbitrary", "arbitrary")),
        name="attn_prompt",
    )(h, h, h, h, h, h, h, band)


def _attn_sample_kernel(q_ref, k_ref, v_ref, ck_ref, cv_ref, bias_ref, o_ref):
    kwin = jnp.concatenate([ck_ref[0], k_ref[...]], axis=0).astype(BF)
    vwin = jnp.concatenate([cv_ref[0], v_ref[...]], axis=0).astype(BF)
    first = lax.broadcasted_iota(jnp.int32, (1, LANES), 1) < ATT_HEAD_DIM
    q = q_ref[...] * (ATT_HEAD_DIM ** -0.5)
    q2 = jnp.concatenate([jnp.where(first, q, 0.0), jnp.where(first, 0.0, q)], axis=0).astype(BF)
    s = _dot_nt(q2, kwin) + jnp.concatenate([bias_ref[0], bias_ref[1]], axis=0)
    p = jnp.exp(s - jnp.max(s, axis=-1, keepdims=True))
    o = _dot(p.astype(BF), vwin) / jnp.sum(p, axis=-1, keepdims=True)
    o_ref[...] = jnp.where(first, o[:CHUNK], o[CHUNK:])


def _attn_sample(h, cache_k, cache_v, bias, bsz):
    qc, kc, vc = OFF_Q_ATT // LANES, OFF_K_ATT // LANES, OFF_V_ATT // LANES
    new = lambda col0: pl.BlockSpec((CHUNK, LANES), lambda b, hp: (b, col0 + hp))
    old = pl.BlockSpec((1, ATT_PAST, LANES), lambda b, hp: (b, 0, hp))
    return pl.pallas_call(
        _attn_sample_kernel,
        out_shape=jax.ShapeDtypeStruct((bsz * CHUNK, ATT_WIDTH), F32),
        grid=(bsz, ATT_HEADS // 2),
        in_specs=[new(qc), new(kc), new(vc), old, old,
                  pl.BlockSpec((2, CHUNK, ATT_BAND), lambda b, hp: (hp, 0, 0))],
        out_specs=pl.BlockSpec((CHUNK, LANES), lambda b, hp: (b, hp)),
        compiler_params=_params(("arbitrary", "arbitrary")),
        name="attn_sample",
    )(h, h, h, cache_k, cache_v, bias)


def _rel_bias_tile(table):
    dist = jnp.arange(-(CHUNK - 1), ATT_BAND)
    per_dist = table[jnp.clip(dist, -REL_CLIP, REL_CLIP) + REL_CLIP].astype(F32).T
    rev = per_dist[:, ::-1]
    n = dist.shape[0]
    return jnp.stack([rev[:, CHUNK - 1 - i:n - i] for i in range(CHUNK)], axis=1)


def _ret_kernel(q_ref, k_ref, v_ref, g_ref, cos_ref, sin_ref, intra_ref, qw_ref, kw_ref, dec_ref, bd_ref, gn_ref,
                s0_ref, y_ref, sf_ref, state, *, cpb):
    @pl.when(pl.program_id(1) == 0)
    def _():
        state[...] = s0_ref[0]

    lane = lax.broadcasted_iota(jnp.int32, (1, LANES), 1)
    first = lane < RET_DIM
    low_half = (lane % RET_DIM) < (RET_DIM // 2)

    for c in range(cpb):
        rows = slice(c * CHUNK, (c + 1) * CHUNK)
        cos = cos_ref[rows, :]
        sin = sin_ref[rows, :]

        def rope(x):
            swapped = jnp.where(low_half, pltpu.roll(x, LANES - RET_DIM // 2, 1), pltpu.roll(x, RET_DIM // 2, 1))
            return x * cos + swapped * sin

        for hp in range(RET_HEADS // 2):
            cols = slice(hp * LANES, (hp + 1) * LANES)
            q = rope(q_ref[rows, cols]) * (RET_DIM ** -0.5)
            k = rope(k_ref[rows, cols])
            kb = k.astype(BF)
            vb = v_ref[rows, cols].astype(BF)
            intra = []
            for hh in range(2):
                qm = jnp.where(first if hh == 0 else jnp.logical_not(first), q, 0.0).astype(BF)
                s = _dot_nt(qm, kb) * intra_ref[2 * hp + hh]
                intra.append(_dot(s.astype(BF), vb))
            s_prev = state[hp]
            o = jnp.where(first, intra[0], intra[1]) + _dot((q * qw_ref[hp]).astype(BF), s_prev.astype(BF))
            kv = _dot_tn((k * kw_ref[hp]).astype(BF), vb)
            state[hp] = dec_ref[hp] * s_prev + bd_ref[...] * kv
            sum_a = jnp.sum(jnp.where(first, o, 0.0), axis=-1, keepdims=True)
            sum_b = jnp.sum(o, axis=-1, keepdims=True) - sum_a
            d = o - jnp.where(first, sum_a, sum_b) * (1.0 / RET_DIM)
            sq = d * d
            var_a = jnp.sum(jnp.where(first, sq, 0.0), axis=-1, keepdims=True)
            var_b = jnp.sum(sq, axis=-1, keepdims=True) - var_a
            var = jnp.where(first, var_a, var_b) * (1.0 / RET_DIM)
            g = g_ref[rows, cols]
            y_ref[rows, cols] = d * lax.rsqrt(var + LN_EPS) * gn_ref[:, cols] * (g * jax.nn.sigmoid(g))
    sf_ref[0] = state[...]


def _retention(h, s0, rt, gn, bsz, t, cpb):
    rows = cpb * CHUNK
    tbn = t // rows
    width = RET_HEADS * RET_DIM
    col = lambda off: pl.BlockSpec((rows, width), lambda b, tb: (b * tbn + tb, off // width))
    tab = pl.BlockSpec((rows, LANES), lambda b, tb: (tb, 0))
    const3 = lambda shape: pl.BlockSpec(shape, lambda b, tb: (0, 0, 0))
    st = pl.BlockSpec((1, 2, LANES, LANES), lambda b, tb: (b, 0, 0, 0))
    return pl.pallas_call(
        functools.partial(_ret_kernel, cpb=cpb),
        out_shape=(jax.ShapeDtypeStruct((bsz * t, width), F32),
                   jax.ShapeDtypeStruct((bsz, 2, LANES, LANES), F32)),
        grid=(bsz, tbn),
        in_specs=[col(OFF_Q_RET), col(OFF_K_RET), col(OFF_V_RET), col(OFF_G_RET), tab, tab,
                  const3((RET_HEADS, CHUNK, CHUNK)), const3((2, CHUNK, LANES)), const3((2, CHUNK, LANES)),
                  const3((2, LANES, LANES)), pl.BlockSpec((LANES, LANES), lambda b, tb: (0, 0)),
                  pl.BlockSpec((1, width), lambda b, tb: (0, 0)), st],
        out_specs=(pl.BlockSpec((rows, width), lambda b, tb: (b * tbn + tb, 0)), st),
        scratch_shapes=[pltpu.VMEM((2, LANES, LANES), F32)],
        compiler_params=_params(("arbitrary", "arbitrary")),
        name="retention",
    )(h, h, h, h, rt["cos"], rt["sin"], rt["intra"], rt["qw"], rt["kw"], rt["dec"], rt["bd"], gn, s0)


def _retention_tables(pos):
    lg = jnp.log1p(-(2.0 ** (-5.0 - jnp.arange(RET_HEADS, dtype=F32))))
    i = jnp.arange(CHUNK, dtype=F32)
    intra = jnp.exp(lg[:, None, None] * jnp.abs(i[:, None] - i[None, :]))
    lane_head = jnp.arange(2 * LANES) // RET_DIM
    lg_lane = lg[lane_head]
    qw = jnp.exp(lg_lane[None, :] * (i[:, None] + 1.0))
    kw = jnp.exp(lg_lane[None, :] * (CHUNK - 1.0 - i)[:, None])
    pair = lambda m: jnp.stack([m[:, :LANES], m[:, LANES:]])
    same_head = (jnp.arange(LANES)[:, None] // RET_DIM) == (jnp.arange(LANES)[None, :] // RET_DIM)
    bd = same_head.astype(F32)
    dec_lane = jnp.exp(lg_lane * CHUNK)
    dec = jnp.stack([bd * dec_lane[:LANES][:, None], bd * dec_lane[LANES:][:, None]])
    half = RET_DIM // 2
    freqs = ROPE_BASE ** (-jnp.arange(half, dtype=F32) / half)
    ang = pos.astype(F32)[:, None] * freqs[None]
    cos = jnp.tile(jnp.cos(ang), (1, LANES // half))
    sin = jnp.tile(jnp.concatenate([-jnp.sin(ang), jnp.sin(ang)], axis=1), (1, LANES // RET_DIM))
    return {"intra": intra, "qw": pair(qw), "kw": pair(kw), "dec": dec, "bd": bd, "cos": cos, "sin": sin}


def _pair_states(s):
    b = s.shape[0]
    s = s.astype(F32).reshape(b, 2, 2, RET_DIM, RET_DIM)
    out = jnp.zeros((b, 2, 2, RET_DIM, 2, RET_DIM), F32)
    out = out.at[:, :, 0, :, 0, :].set(s[:, :, 0]).at[:, :, 1, :, 1, :].set(s[:, :, 1])
    return out.reshape(b, 2, LANES, LANES)


def _unpair_states(s):
    b = s.shape[0]
    s = s.reshape(b, 2, 2, RET_DIM, 2, RET_DIM)
    return jnp.stack([s[:, :, 0, :, 0, :], s[:, :, 1, :, 1, :]], axis=2).reshape(b, RET_HEADS, RET_DIM, RET_DIM)


def _merge_kernel(x_ref, g1_ref, g2_ref, g3_ref, ys_ref, ya_ref, yr_ref, ws_ref, wa_ref, wr_ref, wo_ref,
                  lg_ref, lb_ref, o_ref):
    merged = (jax.nn.sigmoid(g1_ref[...]) * _dot(ys_ref[...].astype(BF), ws_ref[...])
              + jax.nn.sigmoid(g2_ref[...]) * _dot(ya_ref[...].astype(BF), wa_ref[...])
              + jax.nn.sigmoid(g3_ref[...]) * _dot(yr_ref[...].astype(BF), wr_ref[...]))
    r = DN_ALPHA * x_ref[...] + _dot(merged.astype(BF), wo_ref[...])
    o_ref[...] = _layer_norm(r, lg_ref[...], lb_ref[...])


def _merge(x, h, y_ssm, y_att, y_ret, lw, tm):
    n = x.shape[0]
    row = lambda w: pl.BlockSpec((tm, w), lambda i: (i, 0))
    gate = lambda j: pl.BlockSpec((tm, D_MODEL), lambda i: (i, j))
    full = lambda a: pl.BlockSpec(a.shape, lambda i: (0, 0))
    ws = (lw["w_br_ssm"], lw["w_br_att"], lw["w_br_ret"], lw["w_o"], lw["ln1_g"], lw["ln1_b"])
    return pl.pallas_call(
        _merge_kernel,
        out_shape=jax.ShapeDtypeStruct((n, D_MODEL), F32),
        grid=(n // tm,),
        in_specs=[row(D_MODEL), gate(0), gate(1), gate(2), row(SSM_WIDTH), row(ATT_WIDTH),
                  row(RET_HEADS * RET_DIM)] + [full(a) for a in ws],
        out_specs=row(D_MODEL),
        compiler_params=_params(("arbitrary",)),
        name="merge",
    )(x, h, h, h, y_ssm, y_att, y_ret, *ws)


CAND_ROWS = 80


def _cand_flat_index():
    f = [b for b in range(16)]
    for a in range(1, 8):
        f += [a * 16 + b for b in range(8)]
    f += [a * 16 for a in range(8, 16)]
    return np.asarray(f, np.float32)


def _split_bf16(x):
    hi = x.astype(BF)
    return hi, (x - hi.astype(F32)).astype(BF)


def _take16(work, order):
    t = work.shape[1]
    row16 = lax.broadcasted_iota(jnp.int32, (PEER_TOPK, t), 0)
    rank = jnp.full(work.shape, float(PEER_TOPK), F32)
    vals = jnp.zeros((PEER_TOPK, t), F32)
    for r in range(PEER_TOPK):
        m = jnp.max(work, axis=0, keepdims=True)
        first = jnp.min(jnp.where(work == m, order, 1e9), axis=0, keepdims=True)
        sel = order == first
        rank = jnp.where(sel, float(r), rank)
        work = jnp.where(sel, -jnp.inf, work)
        vals = jnp.where(row16 == r, m, vals)
    return rank, vals


def _route_head_exact(s1, s2, fidx, outs, h):
    r2_ref, e2_ref, c1_ref, e1_ref = outs
    key_idx = lax.broadcasted_iota(jnp.int32, s1.shape, 0).astype(F32)
    rank1, v1 = _take16(s1, key_idx)
    rank2, v2 = _take16(s2, key_idx)
    blocks = [v1[0:1] + v2]
    for a in range(1, 8):
        blocks.append(v1[a:a + 1] + v2[0:8])
    blocks.append(v1[8:16] + v2[0:1])
    cand = jnp.concatenate(blocks, axis=0)
    rank_c, _ = _take16(cand, fidx)
    chosen = jnp.where(rank_c < float(PEER_TOPK), 1.0, 0.0)
    counts = [jnp.sum(chosen[0:16], axis=0, keepdims=True)]
    for a in range(1, 8):
        counts.append(jnp.sum(chosen[8 + 8 * a:16 + 8 * a], axis=0, keepdims=True))
    for a in range(8, 16):
        counts.append(chosen[64 + a:65 + a])
    c1 = jnp.zeros(s1.shape, F32)
    for a in range(PEER_TOPK):
        c1 = jnp.where(rank1 == float(a), counts[a], c1)
    z = jnp.sum(chosen * jnp.exp(cand - cand[0:1]), axis=0, keepdims=True)
    r2_ref[h] = rank2.astype(BF)
    e2_ref[h] = jnp.exp(s2 - v2[0:1]).astype(BF)
    c1_ref[h] = c1
    e1_ref[h] = jnp.exp(s1 - v1[0:1]) / z


def _batcher_pairs(n):
    pairs = []

    def merge(lo, m, r):
        step = r * 2
        if step < m:
            merge(lo, m, step)
            merge(lo + r, m, step)
            pairs.extend((i, i + r) for i in range(lo + r, lo + m - r, step))
        else:
            pairs.append((lo, lo + r))

    def sort(lo, m):
        if m > 1:
            sort(lo, m // 2)
            sort(lo + m // 2, m // 2)
            merge(lo, m, 1)

    sort(0, n)
    return pairs


SORT16_PAIRS = _batcher_pairs(PEER_TOPK)


def _exchange(x, i, j):
    a, b = x[i], x[j]
    if b is None:
        return
    if a is None:
        x[i], x[j] = b, None
    else:
        x[i], x[j] = jnp.maximum(a, b), jnp.minimum(a, b)


def _top16_values(tiles):
    x = list(tiles) + [None] * (PEER_TOPK - len(tiles))
    for i, j in SORT16_PAIRS:
        _exchange(x, i, j)
    for shift in (4, 2, 1):
        y = [None if v is None else pltpu.roll(v, shift, 0) for v in x]
        z = []
        for i in range(PEER_TOPK):
            a, b = x[i], y[PEER_TOPK - 1 - i]
            z.append(b if a is None else a if b is None else jnp.maximum(a, b))
        for stride in (8, 4, 2, 1):
            for i in range(PEER_TOPK):
                if i & stride == 0:
                    _exchange(z, i, i + stride)
        x = z
    return x


def _count_greater(v, x):
    b3 = v[7] > x
    b2 = jnp.where(b3, v[11], v[3]) > x
    b1 = jnp.where(b3, jnp.where(b2, v[13], v[9]), jnp.where(b2, v[5], v[1])) > x
    b0 = jnp.where(b3, jnp.where(b2, jnp.where(b1, v[14], v[12]), jnp.where(b1, v[10], v[8])),
                   jnp.where(b2, jnp.where(b1, v[6], v[4]), jnp.where(b1, v[2], v[0]))) > x
    lo = (jnp.where(b3, 8.0, 0.0) + jnp.where(b2, 4.0, 0.0)) + (jnp.where(b1, 2.0, 0.0) + jnp.where(b0, 1.0, 0.0))
    return jnp.where(v[15] > x, float(PEER_TOPK), lo)


def _column_sum(tiles):
    parts = list(tiles)
    while len(parts) > 1:
        parts = [parts[i] + parts[i + 1] for i in range(0, len(parts) - 1, 2)] + parts[len(parts) & ~1:]
    total = parts[0]
    for shift in (4, 2, 1):
        total = total + pltpu.roll(total, shift, 0)
    return total


def _route_head_fast(s1, s2, outs, h, cols):
    r2_ref, e2_ref, c1_ref, e1_ref = outs
    n_tiles = N_KEYS // SUBLANES
    t1 = [s1[i * SUBLANES:(i + 1) * SUBLANES] for i in range(n_tiles)]
    t2 = [s2[i * SUBLANES:(i + 1) * SUBLANES] for i in range(n_tiles)]
    v1 = _top16_values(t1)
    v2 = _top16_values(t2)
    sub = lax.broadcasted_iota(jnp.int32, t1[0].shape, 0)

    def by_sublane(vals):
        out = vals[0]
        for b in range(1, SUBLANES):
            out = jnp.where(sub == b, vals[b], out)
        return out

    v2_lo, v2_hi, v1_hi = by_sublane(v2[0:8]), by_sublane(v2[8:16]), by_sublane(v1[8:16])
    cand = [v1[0] + v2_lo, v1[0] + v2_hi] + [v1[a] + v2_lo for a in range(1, 8)] + [v1_hi + v2[0]]
    tau = _top16_values(cand)[PEER_TOPK - 1]
    chosen = [jnp.where(c >= tau, 1.0, 0.0) for c in cand]
    counts = [_column_sum(chosen[0:2])] + [_column_sum([chosen[1 + a]]) for a in range(1, 8)]
    counts += [jnp.broadcast_to(chosen[9][a:a + 1], sub.shape) for a in range(SUBLANES)]
    z = _column_sum([ch * jnp.exp(c - cand[0][0:1]) for ch, c in zip(chosen, cand)])
    inv_z = 1.0 / z

    rank2 = [_count_greater(v2, x) for x in t2]
    c1 = []
    for x in t1:
        c = jnp.zeros_like(x)
        for a in range(PEER_TOPK):
            c = jnp.where(x == v1[a], counts[a], c)
        c1.append(c)
    r2_ref[h, :, cols] = jnp.concatenate(rank2, axis=0).astype(BF)
    e2_ref[h, :, cols] = jnp.concatenate([jnp.exp(x - v2[0]) for x in t2], axis=0).astype(BF)
    c1_ref[h, :, cols] = jnp.concatenate(c1, axis=0)
    e1_ref[h, :, cols] = jnp.concatenate([jnp.exp(x - v1[0]) * inv_z for x in t1], axis=0)

    no_tie_rank_sum = float(sum(range(PEER_TOPK)) + (N_KEYS - PEER_TOPK) * PEER_TOPK)
    bad = jnp.where(_column_sum(rank2) != no_tie_rank_sum, 1.0, 0.0)
    bad = bad + jnp.where(_column_sum([jnp.where(x >= v1[PEER_TOPK - 1], 1.0, 0.0) for x in t1]) != PEER_TOPK, 1.0, 0.0)
    for a in range(PEER_TOPK - 1):
        bad = bad + jnp.where(v1[a] == v1[a + 1], 1.0, 0.0)
    return bad + jnp.where(_column_sum(chosen) != PEER_TOPK, 1.0, 0.0)


def _route_kernel(x_ref, wh_ref, wl_ref, kh_ref, kl_ref, fidx_ref, r2_ref, e2_ref, c1_ref, e1_ref):
    x = x_ref[...]
    t = x.shape[0]
    xh, xl = _split_bf16(x)
    q = _dot(xh, wh_ref[...]) + (_dot(xl, wh_ref[...]) + _dot(xh, wl_ref[...]))
    outs = (r2_ref, e2_ref, c1_ref, e1_ref)

    def scores(hs):
        qh, ql = _split_bf16(q[:, hs * PEER_HALF:(hs + 1) * PEER_HALF])
        return _dot_nt(kh_ref[hs], qh) + (_dot_nt(kl_ref[hs], qh) + _dot_nt(kh_ref[hs], ql))

    all_scores = [scores(hs) for hs in range(2 * PEER_HEADS)]
    bad = []
    for h in range(PEER_HEADS):
        flags = [_route_head_fast(all_scores[2 * h][:, c * LANES:(c + 1) * LANES],
                                  all_scores[2 * h + 1][:, c * LANES:(c + 1) * LANES], outs, h,
                                  slice(c * LANES, (c + 1) * LANES)) for c in range(t // LANES)]
        bad.append(sum(flags[1:], flags[0]))

    for h in range(PEER_HEADS):
        @pl.when(jnp.max(bad[h]) > 0.0)
        def _(h=h):
            _route_head_exact(all_scores[2 * h], all_scores[2 * h + 1], fidx_ref[...], outs, h)


def _route(x1, lw, tb):
    n = x1.shape[0]
    fidx = jnp.broadcast_to(jnp.asarray(_cand_flat_index())[:, None], (CAND_ROWS, tb))
    out = jax.ShapeDtypeStruct((PEER_HEADS, N_KEYS, n), F32)
    out_bf = jax.ShapeDtypeStruct((PEER_HEADS, N_KEYS, n), BF)
    ospec = pl.BlockSpec((PEER_HEADS, N_KEYS, tb), lambda i: (0, 0, i))
    full2 = lambda a: pl.BlockSpec(a.shape, lambda i: (0, 0))
    full3 = lambda a: pl.BlockSpec(a.shape, lambda i: (0, 0, 0))
    return pl.pallas_call(
        _route_kernel,
        out_shape=(out_bf, out_bf, out, out),
        grid=(n // tb,),
        in_specs=[pl.BlockSpec((tb, D_MODEL), lambda i: (i, 0)), full2(lw["wq_hi"]), full2(lw["wq_lo"]),
                  full3(lw["keys_hi"]), full3(lw["keys_lo"]), full2(fidx)],
        out_specs=(ospec, ospec, ospec, ospec),
        compiler_params=_params(("arbitrary",)),
        name="peer_route",
    )(x1, lw["wq_hi"], lw["wq_lo"], lw["keys_hi"], lw["keys_lo"], fidx)


PEER_EXPERT_BLOCK = 1024


def _peer_kernel(x_ref, u_ref, vt_ref, r2_ref, e2_ref, c1_ref, e1_ref, lg_ref, lb_ref, o_ref, xb, acc, wt, rowbuf, act):
    s = pl.program_id(1)
    last = pl.num_programs(1) - 1
    tb = x_ref.shape[0]

    @pl.when(s == 0)
    def _():
        xb[...] = x_ref[...].T.astype(BF)
        acc[...] = jnp.zeros_like(acc)
        act[1] = jnp.zeros(act.shape[1:], BF)

    keys_per_step = PEER_EXPERT_BLOCK // N_KEYS
    first_key = pl.multiple_of(jnp.maximum(s - 1, 0) * keys_per_step, keys_per_step)

    for h in range(PEER_HEADS):
        for src, ref in enumerate((c1_ref, e1_ref)):
            blk = ref[h, pl.ds(first_key, keys_per_step), :]
            for j in range(keys_per_step):
                rowbuf[src, h, j] = jnp.broadcast_to(blk[j:j + 1], (SUBLANES, tb)).astype(BF)

    half = N_KEYS // 2
    group = keys_per_step // 2
    tiles = half // SUBLANES

    def spread(row):
        return jnp.broadcast_to(row[None], (tiles, SUBLANES, BF_CHUNK)).reshape(half, BF_CHUNK)

    for tc in range(tb // BF_CHUNK):
        cols = slice(tc * BF_CHUNK, (tc + 1) * BF_CHUNK)
        for jg in range(keys_per_step // group):
            for kh in range(2):
                keys = slice(kh * half, (kh + 1) * half)
                w = [jnp.zeros((half, BF_CHUNK), BF) for _ in range(group)]
                for h in range(PEER_HEADS):
                    r2 = r2_ref[h, keys, cols]
                    e2 = e2_ref[h, keys, cols]
                    for jj in range(group):
                        j = jg * group + jj
                        c1 = spread(rowbuf[0, h, j, :, cols])
                        e1 = spread(rowbuf[1, h, j, :, cols])
                        w[jj] = w[jj] + jnp.where(r2 < c1, e2 * e1, 0)
                for jj in range(group):
                    rows = slice((jg * group + jj) * N_KEYS + kh * half, (jg * group + jj) * N_KEYS + (kh + 1) * half)
                    wt[rows, cols] = w[jj]
    wt[...] = wt[...] * act[(s + 1) % 2]
    acc[...] += _dot(vt_ref[...], wt[...])
    act[s % 2] = jax.nn.gelu(_dot(u_ref[...], xb[...]).astype(BF))

    @pl.when(s == last)
    def _():
        r = DN_ALPHA * x_ref[...] + acc[...].T
        o_ref[...] = _layer_norm(r, lg_ref[...], lb_ref[...])


def _peer(x1, route, lw, tb):
    n = x1.shape[0]
    eb = PEER_EXPERT_BLOCK
    nblk = N_EXPERTS // eb
    rspec = pl.BlockSpec((PEER_HEADS, N_KEYS, tb), lambda i, e: (0, 0, i))
    vec = pl.BlockSpec((1, D_MODEL), lambda i, e: (0, 0))
    return pl.pallas_call(
        _peer_kernel,
        out_shape=jax.ShapeDtypeStruct((n, D_MODEL), F32),
        grid=(n // tb, nblk + 1),
        in_specs=[pl.BlockSpec((tb, D_MODEL), lambda i, s: (i, 0)),
                  pl.BlockSpec((eb, D_MODEL), lambda i, s: (jnp.minimum(s, nblk - 1), 0)),
                  pl.BlockSpec((D_MODEL, eb), lambda i, s: (0, jnp.maximum(s - 1, 0))),
                  rspec, rspec, rspec, rspec, vec, vec],
        out_specs=pl.BlockSpec((tb, D_MODEL), lambda i, s: (i, 0)),
        scratch_shapes=[pltpu.VMEM((D_MODEL, tb), BF), pltpu.VMEM((D_MODEL, tb), F32), pltpu.VMEM((eb, tb), BF),
                        pltpu.VMEM((2, PEER_HEADS, eb // N_KEYS, SUBLANES, tb), BF),
                        pltpu.VMEM((2, eb, tb), BF)],
        compiler_params=_params(("arbitrary", "arbitrary")),
        name="peer_experts",
    )(x1, lw["peer_u"], lw["peer_vt"], *route, lw["ln2_g"], lw["ln2_b"])


def _ple_kernel(x_ref, pe_ref, wg_ref, wp_ref, lg_ref, lb_ref, o_ref):
    x = x_ref[...]
    ple = jax.nn.sigmoid(_dot(x.astype(BF), wg_ref[...])) * _dot(pe_ref[...].astype(BF), wp_ref[...])
    o_ref[...] = _layer_norm(DN_ALPHA * x + ple, lg_ref[...], lb_ref[...])


def _ple(x2, pe, lw, tm):
    n = x2.shape[0]
    full = lambda a: pl.BlockSpec(a.shape, lambda i: (0, 0))
    ws = (lw["ple_w_g"], lw["ple_w_p"], lw["ln3_g"], lw["ln3_b"])
    return pl.pallas_call(
        _ple_kernel,
        out_shape=jax.ShapeDtypeStruct((n, D_MODEL), F32),
        grid=(n // tm,),
        in_specs=[pl.BlockSpec((tm, D_MODEL), lambda i: (i, 0)), pl.BlockSpec((tm, PLE_DIM), lambda i: (i, 0))]
        + [full(a) for a in ws],
        out_specs=pl.BlockSpec((tm, D_MODEL), lambda i: (i, 0)),
        compiler_params=_params(("arbitrary",)),
        name="ple",
    )(x2, pe, *ws)


def _time_major(u, bsz, t, bp):
    u = jnp.transpose(u.reshape(bsz, t, -1), (1, 0, 2))
    return jnp.pad(u, ((0, 0), (0, bp - bsz), (0, 0))).reshape(t * bp, -1)


def _batch_major(y, bsz, t, bp):
    return jnp.transpose(y.reshape(t, bp, -1)[:, :bsz], (1, 0, 2)).reshape(bsz * t, -1)


def _layer(x, pe, h0_re, h0_im, cache_k, cache_v, s0, pos, lw, bsz, t, prompt, n_rows):
    n = bsz * t
    h = _in_projection(x, lw["w_in"], 512, IN_COLS // 2)

    bp = max(8, bsz)
    u_tb = _time_major(h[:, OFF_U_SSM:OFF_U_SSM + SSM_WIDTH], bsz, t, bp)
    h0 = jnp.concatenate([h0_re.reshape(bsz, SSM_FLAT), h0_im.reshape(bsz, SSM_FLAT)], axis=1).astype(F32)
    h0 = jnp.pad(h0, ((0, bp - bsz), (0, 0)))
    y_tb, h_last = _s5_mixer(u_tb, h0, lw["s5"], t, bp, CHUNK)
    y_ssm = _batch_major(y_tb, bsz, t, bp)
    new_re = h_last[:bsz, :SSM_FLAT].reshape(bsz, SSM_GROUPS, SSM_STATE)
    new_im = h_last[:bsz, SSM_FLAT:].reshape(bsz, SSM_GROUPS, SSM_STATE)

    k_att = h[:, OFF_K_ATT:OFF_K_ATT + ATT_WIDTH].reshape(bsz, t, ATT_HEADS, ATT_HEAD_DIM)
    v_att = h[:, OFF_V_ATT:OFF_V_ATT + ATT_WIDTH].reshape(bsz, t, ATT_HEADS, ATT_HEAD_DIM)
    if prompt:
        y_att = _attn_prompt(h, lw["rel_bias"], bsz, t)
        new_k, new_v = k_att[:, -n_rows:], v_att[:, -n_rows:]
    else:
        r = cache_k.shape[1]
        y_att = _attn_sample(h, cache_k.reshape(bsz, r, ATT_WIDTH), cache_v.reshape(bsz, r, ATT_WIDTH),
                             lw["rel_bias"], bsz)
        new_k, new_v = k_att, v_att

    y_ret, s_pair = _retention(h, _pair_states(s0), _retention_tables(pos), lw["ret_gn_g"], bsz, t,
                               4 if prompt else 1)
    s_fin = _unpair_states(s_pair)

    tok = 512 if n % 512 == 0 else 256
    x1 = _merge(x, h, y_ssm, y_att, y_ret, lw, tok)
    route = _route(x1, lw, 256)
    x2 = _peer(x1, route, lw, tok)
    x3 = _ple(x2, pe, lw, tok)
    return x3, (new_re, new_im, new_k, new_v, s_fin)


def _layer_weights(li, w_in, ssm_lam_re, ssm_lam_im, ssm_b_re, ssm_b_im, ssm_c_re, ssm_c_im, ssm_log_dt, ssm_d,
                   ssm_w_glu, ssm_b_glu, att_rel_bias, ret_gn_g, w_br_ssm, w_br_att, w_br_ret, w_o, ln1_g, ln1_b,
                   peer_w_q, peer_sub_keys, peer_u, peer_v, ln2_g, ln2_b, ple_w_g, ple_w_p, ln3_g, ln3_b):
    vec = lambda a: a[li].astype(F32).reshape(1, -1)
    w = w_in[li]
    wq_hi, wq_lo = _split_bf16(peer_w_q[li].astype(F32))
    keys_hi, keys_lo = _split_bf16(peer_sub_keys[li].astype(F32).reshape(2 * PEER_HEADS, N_KEYS, PEER_HALF))
    return {
        "w_in": jnp.concatenate([w[:, OLD_GATE_START:], w[:, :OLD_GATE_START]], axis=1).astype(BF),
        "s5": _s5_tables(ssm_lam_re[li], ssm_lam_im[li], ssm_b_re[li], ssm_b_im[li], ssm_c_re[li], ssm_c_im[li],
                         ssm_log_dt[li], ssm_d[li], ssm_w_glu[li], ssm_b_glu[li]),
        "rel_bias": _rel_bias_tile(att_rel_bias[li]),
        "ret_gn_g": vec(ret_gn_g),
        "w_br_ssm": w_br_ssm[li].astype(BF), "w_br_att": w_br_att[li].astype(BF), "w_br_ret": w_br_ret[li].astype(BF),
        "w_o": w_o[li].astype(BF), "ln1_g": vec(ln1_g), "ln1_b": vec(ln1_b),
        "wq_hi": wq_hi, "wq_lo": wq_lo, "keys_hi": keys_hi, "keys_lo": keys_lo,
        "peer_u": peer_u[li].astype(BF), "peer_vt": peer_v[li].astype(BF).T,
        "ln2_g": vec(ln2_g), "ln2_b": vec(ln2_b),
        "ple_w_g": ple_w_g[li].astype(BF), "ple_w_p": ple_w_p[li].astype(BF), "ln3_g": vec(ln3_g), "ln3_b": vec(ln3_b),
    }


def kernel(x_prompt, x_sample, p_prompt, p_sample, state_ssm_re, state_ssm_im, cache_attn_k, cache_attn_v, state_ret, w_in, ssm_lam_re, ssm_lam_im, ssm_b_re, ssm_b_im, ssm_c_re, ssm_c_im, ssm_log_dt, ssm_d, ssm_w_glu, ssm_b_glu, att_rel_bias, ret_gn_g, w_br_ssm, w_br_att, w_br_ret, w_o, ln1_g, ln1_b, peer_w_q, peer_sub_keys, peer_u, peer_v, ln2_g, ln2_b, ple_w_g, ple_w_p, ln3_g, ln3_b):
    bp, tp, _ = x_prompt.shape
    bs, ts, _ = x_sample.shape
    n_rows = cache_attn_k.shape[2]
    pos_p = jnp.arange(tp)
    pos_s = PAST_LEN + jnp.arange(ts)
    y_p = x_prompt.reshape(bp * tp, D_MODEL).astype(F32)
    y_s = x_sample.reshape(bs * ts, D_MODEL).astype(F32)
    new_p, new_s = [], []
    for li in range(DEPTH):
        lw = _layer_weights(li, w_in, ssm_lam_re, ssm_lam_im, ssm_b_re, ssm_b_im, ssm_c_re, ssm_c_im, ssm_log_dt,
                            ssm_d, ssm_w_glu, ssm_b_glu, att_rel_bias, ret_gn_g, w_br_ssm, w_br_att, w_br_ret, w_o,
                            ln1_g, ln1_b, peer_w_q, peer_sub_keys, peer_u, peer_v, ln2_g, ln2_b, ple_w_g, ple_w_p,
                            ln3_g, ln3_b)
        zero_ssm = jnp.zeros((bp, SSM_GROUPS, SSM_STATE), F32)
        zero_ret = jnp.zeros((bp, RET_HEADS, RET_DIM, RET_DIM), F32)
        y_p, st = _layer(y_p, p_prompt[li].reshape(bp * tp, PLE_DIM), zero_ssm, zero_ssm, None, None, zero_ret,
                         pos_p, lw, bp, tp, True, n_rows)
        new_p.append(st)
        y_s, st = _layer(y_s, p_sample[li].reshape(bs * ts, PLE_DIM), state_ssm_re[li], state_ssm_im[li],
                         cache_attn_k[li], cache_attn_v[li], state_ret[li], pos_s, lw, bs, ts, False, n_rows)
        new_s.append(st)
    stack = lambda states, k: jnp.stack([s[k] for s in states])
    return (y_p.reshape(bp, tp, D_MODEL), y_s.reshape(bs, ts, D_MODEL),
            stack(new_p, 0), stack(new_p, 1), stack(new_p, 2), stack(new_p, 3), stack(new_p, 4),
            stack(new_s, 0), stack(new_s, 1), stack(new_s, 2), stack(new_s, 3), stack(new_s, 4))
```

```python
import functools

import jax
import jax.numpy as jnp
import numpy as np
from jax import lax
from jax.experimental import pallas as pl
from jax.experimental.pallas import tpu as pltpu

F32 = jnp.float32
BF = jnp.bfloat16

D_MODEL = 1024
DEPTH = 2
PAST_LEN = 2048
CHUNK = 64
PLE_DIM = 256
SSM_WIDTH = 256
SSM_GROUP = 16
SSM_GROUPS = 16
SSM_STATE = 64
SSM_FLAT = SSM_GROUPS * SSM_STATE
ATT_HEADS = 8
ATT_HEAD_DIM = 64
ATT_WIDTH = 512
ATT_PAST = 8 * CHUNK
ATT_BAND = ATT_PAST + CHUNK
REL_CLIP = 256
RET_HEADS = 4
RET_DIM = 64
ROPE_BASE = 10000.0
PEER_HEADS = 8
PEER_HALF = 128
N_KEYS = 128
N_EXPERTS = N_KEYS * N_KEYS
PEER_TOPK = 16
SUBLANES = 8
BF_CHUNK = 256
DN_ALPHA = (2 * DEPTH) ** 0.25
LN_EPS = 1e-5
LANES = 128
NEG_MASK = float(jnp.finfo(jnp.float32).min)

IN_COLS = 5888
OFF_GATES = 0
OFF_U_SSM = 3072
OFF_Q_ATT = 3328
OFF_K_ATT = 3840
OFF_V_ATT = 4352
OFF_Q_RET = 4864
OFF_K_RET = 5120
OFF_V_RET = 5376
OFF_G_RET = 5632
OLD_GATE_START = 2816

VMEM_LIMIT = 48 * 1024 * 1024


def _params(sem):
    return pltpu.CompilerParams(dimension_semantics=sem, vmem_limit_bytes=VMEM_LIMIT)


def _dot(a, b):
    return jnp.dot(a, b, preferred_element_type=F32)


def _dot_nt(a, b):
    return lax.dot_general(a, b, (((1,), (1,)), ((), ())), preferred_element_type=F32)


def _dot_tn(a, b):
    return lax.dot_general(a, b, (((0,), (0,)), ((), ())), preferred_element_type=F32)


def _layer_norm(r, g, b):
    mu = jnp.mean(r, axis=-1, keepdims=True)
    d = r - mu
    var = jnp.mean(d * d, axis=-1, keepdims=True)
    return d * lax.rsqrt(var + LN_EPS) * g + b


def _matmul_kernel(x_ref, w_ref, o_ref):
    o_ref[...] = _dot(x_ref[...].astype(BF), w_ref[...])


def _in_projection(x, w, tm, tn):
    m, k = x.shape
    n = w.shape[1]
    return pl.pallas_call(
        _matmul_kernel,
        out_shape=jax.ShapeDtypeStruct((m, n), F32),
        grid=(n // tn, m // tm),
        in_specs=[pl.BlockSpec((tm, k), lambda j, i: (i, 0)),
                  pl.BlockSpec((k, tn), lambda j, i: (0, j))],
        out_specs=pl.BlockSpec((tm, tn), lambda j, i: (i, j)),
        compiler_params=_params(("arbitrary", "arbitrary")),
        name="in_projection",
    )(x, w)


def _s5_kernel(u_ref, bm_ref, a_ref, cm_ref, d_ref, wg_ref, bg_ref, h0_ref, y_ref, hl_ref, hbuf, hst, *, steps, bp):
    @pl.when(pl.program_id(0) == 0)
    def _():
        hst[...] = h0_ref[...]

    u = u_ref[...]
    hbuf[...] = _dot(u.astype(BF), bm_ref[...])
    a_re = jnp.broadcast_to(a_ref[0:1, :], (bp, SSM_FLAT))
    a_im = jnp.broadcast_to(a_ref[1:2, :], (bp, SSM_FLAT))

    def step(t, carry):
        h_re, h_im = carry
        r0 = pl.multiple_of(t * bp, bp)
        n_re = a_re * h_re - a_im * h_im + hbuf[pl.ds(r0, bp), 0:SSM_FLAT]
        n_im = a_re * h_im + a_im * h_re + hbuf[pl.ds(r0, bp), SSM_FLAT:2 * SSM_FLAT]
        hbuf[pl.ds(r0, bp), 0:SSM_FLAT] = n_re
        hbuf[pl.ds(r0, bp), SSM_FLAT:2 * SSM_FLAT] = n_im
        return n_re, n_im

    h_re, h_im = lax.fori_loop(0, steps, step, (hst[:, 0:SSM_FLAT], hst[:, SSM_FLAT:2 * SSM_FLAT]))
    hst[:, 0:SSM_FLAT] = h_re
    hst[:, SSM_FLAT:2 * SSM_FLAT] = h_im
    hl_ref[...] = hst[...]
    y = _dot(hbuf[...].astype(BF), cm_ref[...]) + d_ref[...] * u
    z = jax.nn.gelu(y)
    y_ref[...] = z * jax.nn.sigmoid(_dot(z.astype(BF), wg_ref[...]) + bg_ref[...])


def _s5_mixer(u_tb, h0, sp, n_steps, bp, steps):
    rows = steps * bp
    kern = functools.partial(_s5_kernel, steps=steps, bp=bp)
    const = lambda c: (0, 0)
    return pl.pallas_call(
        kern,
        out_shape=(jax.ShapeDtypeStruct((n_steps * bp, SSM_WIDTH), F32),
                   jax.ShapeDtypeStruct((bp, 2 * SSM_FLAT), F32)),
        grid=(n_steps // steps,),
        in_specs=[pl.BlockSpec((rows, SSM_WIDTH), lambda c: (c, 0)),
                  pl.BlockSpec((SSM_WIDTH, 2 * SSM_FLAT), const),
                  pl.BlockSpec((2, SSM_FLAT), const),
                  pl.BlockSpec((2 * SSM_FLAT, SSM_WIDTH), const),
                  pl.BlockSpec((1, SSM_WIDTH), const),
                  pl.BlockSpec((SSM_WIDTH, SSM_WIDTH), const),
                  pl.BlockSpec((1, SSM_WIDTH), const),
                  pl.BlockSpec((bp, 2 * SSM_FLAT), const)],
        out_specs=(pl.BlockSpec((rows, SSM_WIDTH), lambda c: (c, 0)),
                   pl.BlockSpec((bp, 2 * SSM_FLAT), const)),
        scratch_shapes=[pltpu.VMEM((rows, 2 * SSM_FLAT), F32), pltpu.VMEM((bp, 2 * SSM_FLAT), F32)],
        compiler_params=_params(("arbitrary",)),
        name="s5_mixer",
    )(u_tb, sp["bmat"], sp["a"], sp["cmat"], sp["d"], sp["w_glu"], sp["b_glu"], h0)


def _s5_tables(lam_re, lam_im, b_re, b_im, c_re, c_im, log_dt, d_skip, w_glu, b_glu):
    lr, li = lam_re.astype(F32), lam_im.astype(F32)
    dt = jnp.exp(log_dt.astype(F32))[:, None]
    mag = jnp.exp(lr * dt)
    a_re, a_im = mag * jnp.cos(li * dt), mag * jnp.sin(li * dt)
    den = lr * lr + li * li
    k_re = ((a_re - 1.0) * lr + a_im * li) / den
    k_im = (a_im * lr - (a_re - 1.0) * li) / den
    br, bi = b_re.astype(F32), b_im.astype(F32)
    bbar_re = k_re[..., None] * br - k_im[..., None] * bi
    bbar_im = k_re[..., None] * bi + k_im[..., None] * br
    eye = jnp.eye(SSM_GROUPS, dtype=F32)

    def in_blockdiag(m):
        m = jnp.transpose(m, (0, 2, 1))
        return (m[:, :, None, :] * eye[:, None, :, None]).reshape(SSM_WIDTH, SSM_FLAT)

    def out_blockdiag(m):
        m = jnp.transpose(m, (0, 2, 1))
        return (m[:, :, None, :] * eye[:, None, :, None]).reshape(SSM_FLAT, SSM_WIDTH)

    bmat = jnp.concatenate([in_blockdiag(bbar_re), in_blockdiag(bbar_im)], axis=1)
    cmat = jnp.concatenate([out_blockdiag(c_re.astype(F32)), -out_blockdiag(c_im.astype(F32))], axis=0)
    a = jnp.stack([a_re.reshape(SSM_FLAT), a_im.reshape(SSM_FLAT)])
    return {"bmat": bmat.astype(BF), "cmat": cmat.astype(BF), "a": a,
            "d": d_skip.astype(F32).reshape(1, SSM_WIDTH), "w_glu": w_glu.astype(BF),
            "b_glu": b_glu.astype(F32).reshape(1, SSM_WIDTH)}


ATT_QROWS = 4 * CHUNK
ATT_WINDOW = ATT_PAST + ATT_QROWS


def _attn_prompt_kernel(q_ref, k0_ref, k1_ref, k2_ref, v0_ref, v1_ref, v2_ref, bias_ref, o_ref):
    tb = pl.program_id(2)
    kwin = jnp.concatenate([k0_ref[...], k1_ref[...], k2_ref[...]], axis=0).astype(BF)
    vwin = jnp.concatenate([v0_ref[...], v1_ref[...], v2_ref[...]], axis=0).astype(BF)
    first = lax.broadcasted_iota(jnp.int32, (1, LANES), 1) < ATT_HEAD_DIM
    q = q_ref[...] * (ATT_HEAD_DIM ** -0.5)
    q2 = jnp.concatenate([jnp.where(first, q, 0.0), jnp.where(first, 0.0, q)], axis=0).astype(BF)
    s = _dot_nt(q2, kwin) + jnp.concatenate([bias_ref[0], bias_ref[1]], axis=0)
    col = lax.broadcasted_iota(jnp.int32, (1, ATT_WINDOW), 1)
    s = jnp.where(col + (tb * ATT_QROWS - ATT_PAST) >= 0, s, NEG_MASK)
    p = jnp.exp(s - jnp.max(s, axis=-1, keepdims=True))
    o = _dot(p.astype(BF), vwin) / jnp.sum(p, axis=-1, keepdims=True)
    o_ref[...] = jnp.where(first, o[:ATT_QROWS], o[ATT_QROWS:])


def _attn_prompt(h, bias, bsz, t):
    rows = ATT_QROWS
    tbn = t // rows
    qc, kc, vc = OFF_Q_ATT // LANES, OFF_K_ATT // LANES, OFF_V_ATT // LANES
    band = jnp.concatenate(
        [jnp.pad(bias, ((0, 0), (0, 0), (c * CHUNK, ATT_WINDOW - ATT_BAND - c * CHUNK)), constant_values=NEG_MASK)
         for c in range(rows // CHUNK)], axis=1)

    def past(col0, back):
        return pl.BlockSpec((rows, LANES), lambda b, hp, tb: (b * tbn + jnp.maximum(tb - back, 0), col0 + hp))

    return pl.pallas_call(
        _attn_prompt_kernel,
        out_shape=jax.ShapeDtypeStruct((bsz * t, ATT_WIDTH), F32),
        grid=(bsz, ATT_HEADS // 2, tbn),
        in_specs=[past(qc, 0), past(kc, 2), past(kc, 1), past(kc, 0), past(vc, 2), past(vc, 1), past(vc, 0),
                  pl.BlockSpec((2, rows, ATT_WINDOW), lambda b, hp, tb: (hp, 0, 0))],
        out_specs=pl.BlockSpec((rows, LANES), lambda b, hp, tb: (b * tbn + tb, hp)),
        compiler_params=_params(("arbitrary", "arbitrary", "arbitrary")),
        name="attn_prompt",
    )(h, h, h, h, h, h, h, band)


def _attn_sample_kernel(q_ref, k_ref, v_ref, ck_ref, cv_ref, bias_ref, o_ref):
    kwin = jnp.concatenate([ck_ref[0], k_ref[...]], axis=0).astype(BF)
    vwin = jnp.concatenate([cv_ref[0], v_ref[...]], axis=0).astype(BF)
    first = lax.broadcasted_iota(jnp.int32, (1, LANES), 1) < ATT_HEAD_DIM
    q = q_ref[...] * (ATT_HEAD_DIM ** -0.5)
    q2 = jnp.concatenate([jnp.where(first, q, 0.0), jnp.where(first, 0.0, q)], axis=0).astype(BF)
    s = _dot_nt(q2, kwin) + jnp.concatenate([bias_ref[0], bias_ref[1]], axis=0)
    p = jnp.exp(s - jnp.max(s, axis=-1, keepdims=True))
    o = _dot(p.astype(BF), vwin) / jnp.sum(p, axis=-1, keepdims=True)
    o_ref[...] = jnp.where(first, o[:CHUNK], o[CHUNK:])


def _attn_sample(h, cache_k, cache_v, bias, bsz):
    qc, kc, vc = OFF_Q_ATT // LANES, OFF_K_ATT // LANES, OFF_V_ATT // LANES
    new = lambda col0: pl.BlockSpec((CHUNK, LANES), lambda b, hp: (b, col0 + hp))
    old = pl.BlockSpec((1, ATT_PAST, LANES), lambda b, hp: (b, 0, hp))
    return pl.pallas_call(
        _attn_sample_kernel,
        out_shape=jax.ShapeDtypeStruct((bsz * CHUNK, ATT_WIDTH), F32),
        grid=(bsz, ATT_HEADS // 2),
        in_specs=[new(qc), new(kc), new(vc), old, old,
                  pl.BlockSpec((2, CHUNK, ATT_BAND), lambda b, hp: (hp, 0, 0))],
        out_specs=pl.BlockSpec((CHUNK, LANES), lambda b, hp: (b, hp)),
        compiler_params=_params(("arbitrary", "arbitrary")),
        name="attn_sample",
    )(h, h, h, cache_k, cache_v, bias)


def _rel_bias_tile(table):
    dist = jnp.arange(-(CHUNK - 1), ATT_BAND)
    per_dist = table[jnp.clip(dist, -REL_CLIP, REL_CLIP) + REL_CLIP].astype(F32).T
    n = dist.shape[0]
    rev = jnp.roll(per_dist[:, ::-1], -(CHUNK - 1), axis=1)
    skew = jnp.tile(rev, (1, CHUNK))[:, :CHUNK * (n - 1)].reshape(ATT_HEADS, CHUNK, n - 1)
    return skew[:, :, :ATT_BAND]


def _ret_kernel(q_ref, k_ref, v_ref, g_ref, cos_ref, sin_ref, intra_ref, qw_ref, kw_ref, dec_ref, bd_ref, gn_ref,
                s0_ref, y_ref, sf_ref, state, *, cpb):
    @pl.when(pl.program_id(1) == 0)
    def _():
        state[...] = s0_ref[0]

    lane = lax.broadcasted_iota(jnp.int32, (1, LANES), 1)
    first = lane < RET_DIM
    low_half = (lane % RET_DIM) < (RET_DIM // 2)

    for c in range(cpb):
        rows = slice(c * CHUNK, (c + 1) * CHUNK)
        cos = cos_ref[rows, :]
        sin = sin_ref[rows, :]

        def rope(x):
            swapped = jnp.where(low_half, pltpu.roll(x, LANES - RET_DIM // 2, 1), pltpu.roll(x, RET_DIM // 2, 1))
            return x * cos + swapped * sin

        for hp in range(RET_HEADS // 2):
            cols = slice(hp * LANES, (hp + 1) * LANES)
            q = rope(q_ref[rows, cols]) * (RET_DIM ** -0.5)
            k = rope(k_ref[rows, cols])
            kb = k.astype(BF)
            vb = v_ref[rows, cols].astype(BF)
            intra = []
            for hh in range(2):
                qm = jnp.where(first if hh == 0 else jnp.logical_not(first), q, 0.0).astype(BF)
                s = _dot_nt(qm, kb) * intra_ref[2 * hp + hh]
                intra.append(_dot(s.astype(BF), vb))
            s_prev = state[hp]
            o = jnp.where(first, intra[0], intra[1]) + _dot((q * qw_ref[hp]).astype(BF), s_prev.astype(BF))
            kv = _dot_tn((k * kw_ref[hp]).astype(BF), vb)
            state[hp] = dec_ref[hp] * s_prev + bd_ref[...] * kv
            sum_a = jnp.sum(jnp.where(first, o, 0.0), axis=-1, keepdims=True)
            sum_b = jnp.sum(o, axis=-1, keepdims=True) - sum_a
            d = o - jnp.where(first, sum_a, sum_b) * (1.0 / RET_DIM)
            sq = d * d
            var_a = jnp.sum(jnp.where(first, sq, 0.0), axis=-1, keepdims=True)
            var_b = jnp.sum(sq, axis=-1, keepdims=True) - var_a
            var = jnp.where(first, var_a, var_b) * (1.0 / RET_DIM)
            g = g_ref[rows, cols]
            y_ref[rows, cols] = d * lax.rsqrt(var + LN_EPS) * gn_ref[:, cols] * (g * jax.nn.sigmoid(g))
    sf_ref[0] = state[...]


def _retention(h, s0, rt, gn, bsz, t, cpb):
    rows = cpb * CHUNK
    tbn = t // rows
    width = RET_HEADS * RET_DIM
    col = lambda off: pl.BlockSpec((rows, width), lambda b, tb: (b * tbn + tb, off // width))
    tab = pl.BlockSpec((rows, LANES), lambda b, tb: (tb, 0))
    const3 = lambda shape: pl.BlockSpec(shape, lambda b, tb: (0, 0, 0))
    st = pl.BlockSpec((1, 2, LANES, LANES), lambda b, tb: (b, 0, 0, 0))
    return pl.pallas_call(
        functools.partial(_ret_kernel, cpb=cpb),
        out_shape=(jax.ShapeDtypeStruct((bsz * t, width), F32),
                   jax.ShapeDtypeStruct((bsz, 2, LANES, LANES), F32)),
        grid=(bsz, tbn),
        in_specs=[col(OFF_Q_RET), col(OFF_K_RET), col(OFF_V_RET), col(OFF_G_RET), tab, tab,
                  const3((RET_HEADS, CHUNK, CHUNK)), const3((2, CHUNK, LANES)), const3((2, CHUNK, LANES)),
                  const3((2, LANES, LANES)), pl.BlockSpec((LANES, LANES), lambda b, tb: (0, 0)),
                  pl.BlockSpec((1, width), lambda b, tb: (0, 0)), st],
        out_specs=(pl.BlockSpec((rows, width), lambda b, tb: (b * tbn + tb, 0)), st),
        scratch_shapes=[pltpu.VMEM((2, LANES, LANES), F32)],
        compiler_params=_params(("arbitrary", "arbitrary")),
        name="retention",
    )(h, h, h, h, rt["cos"], rt["sin"], rt["intra"], rt["qw"], rt["kw"], rt["dec"], rt["bd"], gn, s0)


def _retention_tables(pos):
    f32 = np.float32
    lg = np.log1p(-(f32(2.0) ** (f32(-5.0) - np.arange(RET_HEADS, dtype=f32)))).astype(f32)
    i = np.arange(CHUNK, dtype=f32)
    intra = np.exp(lg[:, None, None] * np.abs(i[:, None] - i[None, :])).astype(f32)
    lane_head = np.arange(2 * LANES) // RET_DIM
    lg_lane = lg[lane_head]
    qw = np.exp(lg_lane[None, :] * (i[:, None] + f32(1.0))).astype(f32)
    kw = np.exp(lg_lane[None, :] * (f32(CHUNK - 1.0) - i)[:, None]).astype(f32)
    pair = lambda m: np.stack([m[:, :LANES], m[:, LANES:]])
    same_head = (np.arange(LANES)[:, None] // RET_DIM) == (np.arange(LANES)[None, :] // RET_DIM)
    bd = same_head.astype(f32)
    dec_lane = np.exp(lg_lane * f32(CHUNK)).astype(f32)
    dec = np.stack([bd * dec_lane[:LANES][:, None], bd * dec_lane[LANES:][:, None]])
    half = RET_DIM // 2
    freqs = (f32(ROPE_BASE) ** (-np.arange(half, dtype=f32) / f32(half))).astype(f32)
    ang = (np.asarray(pos).astype(f32)[:, None] * freqs[None]).astype(f32)
    cos = np.tile(np.cos(ang).astype(f32), (1, LANES // half))
    sin = np.tile(np.concatenate([-np.sin(ang), np.sin(ang)], axis=1).astype(f32), (1, LANES // RET_DIM))
    return {"intra": intra, "qw": pair(qw), "kw": pair(kw), "dec": dec, "bd": bd, "cos": cos, "sin": sin}


def _pair_states(s):
    b = s.shape[0]
    s = s.astype(F32).reshape(b, 2, 2, RET_DIM, RET_DIM)
    out = jnp.zeros((b, 2, 2, RET_DIM, 2, RET_DIM), F32)
    out = out.at[:, :, 0, :, 0, :].set(s[:, :, 0]).at[:, :, 1, :, 1, :].set(s[:, :, 1])
    return out.reshape(b, 2, LANES, LANES)


def _unpair_states(s):
    b = s.shape[0]
    s = s.reshape(b, 2, 2, RET_DIM, 2, RET_DIM)
    return jnp.stack([s[:, :, 0, :, 0, :], s[:, :, 1, :, 1, :]], axis=2).reshape(b, RET_HEADS, RET_DIM, RET_DIM)


def _merge_kernel(x_ref, g1_ref, g2_ref, g3_ref, ys_ref, ya_ref, yr_ref, ws_ref, wa_ref, wr_ref, wo_ref,
                  lg_ref, lb_ref, o_ref):
    merged = (jax.nn.sigmoid(g1_ref[...]) * _dot(ys_ref[...].astype(BF), ws_ref[...])
              + jax.nn.sigmoid(g2_ref[...]) * _dot(ya_ref[...].astype(BF), wa_ref[...])
              + jax.nn.sigmoid(g3_ref[...]) * _dot(yr_ref[...].astype(BF), wr_ref[...]))
    r = DN_ALPHA * x_ref[...] + _dot(merged.astype(BF), wo_ref[...])
    o_ref[...] = _layer_norm(r, lg_ref[...], lb_ref[...])


def _merge(x, h, y_ssm, y_att, y_ret, lw, tm):
    n = x.shape[0]
    row = lambda w: pl.BlockSpec((tm, w), lambda i: (i, 0))
    gate = lambda j: pl.BlockSpec((tm, D_MODEL), lambda i: (i, j))
    full = lambda a: pl.BlockSpec(a.shape, lambda i: (0, 0))
    ws = (lw["w_br_ssm"], lw["w_br_att"], lw["w_br_ret"], lw["w_o"], lw["ln1_g"], lw["ln1_b"])
    return pl.pallas_call(
        _merge_kernel,
        out_shape=jax.ShapeDtypeStruct((n, D_MODEL), F32),
        grid=(n // tm,),
        in_specs=[row(D_MODEL), gate(0), gate(1), gate(2), row(SSM_WIDTH), row(ATT_WIDTH),
                  row(RET_HEADS * RET_DIM)] + [full(a) for a in ws],
        out_specs=row(D_MODEL),
        compiler_params=_params(("arbitrary",)),
        name="merge",
    )(x, h, h, h, y_ssm, y_att, y_ret, *ws)


CAND_ROWS = 80


def _cand_flat_index():
    f = [b for b in range(16)]
    for a in range(1, 8):
        f += [a * 16 + b for b in range(8)]
    f += [a * 16 for a in range(8, 16)]
    return np.asarray(f, np.float32)


def _split_bf16(x):
    hi = x.astype(BF)
    return hi, (x - hi.astype(F32)).astype(BF)


def _take16(work, order):
    t = work.shape[1]
    row16 = lax.broadcasted_iota(jnp.int32, (PEER_TOPK, t), 0)
    rank = jnp.full(work.shape, float(PEER_TOPK), F32)
    vals = jnp.zeros((PEER_TOPK, t), F32)
    for r in range(PEER_TOPK):
        m = jnp.max(work, axis=0, keepdims=True)
        first = jnp.min(jnp.where(work == m, order, 1e9), axis=0, keepdims=True)
        sel = order == first
        rank = jnp.where(sel, float(r), rank)
        work = jnp.where(sel, -jnp.inf, work)
        vals = jnp.where(row16 == r, m, vals)
    return rank, vals


def _route_head_exact(s1, s2, fidx, outs, h):
    r2_ref, e2_ref, c1_ref, e1_ref = outs
    key_idx = lax.broadcasted_iota(jnp.int32, s1.shape, 0).astype(F32)
    rank1, v1 = _take16(s1, key_idx)
    rank2, v2 = _take16(s2, key_idx)
    blocks = [v1[0:1] + v2]
    for a in range(1, 8):
        blocks.append(v1[a:a + 1] + v2[0:8])
    blocks.append(v1[8:16] + v2[0:1])
    cand = jnp.concatenate(blocks, axis=0)
    rank_c, _ = _take16(cand, fidx)
    chosen = jnp.where(rank_c < float(PEER_TOPK), 1.0, 0.0)
    counts = [jnp.sum(chosen[0:16], axis=0, keepdims=True)]
    for a in range(1, 8):
        counts.append(jnp.sum(chosen[8 + 8 * a:16 + 8 * a], axis=0, keepdims=True))
    for a in range(8, 16):
        counts.append(chosen[64 + a:65 + a])
    c1 = jnp.zeros(s1.shape, F32)
    for a in range(PEER_TOPK):
        c1 = jnp.where(rank1 == float(a), counts[a], c1)
    z = jnp.sum(chosen * jnp.exp(cand - cand[0:1]), axis=0, keepdims=True)
    r2_ref[h] = rank2.astype(BF)
    e2_ref[h] = jnp.exp(s2 - v2[0:1]).astype(BF)
    c1_ref[h] = c1
    e1_ref[h] = jnp.exp(s1 - v1[0:1]) / z


def _batcher_pairs(n):
    pairs = []

    def merge(lo, m, r):
        step = r * 2
        if step < m:
            merge(lo, m, step)
            merge(lo + r, m, step)
            pairs.extend((i, i + r) for i in range(lo + r, lo + m - r, step))
        else:
            pairs.append((lo, lo + r))

    def sort(lo, m):
        if m > 1:
            sort(lo, m // 2)
            sort(lo + m // 2, m // 2)
            merge(lo, m, 1)

    sort(0, n)
    return pairs


SORT16_PAIRS = _batcher_pairs(PEER_TOPK)


def _exchange(x, i, j):
    a, b = x[i], x[j]
    if b is None:
        return
    if a is None:
        x[i], x[j] = b, None
    else:
        x[i], x[j] = jnp.maximum(a, b), jnp.minimum(a, b)


def _top16_values(tiles):
    x = list(tiles) + [None] * (PEER_TOPK - len(tiles))
    for i, j in SORT16_PAIRS:
        _exchange(x, i, j)
    for shift in (4, 2, 1):
        y = [None if v is None else pltpu.roll(v, shift, 0) for v in x]
        z = []
        for i in range(PEER_TOPK):
            a, b = x[i], y[PEER_TOPK - 1 - i]
            z.append(b if a is None else a if b is None else jnp.maximum(a, b))
        for stride in (8, 4, 2, 1):
            for i in range(PEER_TOPK):
                if i & stride == 0:
                    _exchange(z, i, i + stride)
        x = z
    return x


def _count_greater(v, x):
    b3 = v[7] > x
    b2 = jnp.where(b3, v[11], v[3]) > x
    b1 = jnp.where(b3, jnp.where(b2, v[13], v[9]), jnp.where(b2, v[5], v[1])) > x
    b0 = jnp.where(b3, jnp.where(b2, jnp.where(b1, v[14], v[12]), jnp.where(b1, v[10], v[8])),
                   jnp.where(b2, jnp.where(b1, v[6], v[4]), jnp.where(b1, v[2], v[0]))) > x
    lo = (jnp.where(b3, 8.0, 0.0) + jnp.where(b2, 4.0, 0.0)) + (jnp.where(b1, 2.0, 0.0) + jnp.where(b0, 1.0, 0.0))
    return jnp.where(v[15] > x, float(PEER_TOPK), lo)


def _column_sum(tiles):
    parts = list(tiles)
    while len(parts) > 1:
        parts = [parts[i] + parts[i + 1] for i in range(0, len(parts) - 1, 2)] + parts[len(parts) & ~1:]
    total = parts[0]
    for shift in (4, 2, 1):
        total = total + pltpu.roll(total, shift, 0)
    return total


def _route_head_fast(s1, s2, outs, h, cols):
    r2_ref, e2_ref, c1_ref, e1_ref = outs
    n_tiles = N_KEYS // SUBLANES
    t1 = [s1[i * SUBLANES:(i + 1) * SUBLANES] for i in range(n_tiles)]
    t2 = [s2[i * SUBLANES:(i + 1) * SUBLANES] for i in range(n_tiles)]
    v1 = _top16_values(t1)
    v2 = _top16_values(t2)
    sub = lax.broadcasted_iota(jnp.int32, t1[0].shape, 0)

    def by_sublane(vals):
        out = vals[0]
        for b in range(1, SUBLANES):
            out = jnp.where(sub == b, vals[b], out)
        return out

    v2_lo, v2_hi, v1_hi = by_sublane(v2[0:8]), by_sublane(v2[8:16]), by_sublane(v1[8:16])
    cand = [v1[0] + v2_lo, v1[0] + v2_hi] + [v1[a] + v2_lo for a in range(1, 8)] + [v1_hi + v2[0]]
    tau = _top16_values(cand)[PEER_TOPK - 1]
    chosen = [jnp.where(c >= tau, 1.0, 0.0) for c in cand]
    counts = [_column_sum(chosen[0:2])] + [_column_sum([chosen[1 + a]]) for a in range(1, 8)]
    counts += [jnp.broadcast_to(chosen[9][a:a + 1], sub.shape) for a in range(SUBLANES)]
    z = _column_sum([ch * jnp.exp(c - cand[0][0:1]) for ch, c in zip(chosen, cand)])
    inv_z = 1.0 / z

    rank2 = [_count_greater(v2, x) for x in t2]
    c1 = []
    for x in t1:
        c = jnp.zeros_like(x)
        for a in range(PEER_TOPK):
            c = jnp.where(x == v1[a], counts[a], c)
        c1.append(c)
    r2_ref[h, :, cols] = jnp.concatenate(rank2, axis=0).astype(BF)
    e2_ref[h, :, cols] = jnp.concatenate([jnp.exp(x - v2[0]) for x in t2], axis=0).astype(BF)
    c1_ref[h, :, cols] = jnp.concatenate(c1, axis=0)
    e1_ref[h, :, cols] = jnp.concatenate([jnp.exp(x - v1[0]) * inv_z for x in t1], axis=0)

    no_tie_rank_sum = float(sum(range(PEER_TOPK)) + (N_KEYS - PEER_TOPK) * PEER_TOPK)
    bad = jnp.where(_column_sum(rank2) != no_tie_rank_sum, 1.0, 0.0)
    bad = bad + jnp.where(_column_sum([jnp.where(x >= v1[PEER_TOPK - 1], 1.0, 0.0) for x in t1]) != PEER_TOPK, 1.0, 0.0)
    for a in range(PEER_TOPK - 1):
        bad = bad + jnp.where(v1[a] == v1[a + 1], 1.0, 0.0)
    return bad + jnp.where(_column_sum(chosen) != PEER_TOPK, 1.0, 0.0)


def _route_kernel(x_ref, wh_ref, wl_ref, kh_ref, kl_ref, fidx_ref, r2_ref, e2_ref, c1_ref, e1_ref):
    x = x_ref[...]
    t = x.shape[0]
    xh, xl = _split_bf16(x)
    q = _dot(xh, wh_ref[...]) + (_dot(xl, wh_ref[...]) + _dot(xh, wl_ref[...]))
    outs = (r2_ref, e2_ref, c1_ref, e1_ref)

    def scores(hs):
        qh, ql = _split_bf16(q[:, hs * PEER_HALF:(hs + 1) * PEER_HALF])
        return _dot_nt(kh_ref[hs], qh) + (_dot_nt(kl_ref[hs], qh) + _dot_nt(kh_ref[hs], ql))

    all_scores = [scores(hs) for hs in range(2 * PEER_HEADS)]
    bad = []
    for h in range(PEER_HEADS):
        flags = [_route_head_fast(all_scores[2 * h][:, c * LANES:(c + 1) * LANES],
                                  all_scores[2 * h + 1][:, c * LANES:(c + 1) * LANES], outs, h,
                                  slice(c * LANES, (c + 1) * LANES)) for c in range(t // LANES)]
        bad.append(sum(flags[1:], flags[0]))

    for h in range(PEER_HEADS):
        @pl.when(jnp.max(bad[h]) > 0.0)
        def _(h=h):
            _route_head_exact(all_scores[2 * h], all_scores[2 * h + 1], fidx_ref[...], outs, h)


def _route(x1, lw, tb):
    n = x1.shape[0]
    fidx = jnp.broadcast_to(jnp.asarray(_cand_flat_index())[:, None], (CAND_ROWS, tb))
    out = jax.ShapeDtypeStruct((PEER_HEADS, N_KEYS, n), F32)
    out_bf = jax.ShapeDtypeStruct((PEER_HEADS, N_KEYS, n), BF)
    ospec = pl.BlockSpec((PEER_HEADS, N_KEYS, tb), lambda i: (0, 0, i))
    full2 = lambda a: pl.BlockSpec(a.shape, lambda i: (0, 0))
    full3 = lambda a: pl.BlockSpec(a.shape, lambda i: (0, 0, 0))
    return pl.pallas_call(
        _route_kernel,
        out_shape=(out_bf, out_bf, out, out),
        grid=(n // tb,),
        in_specs=[pl.BlockSpec((tb, D_MODEL), lambda i: (i, 0)), full2(lw["wq_hi"]), full2(lw["wq_lo"]),
                  full3(lw["keys_hi"]), full3(lw["keys_lo"]), full2(fidx)],
        out_specs=(ospec, ospec, ospec, ospec),
        compiler_params=_params(("arbitrary",)),
        name="peer_route",
    )(x1, lw["wq_hi"], lw["wq_lo"], lw["keys_hi"], lw["keys_lo"], fidx)


PEER_EXPERT_BLOCK = 1024


def _peer_kernel(x_ref, u_ref, vt_ref, r2_ref, e2_ref, c1_ref, e1_ref, lg_ref, lb_ref, o_ref, xb, acc, wt, rowbuf, act):
    s = pl.program_id(1)
    last = pl.num_programs(1) - 1
    tb = x_ref.shape[0]

    @pl.when(s == 0)
    def _():
        xb[...] = x_ref[...].T.astype(BF)
        acc[...] = jnp.zeros_like(acc)
        act[1] = jnp.zeros(act.shape[1:], BF)

    keys_per_step = PEER_EXPERT_BLOCK // N_KEYS
    first_key = pl.multiple_of(jnp.maximum(s - 1, 0) * keys_per_step, keys_per_step)

    for h in range(PEER_HEADS):
        for src, ref in enumerate((c1_ref, e1_ref)):
            blk = ref[h, pl.ds(first_key, keys_per_step), :]
            for j in range(keys_per_step):
                rowbuf[src, h, j] = jnp.broadcast_to(blk[j:j + 1], (SUBLANES, tb)).astype(BF)

    half = N_KEYS // 2
    group = keys_per_step // 2
    tiles = half // SUBLANES

    def spread(row):
        return jnp.broadcast_to(row[None], (tiles, SUBLANES, BF_CHUNK)).reshape(half, BF_CHUNK)

    for tc in range(tb // BF_CHUNK):
        cols = slice(tc * BF_CHUNK, (tc + 1) * BF_CHUNK)
        for jg in range(keys_per_step // group):
            for kh in range(2):
                keys = slice(kh * half, (kh + 1) * half)
                w = [jnp.zeros((half, BF_CHUNK), BF) for _ in range(group)]
                for h in range(PEER_HEADS):
                    r2 = r2_ref[h, keys, cols]
                    e2 = e2_ref[h, keys, cols]
                    for jj in range(group):
                        j = jg * group + jj
                        c1 = spread(rowbuf[0, h, j, :, cols])
                        e1 = spread(rowbuf[1, h, j, :, cols])
                        w[jj] = w[jj] + jnp.where(r2 < c1, e2 * e1, 0)
                for jj in range(group):
                    rows = slice((jg * group + jj) * N_KEYS + kh * half, (jg * group + jj) * N_KEYS + (kh + 1) * half)
                    wt[rows, cols] = w[jj]
    wt[...] = wt[...] * act[(s + 1) % 2]
    acc[...] += _dot(vt_ref[...], wt[...])
    act[s % 2] = jax.nn.gelu(_dot(u_ref[...], xb[...]).astype(BF))

    @pl.when(s == last)
    def _():
        r = DN_ALPHA * x_ref[...] + acc[...].T
        o_ref[...] = _layer_norm(r, lg_ref[...], lb_ref[...])


def _peer(x1, route, lw, tb):
    n = x1.shape[0]
    eb = PEER_EXPERT_BLOCK
    nblk = N_EXPERTS // eb
    rspec = pl.BlockSpec((PEER_HEADS, N_KEYS, tb), lambda i, e: (0, 0, i))
    vec = pl.BlockSpec((1, D_MODEL), lambda i, e: (0, 0))
    return pl.pallas_call(
        _peer_kernel,
        out_shape=jax.ShapeDtypeStruct((n, D_MODEL), F32),
        grid=(n // tb, nblk + 1),
        in_specs=[pl.BlockSpec((tb, D_MODEL), lambda i, s: (i, 0)),
                  pl.BlockSpec((eb, D_MODEL), lambda i, s: (jnp.minimum(s, nblk - 1), 0)),
                  pl.BlockSpec((D_MODEL, eb), lambda i, s: (0, jnp.maximum(s - 1, 0))),
                  rspec, rspec, rspec, rspec, vec, vec],
        out_specs=pl.BlockSpec((tb, D_MODEL), lambda i, s: (i, 0)),
        scratch_shapes=[pltpu.VMEM((D_MODEL, tb), BF), pltpu.VMEM((D_MODEL, tb), F32), pltpu.VMEM((eb, tb), BF),
                        pltpu.VMEM((2, PEER_HEADS, eb // N_KEYS, SUBLANES, tb), BF),
                        pltpu.VMEM((2, eb, tb), BF)],
        compiler_params=_params(("arbitrary", "arbitrary")),
        name="peer_experts",
    )(x1, lw["peer_u"], lw["peer_vt"], *route, lw["ln2_g"], lw["ln2_b"])


def _ple_kernel(x_ref, pe_ref, wg_ref, wp_ref, lg_ref, lb_ref, o_ref):
    x = x_ref[...]
    ple = jax.nn.sigmoid(_dot(x.astype(BF), wg_ref[...])) * _dot(pe_ref[...].astype(BF), wp_ref[...])
    o_ref[...] = _layer_norm(DN_ALPHA * x + ple, lg_ref[...], lb_ref[...])


def _ple(x2, pe, lw, tm):
    n = x2.shape[0]
    full = lambda a: pl.BlockSpec(a.shape, lambda i: (0, 0))
    ws = (lw["ple_w_g"], lw["ple_w_p"], lw["ln3_g"], lw["ln3_b"])
    return pl.pallas_call(
        _ple_kernel,
        out_shape=jax.ShapeDtypeStruct((n, D_MODEL), F32),
        grid=(n // tm,),
        in_specs=[pl.BlockSpec((tm, D_MODEL), lambda i: (i, 0)), pl.BlockSpec((tm, PLE_DIM), lambda i: (i, 0))]
        + [full(a) for a in ws],
        out_specs=pl.BlockSpec((tm, D_MODEL), lambda i: (i, 0)),
        compiler_params=_params(("arbitrary",)),
        name="ple",
    )(x2, pe, *ws)


def _time_major(u, bsz, t, bp):
    u = jnp.transpose(u.reshape(bsz, t, -1), (1, 0, 2))
    return jnp.pad(u, ((0, 0), (0, bp - bsz), (0, 0))).reshape(t * bp, -1)


def _batch_major(y, bsz, t, bp):
    return jnp.transpose(y.reshape(t, bp, -1)[:, :bsz], (1, 0, 2)).reshape(bsz * t, -1)


def _layer(x, pe, h0_re, h0_im, cache_k, cache_v, s0, pos, lw, bsz, t, prompt, n_rows):
    n = bsz * t
    h = _in_projection(x, lw["w_in"], 512, IN_COLS // 2)

    bp = max(8, bsz)
    u_tb = _time_major(h[:, OFF_U_SSM:OFF_U_SSM + SSM_WIDTH], bsz, t, bp)
    h0 = jnp.concatenate([h0_re.reshape(bsz, SSM_FLAT), h0_im.reshape(bsz, SSM_FLAT)], axis=1).astype(F32)
    h0 = jnp.pad(h0, ((0, bp - bsz), (0, 0)))
    y_tb, h_last = _s5_mixer(u_tb, h0, lw["s5"], t, bp, CHUNK)
    y_ssm = _batch_major(y_tb, bsz, t, bp)
    new_re = h_last[:bsz, :SSM_FLAT].reshape(bsz, SSM_GROUPS, SSM_STATE)
    new_im = h_last[:bsz, SSM_FLAT:].reshape(bsz, SSM_GROUPS, SSM_STATE)

    k_att = h[:, OFF_K_ATT:OFF_K_ATT + ATT_WIDTH].reshape(bsz, t, ATT_HEADS, ATT_HEAD_DIM)
    v_att = h[:, OFF_V_ATT:OFF_V_ATT + ATT_WIDTH].reshape(bsz, t, ATT_HEADS, ATT_HEAD_DIM)
    if prompt:
        y_att = _attn_prompt(h, lw["rel_bias"], bsz, t)
        new_k, new_v = k_att[:, -n_rows:], v_att[:, -n_rows:]
    else:
        r = cache_k.shape[1]
        y_att = _attn_sample(h, cache_k.reshape(bsz, r, ATT_WIDTH), cache_v.reshape(bsz, r, ATT_WIDTH),
                             lw["rel_bias"], bsz)
        new_k, new_v = k_att, v_att

    y_ret, s_pair = _retention(h, _pair_states(s0), _retention_tables(pos), lw["ret_gn_g"], bsz, t,
                               4 if prompt else 1)
    s_fin = _unpair_states(s_pair)

    tok = 512 if n % 512 == 0 else 256
    x1 = _merge(x, h, y_ssm, y_att, y_ret, lw, tok)
    route = _route(x1, lw, 256)
    x2 = _peer(x1, route, lw, tok)
    x3 = _ple(x2, pe, lw, tok)
    return x3, (new_re, new_im, new_k, new_v, s_fin)


def _layer_weights(li, w_in, ssm_lam_re, ssm_lam_im, ssm_b_re, ssm_b_im, ssm_c_re, ssm_c_im, ssm_log_dt, ssm_d,
                   ssm_w_glu, ssm_b_glu, att_rel_bias, ret_gn_g, w_br_ssm, w_br_att, w_br_ret, w_o, ln1_g, ln1_b,
                   peer_w_q, peer_sub_keys, peer_u, peer_v, ln2_g, ln2_b, ple_w_g, ple_w_p, ln3_g, ln3_b):
    vec = lambda a: a[li].astype(F32).reshape(1, -1)
    w = w_in[li]
    wq_hi, wq_lo = _split_bf16(peer_w_q[li].astype(F32))
    keys_hi, keys_lo = _split_bf16(peer_sub_keys[li].astype(F32).reshape(2 * PEER_HEADS, N_KEYS, PEER_HALF))
    return {
        "w_in": jnp.concatenate([w[:, OLD_GATE_START:], w[:, :OLD_GATE_START]], axis=1).astype(BF),
        "s5": _s5_tables(ssm_lam_re[li], ssm_lam_im[li], ssm_b_re[li], ssm_b_im[li], ssm_c_re[li], ssm_c_im[li],
                         ssm_log_dt[li], ssm_d[li], ssm_w_glu[li], ssm_b_glu[li]),
        "rel_bias": _rel_bias_tile(att_rel_bias[li]),
        "ret_gn_g": vec(ret_gn_g),
        "w_br_ssm": w_br_ssm[li].astype(BF), "w_br_att": w_br_att[li].astype(BF), "w_br_ret": w_br_ret[li].astype(BF),
        "w_o": w_o[li].astype(BF), "ln1_g": vec(ln1_g), "ln1_b": vec(ln1_b),
        "wq_hi": wq_hi, "wq_lo": wq_lo, "keys_hi": keys_hi, "keys_lo": keys_lo,
        "peer_u": peer_u[li].astype(BF), "peer_vt": peer_v[li].astype(BF).T,
        "ln2_g": vec(ln2_g), "ln2_b": vec(ln2_b),
        "ple_w_g": ple_w_g[li].astype(BF), "ple_w_p": ple_w_p[li].astype(BF), "ln3_g": vec(ln3_g), "ln3_b": vec(ln3_b),
    }


def kernel(x_prompt, x_sample, p_prompt, p_sample, state_ssm_re, state_ssm_im, cache_attn_k, cache_attn_v, state_ret, w_in, ssm_lam_re, ssm_lam_im, ssm_b_re, ssm_b_im, ssm_c_re, ssm_c_im, ssm_log_dt, ssm_d, ssm_w_glu, ssm_b_glu, att_rel_bias, ret_gn_g, w_br_ssm, w_br_att, w_br_ret, w_o, ln1_g, ln1_b, peer_w_q, peer_sub_keys, peer_u, peer_v, ln2_g, ln2_b, ple_w_g, ple_w_p, ln3_g, ln3_b):
    bp, tp, _ = x_prompt.shape
    bs, ts, _ = x_sample.shape
    n_rows = cache_attn_k.shape[2]
    pos_p = np.arange(tp)
    pos_s = PAST_LEN + np.arange(ts)
    y_p = x_prompt.reshape(bp * tp, D_MODEL).astype(F32)
    y_s = x_sample.reshape(bs * ts, D_MODEL).astype(F32)
    new_p, new_s = [], []
    for li in range(DEPTH):
        lw = _layer_weights(li, w_in, ssm_lam_re, ssm_lam_im, ssm_b_re, ssm_b_im, ssm_c_re, ssm_c_im, ssm_log_dt,
                            ssm_d, ssm_w_glu, ssm_b_glu, att_rel_bias, ret_gn_g, w_br_ssm, w_br_att, w_br_ret, w_o,
                            ln1_g, ln1_b, peer_w_q, peer_sub_keys, peer_u, peer_v, ln2_g, ln2_b, ple_w_g, ple_w_p,
                            ln3_g, ln3_b)
        zero_ssm = jnp.zeros((bp, SSM_GROUPS, SSM_STATE), F32)
        zero_ret = jnp.zeros((bp, RET_HEADS, RET_DIM, RET_DIM), F32)
        y_p, st = _layer(y_p, p_prompt[li].reshape(bp * tp, PLE_DIM), zero_ssm, zero_ssm, None, None, zero_ret,
                         pos_p, lw, bp, tp, True, n_rows)
        new_p.append(st)
        y_s, st = _layer(y_s, p_sample[li].reshape(bs * ts, PLE_DIM), state_ssm_re[li], state_ssm_im[li],
                         cache_attn_k[li], cache_attn_v[li], state_ret[li], pos_s, lw, bs, ts, False, n_rows)
        new_s.append(st)
    stack = lambda states, k: jnp.stack([s[k] for s in states])
    return (y_p.reshape(bp, tp, D_MODEL), y_s.reshape(bs, ts, D_MODEL),
            stack(new_p, 0), stack(new_p, 1), stack(new_p, 2), stack(new_p, 3), stack(new_p, 4),
            stack(new_s, 0), stack(new_s, 1), stack(new_s, 2), stack(new_s, 3), stack(new_s, 4))
```

```python
import functools

import jax
import jax.numpy as jnp
import numpy as np
from jax import lax
from jax.experimental import pallas as pl
from jax.experimental.pallas import tpu as pltpu

F32 = jnp.float32
BF = jnp.bfloat16

D_MODEL = 1024
DEPTH = 2
PAST_LEN = 2048
CHUNK = 64
PLE_DIM = 256
SSM_WIDTH = 256
SSM_GROUP = 16
SSM_GROUPS = 16
SSM_STATE = 64
SSM_FLAT = SSM_GROUPS * SSM_STATE
ATT_HEADS = 8
ATT_HEAD_DIM = 64
ATT_WIDTH = 512
ATT_PAST = 8 * CHUNK
ATT_BAND = ATT_PAST + CHUNK
REL_CLIP = 256
RET_HEADS = 4
RET_DIM = 64
ROPE_BASE = 10000.0
PEER_HEADS = 8
PEER_HALF = 128
N_KEYS = 128
N_EXPERTS = N_KEYS * N_KEYS
PEER_TOPK = 16
SUBLANES = 8
BF_CHUNK = 256
DN_ALPHA = (2 * DEPTH) ** 0.25
LN_EPS = 1e-5
LANES = 128
NEG_MASK = float(jnp.finfo(jnp.float32).min)

IN_COLS = 5888
OFF_GATES = 0
OFF_U_SSM = 3072
OFF_Q_ATT = 3328
OFF_K_ATT = 3840
OFF_V_ATT = 4352
OFF_Q_RET = 4864
OFF_K_RET = 5120
OFF_V_RET = 5376
OFF_G_RET = 5632
OLD_GATE_START = 2816

VMEM_LIMIT = 48 * 1024 * 1024


def _params(sem):
    return pltpu.CompilerParams(dimension_semantics=sem, vmem_limit_bytes=VMEM_LIMIT)


def _dot(a, b):
    return jnp.dot(a, b, preferred_element_type=F32)


def _dot_nt(a, b):
    return lax.dot_general(a, b, (((1,), (1,)), ((), ())), preferred_element_type=F32)


def _dot_tn(a, b):
    return lax.dot_general(a, b, (((0,), (0,)), ((), ())), preferred_element_type=F32)


def _layer_norm(r, g, b):
    mu = jnp.mean(r, axis=-1, keepdims=True)
    d = r - mu
    var = jnp.mean(d * d, axis=-1, keepdims=True)
    return d * lax.rsqrt(var + LN_EPS) * g + b


def _matmul_kernel(x_ref, w_ref, o_ref):
    o_ref[...] = _dot(x_ref[...].astype(BF), w_ref[...])


def _in_projection(x, w, tm, tn):
    m, k = x.shape
    n = w.shape[1]
    return pl.pallas_call(
        _matmul_kernel,
        out_shape=jax.ShapeDtypeStruct((m, n), F32),
        grid=(n // tn, m // tm),
        in_specs=[pl.BlockSpec((tm, k), lambda j, i: (i, 0)),
                  pl.BlockSpec((k, tn), lambda j, i: (0, j))],
        out_specs=pl.BlockSpec((tm, tn), lambda j, i: (i, j)),
        compiler_params=_params(("arbitrary", "arbitrary")),
        name="in_projection",
    )(x, w)


def _s5_kernel(u_ref, bm_ref, a_ref, cm_ref, d_ref, wg_ref, bg_ref, h0_ref, y_ref, hl_ref, hbuf, hst, *, steps, bp):
    @pl.when(pl.program_id(0) == 0)
    def _():
        hst[...] = h0_ref[...]

    u = u_ref[...]
    hbuf[...] = _dot(u.astype(BF), bm_ref[...])
    a_re = jnp.broadcast_to(a_ref[0:1, :], (bp, SSM_FLAT))
    a_im = jnp.broadcast_to(a_ref[1:2, :], (bp, SSM_FLAT))

    def step(t, carry):
        h_re, h_im = carry
        r0 = pl.multiple_of(t * bp, bp)
        n_re = a_re * h_re - a_im * h_im + hbuf[pl.ds(r0, bp), 0:SSM_FLAT]
        n_im = a_re * h_im + a_im * h_re + hbuf[pl.ds(r0, bp), SSM_FLAT:2 * SSM_FLAT]
        hbuf[pl.ds(r0, bp), 0:SSM_FLAT] = n_re
        hbuf[pl.ds(r0, bp), SSM_FLAT:2 * SSM_FLAT] = n_im
        return n_re, n_im

    h_re, h_im = lax.fori_loop(0, steps, step, (hst[:, 0:SSM_FLAT], hst[:, SSM_FLAT:2 * SSM_FLAT]))
    hst[:, 0:SSM_FLAT] = h_re
    hst[:, SSM_FLAT:2 * SSM_FLAT] = h_im
    hl_ref[...] = hst[...]
    y = _dot(hbuf[...].astype(BF), cm_ref[...]) + d_ref[...] * u
    z = jax.nn.gelu(y)
    y_ref[...] = z * jax.nn.sigmoid(_dot(z.astype(BF), wg_ref[...]) + bg_ref[...])


def _s5_mixer(u_tb, h0, sp, n_steps, bp, steps):
    rows = steps * bp
    kern = functools.partial(_s5_kernel, steps=steps, bp=bp)
    const = lambda c: (0, 0)
    return pl.pallas_call(
        kern,
        out_shape=(jax.ShapeDtypeStruct((n_steps * bp, SSM_WIDTH), F32),
                   jax.ShapeDtypeStruct((bp, 2 * SSM_FLAT), F32)),
        grid=(n_steps // steps,),
        in_specs=[pl.BlockSpec((rows, SSM_WIDTH), lambda c: (c, 0)),
                  pl.BlockSpec((SSM_WIDTH, 2 * SSM_FLAT), const),
                  pl.BlockSpec((2, SSM_FLAT), const),
                  pl.BlockSpec((2 * SSM_FLAT, SSM_WIDTH), const),
                  pl.BlockSpec((1, SSM_WIDTH), const),
                  pl.BlockSpec((SSM_WIDTH, SSM_WIDTH), const),
                  pl.BlockSpec((1, SSM_WIDTH), const),
                  pl.BlockSpec((bp, 2 * SSM_FLAT), const)],
        out_specs=(pl.BlockSpec((rows, SSM_WIDTH), lambda c: (c, 0)),
                   pl.BlockSpec((bp, 2 * SSM_FLAT), const)),
        scratch_shapes=[pltpu.VMEM((rows, 2 * SSM_FLAT), F32), pltpu.VMEM((bp, 2 * SSM_FLAT), F32)],
        compiler_params=_params(("arbitrary",)),
        name="s5_mixer",
    )(u_tb, sp["bmat"], sp["a"], sp["cmat"], sp["d"], sp["w_glu"], sp["b_glu"], h0)


def _s5_tables(lam_re, lam_im, b_re, b_im, c_re, c_im, log_dt, d_skip, w_glu, b_glu):
    lr, li = lam_re.astype(F32), lam_im.astype(F32)
    dt = jnp.exp(log_dt.astype(F32))[:, None]
    mag = jnp.exp(lr * dt)
    a_re, a_im = mag * jnp.cos(li * dt), mag * jnp.sin(li * dt)
    den = lr * lr + li * li
    k_re = ((a_re - 1.0) * lr + a_im * li) / den
    k_im = (a_im * lr - (a_re - 1.0) * li) / den
    br, bi = b_re.astype(F32), b_im.astype(F32)
    bbar_re = k_re[..., None] * br - k_im[..., None] * bi
    bbar_im = k_re[..., None] * bi + k_im[..., None] * br
    eye = jnp.eye(SSM_GROUPS, dtype=F32)

    def in_blockdiag(m):
        m = jnp.transpose(m, (0, 2, 1))
        return (m[:, :, None, :] * eye[:, None, :, None]).reshape(SSM_WIDTH, SSM_FLAT)

    def out_blockdiag(m):
        m = jnp.transpose(m, (0, 2, 1))
        return (m[:, :, None, :] * eye[:, None, :, None]).reshape(SSM_FLAT, SSM_WIDTH)

    bmat = jnp.concatenate([in_blockdiag(bbar_re), in_blockdiag(bbar_im)], axis=1)
    cmat = jnp.concatenate([out_blockdiag(c_re.astype(F32)), -out_blockdiag(c_im.astype(F32))], axis=0)
    a = jnp.stack([a_re.reshape(SSM_FLAT), a_im.reshape(SSM_FLAT)])
    return {"bmat": bmat.astype(BF), "cmat": cmat.astype(BF), "a": a,
            "d": d_skip.astype(F32).reshape(1, SSM_WIDTH), "w_glu": w_glu.astype(BF),
            "b_glu": b_glu.astype(F32).reshape(1, SSM_WIDTH)}


ATT_QROWS = 4 * CHUNK
ATT_WINDOW = ATT_PAST + ATT_QROWS


def _attn_prompt_kernel(q_ref, k0_ref, k1_ref, k2_ref, v0_ref, v1_ref, v2_ref, bias_ref, o_ref):
    tb = pl.program_id(2)
    kwin = jnp.concatenate([k0_ref[...], k1_ref[...], k2_ref[...]], axis=0).astype(BF)
    vwin = jnp.concatenate([v0_ref[...], v1_ref[...], v2_ref[...]], axis=0).astype(BF)
    first = lax.broadcasted_iota(jnp.int32, (1, LANES), 1) < ATT_HEAD_DIM
    q = q_ref[...] * (ATT_HEAD_DIM ** -0.5)
    q2 = jnp.concatenate([jnp.where(first, q, 0.0), jnp.where(first, 0.0, q)], axis=0).astype(BF)
    s = _dot_nt(q2, kwin) + jnp.concatenate([bias_ref[0], bias_ref[1]], axis=0)
    col = lax.broadcasted_iota(jnp.int32, (1, ATT_WINDOW), 1)
    s = jnp.where(col + (tb * ATT_QROWS - ATT_PAST) >= 0, s, NEG_MASK)
    p = jnp.exp(s - jnp.max(s, axis=-1, keepdims=True))
    o = _dot(p.astype(BF), vwin) / jnp.sum(p, axis=-1, keepdims=True)
    o_ref[...] = jnp.where(first, o[:ATT_QROWS], o[ATT_QROWS:])


def _attn_prompt(h, bias, bsz, t):
    rows = ATT_QROWS
    tbn = t // rows
    qc, kc, vc = OFF_Q_ATT // LANES, OFF_K_ATT // LANES, OFF_V_ATT // LANES
    band = jnp.concatenate(
        [jnp.pad(bias, ((0, 0), (0, 0), (c * CHUNK, ATT_WINDOW - ATT_BAND - c * CHUNK)), constant_values=NEG_MASK)
         for c in range(rows // CHUNK)], axis=1)

    def past(col0, back):
        return pl.BlockSpec((rows, LANES), lambda b, hp, tb: (b * tbn + jnp.maximum(tb - back, 0), col0 + hp))

    return pl.pallas_call(
        _attn_prompt_kernel,
        out_shape=jax.ShapeDtypeStruct((bsz * t, ATT_WIDTH), F32),
        grid=(bsz, ATT_HEADS // 2, tbn),
        in_specs=[past(qc, 0), past(kc, 2), past(kc, 1), past(kc, 0), past(vc, 2), past(vc, 1), past(vc, 0),
                  pl.BlockSpec((2, rows, ATT_WINDOW), lambda b, hp, tb: (hp, 0, 0))],
        out_specs=pl.BlockSpec((rows, LANES), lambda b, hp, tb: (b * tbn + tb, hp)),
        compiler_params=_params(("arbitrary", "arbitrary", "arbitrary")),
        name="attn_prompt",
    )(h, h, h, h, h, h, h, band)


def _attn_sample_kernel(q_ref, k_ref, v_ref, ck_ref, cv_ref, bias_ref, o_ref):
    kwin = jnp.concatenate([ck_ref[0], k_ref[...]], axis=0).astype(BF)
    vwin = jnp.concatenate([cv_ref[0], v_ref[...]], axis=0).astype(BF)
    first = lax.broadcasted_iota(jnp.int32, (1, LANES), 1) < ATT_HEAD_DIM
    q = q_ref[...] * (ATT_HEAD_DIM ** -0.5)
    q2 = jnp.concatenate([jnp.where(first, q, 0.0), jnp.where(first, 0.0, q)], axis=0).astype(BF)
    s = _dot_nt(q2, kwin) + jnp.concatenate([bias_ref[0], bias_ref[1]], axis=0)
    p = jnp.exp(s - jnp.max(s, axis=-1, keepdims=True))
    o = _dot(p.astype(BF), vwin) / jnp.sum(p, axis=-1, keepdims=True)
    o_ref[...] = jnp.where(first, o[:CHUNK], o[CHUNK:])


def _attn_sample(h, cache_k, cache_v, bias, bsz):
    qc, kc, vc = OFF_Q_ATT // LANES, OFF_K_ATT // LANES, OFF_V_ATT // LANES
    new = lambda col0: pl.BlockSpec((CHUNK, LANES), lambda b, hp: (b, col0 + hp))
    old = pl.BlockSpec((1, ATT_PAST, LANES), lambda b, hp: (b, 0, hp))
    return pl.pallas_call(
        _attn_sample_kernel,
        out_shape=jax.ShapeDtypeStruct((bsz * CHUNK, ATT_WIDTH), F32),
        grid=(bsz, ATT_HEADS // 2),
        in_specs=[new(qc), new(kc), new(vc), old, old,
                  pl.BlockSpec((2, CHUNK, ATT_BAND), lambda b, hp: (hp, 0, 0))],
        out_specs=pl.BlockSpec((CHUNK, LANES), lambda b, hp: (b, hp)),
        compiler_params=_params(("arbitrary", "arbitrary")),
        name="attn_sample",
    )(h, h, h, cache_k, cache_v, bias)


def _rel_bias_tile(table):
    dist = jnp.arange(-(CHUNK - 1), ATT_BAND)
    per_dist = table[jnp.clip(dist, -REL_CLIP, REL_CLIP) + REL_CLIP].astype(F32).T
    n = dist.shape[0]
    rev = jnp.roll(per_dist[:, ::-1], -(CHUNK - 1), axis=1)
    skew = jnp.tile(rev, (1, CHUNK))[:, :CHUNK * (n - 1)].reshape(ATT_HEADS, CHUNK, n - 1)
    return skew[:, :, :ATT_BAND]


def _ret_kernel(q_ref, k_ref, v_ref, g_ref, cos_ref, sin_ref, intra_ref, qw_ref, kw_ref, dec_ref, bd_ref, gn_ref,
                s0_ref, y_ref, sf_ref, state, *, cpb):
    @pl.when(pl.program_id(1) == 0)
    def _():
        state[...] = s0_ref[0]

    lane = lax.broadcasted_iota(jnp.int32, (1, LANES), 1)
    first = lane < RET_DIM
    low_half = (lane % RET_DIM) < (RET_DIM // 2)

    for c in range(cpb):
        rows = slice(c * CHUNK, (c + 1) * CHUNK)
        cos = cos_ref[rows, :]
        sin = sin_ref[rows, :]

        def rope(x):
            swapped = jnp.where(low_half, pltpu.roll(x, LANES - RET_DIM // 2, 1), pltpu.roll(x, RET_DIM // 2, 1))
            return x * cos + swapped * sin

        for hp in range(RET_HEADS // 2):
            cols = slice(hp * LANES, (hp + 1) * LANES)
            q = rope(q_ref[rows, cols]) * (RET_DIM ** -0.5)
            k = rope(k_ref[rows, cols])
            kb = k.astype(BF)
            vb = v_ref[rows, cols].astype(BF)
            intra = []
            for hh in range(2):
                qm = jnp.where(first if hh == 0 else jnp.logical_not(first), q, 0.0).astype(BF)
                s = _dot_nt(qm, kb) * intra_ref[2 * hp + hh]
                intra.append(_dot(s.astype(BF), vb))
            s_prev = state[hp]
            o = jnp.where(first, intra[0], intra[1]) + _dot((q * qw_ref[hp]).astype(BF), s_prev.astype(BF))
            kv = _dot_tn((k * kw_ref[hp]).astype(BF), vb)
            state[hp] = dec_ref[hp] * s_prev + bd_ref[...] * kv
            sum_a = jnp.sum(jnp.where(first, o, 0.0), axis=-1, keepdims=True)
            sum_b = jnp.sum(o, axis=-1, keepdims=True) - sum_a
            d = o - jnp.where(first, sum_a, sum_b) * (1.0 / RET_DIM)
            sq = d * d
            var_a = jnp.sum(jnp.where(first, sq, 0.0), axis=-1, keepdims=True)
            var_b = jnp.sum(sq, axis=-1, keepdims=True) - var_a
            var = jnp.where(first, var_a, var_b) * (1.0 / RET_DIM)
            g = g_ref[rows, cols]
            y_ref[rows, cols] = d * lax.rsqrt(var + LN_EPS) * gn_ref[:, cols] * (g * jax.nn.sigmoid(g))
    sf_ref[0] = state[...]


def _retention(h, s0, rt, gn, bsz, t, cpb):
    rows = cpb * CHUNK
    tbn = t // rows
    width = RET_HEADS * RET_DIM
    col = lambda off: pl.BlockSpec((rows, width), lambda b, tb: (b * tbn + tb, off // width))
    tab = pl.BlockSpec((rows, LANES), lambda b, tb: (tb, 0))
    const3 = lambda shape: pl.BlockSpec(shape, lambda b, tb: (0, 0, 0))
    st = pl.BlockSpec((1, 2, LANES, LANES), lambda b, tb: (b, 0, 0, 0))
    return pl.pallas_call(
        functools.partial(_ret_kernel, cpb=cpb),
        out_shape=(jax.ShapeDtypeStruct((bsz * t, width), F32),
                   jax.ShapeDtypeStruct((bsz, 2, LANES, LANES), F32)),
        grid=(bsz, tbn),
        in_specs=[col(OFF_Q_RET), col(OFF_K_RET), col(OFF_V_RET), col(OFF_G_RET), tab, tab,
                  const3((RET_HEADS, CHUNK, CHUNK)), const3((2, CHUNK, LANES)), const3((2, CHUNK, LANES)),
                  const3((2, LANES, LANES)), pl.BlockSpec((LANES, LANES), lambda b, tb: (0, 0)),
                  pl.BlockSpec((1, width), lambda b, tb: (0, 0)), st],
        out_specs=(pl.BlockSpec((rows, width), lambda b, tb: (b * tbn + tb, 0)), st),
        scratch_shapes=[pltpu.VMEM((2, LANES, LANES), F32)],
        compiler_params=_params(("arbitrary", "arbitrary")),
        name="retention",
    )(h, h, h, h, rt["cos"], rt["sin"], rt["intra"], rt["qw"], rt["kw"], rt["dec"], rt["bd"], gn, s0)


def _retention_tables(pos):
    f32 = np.float32
    lg = np.log1p(-(f32(2.0) ** (f32(-5.0) - np.arange(RET_HEADS, dtype=f32)))).astype(f32)
    i = np.arange(CHUNK, dtype=f32)
    intra = np.exp(lg[:, None, None] * np.abs(i[:, None] - i[None, :])).astype(f32)
    lane_head = np.arange(2 * LANES) // RET_DIM
    lg_lane = lg[lane_head]
    qw = np.exp(lg_lane[None, :] * (i[:, None] + f32(1.0))).astype(f32)
    kw = np.exp(lg_lane[None, :] * (f32(CHUNK - 1.0) - i)[:, None]).astype(f32)
    pair = lambda m: np.stack([m[:, :LANES], m[:, LANES:]])
    same_head = (np.arange(LANES)[:, None] // RET_DIM) == (np.arange(LANES)[None, :] // RET_DIM)
    bd = same_head.astype(f32)
    dec_lane = np.exp(lg_lane * f32(CHUNK)).astype(f32)
    dec = np.stack([bd * dec_lane[:LANES][:, None], bd * dec_lane[LANES:][:, None]])
    half = RET_DIM // 2
    freqs = (f32(ROPE_BASE) ** (-np.arange(half, dtype=f32) / f32(half))).astype(f32)
    ang = (np.asarray(pos).astype(f32)[:, None] * freqs[None]).astype(f32)
    cos = np.tile(np.cos(ang).astype(f32), (1, LANES // half))
    sin = np.tile(np.concatenate([-np.sin(ang), np.sin(ang)], axis=1).astype(f32), (1, LANES // RET_DIM))
    return {"intra": intra, "qw": pair(qw), "kw": pair(kw), "dec": dec, "bd": bd, "cos": cos, "sin": sin}


def _pair_states(s):
    b = s.shape[0]
    s = s.astype(F32).reshape(b, 2, 2, RET_DIM, RET_DIM)
    out = jnp.zeros((b, 2, 2, RET_DIM, 2, RET_DIM), F32)
    out = out.at[:, :, 0, :, 0, :].set(s[:, :, 0]).at[:, :, 1, :, 1, :].set(s[:, :, 1])
    return out.reshape(b, 2, LANES, LANES)


def _unpair_states(s):
    b = s.shape[0]
    s = s.reshape(b, 2, 2, RET_DIM, 2, RET_DIM)
    return jnp.stack([s[:, :, 0, :, 0, :], s[:, :, 1, :, 1, :]], axis=2).reshape(b, RET_HEADS, RET_DIM, RET_DIM)


def _merge_kernel(x_ref, g1_ref, g2_ref, g3_ref, ys_ref, ya_ref, yr_ref, ws_ref, wa_ref, wr_ref, wo_ref,
                  lg_ref, lb_ref, o_ref):
    merged = (jax.nn.sigmoid(g1_ref[...]) * _dot(ys_ref[...].astype(BF), ws_ref[...])
              + jax.nn.sigmoid(g2_ref[...]) * _dot(ya_ref[...].astype(BF), wa_ref[...])
              + jax.nn.sigmoid(g3_ref[...]) * _dot(yr_ref[...].astype(BF), wr_ref[...]))
    r = DN_ALPHA * x_ref[...] + _dot(merged.astype(BF), wo_ref[...])
    o_ref[...] = _layer_norm(r, lg_ref[...], lb_ref[...])


def _merge(x, h, y_ssm, y_att, y_ret, lw, tm):
    n = x.shape[0]
    row = lambda w: pl.BlockSpec((tm, w), lambda i: (i, 0))
    gate = lambda j: pl.BlockSpec((tm, D_MODEL), lambda i: (i, j))
    full = lambda a: pl.BlockSpec(a.shape, lambda i: (0, 0))
    ws = (lw["w_br_ssm"], lw["w_br_att"], lw["w_br_ret"], lw["w_o"], lw["ln1_g"], lw["ln1_b"])
    return pl.pallas_call(
        _merge_kernel,
        out_shape=jax.ShapeDtypeStruct((n, D_MODEL), F32),
        grid=(n // tm,),
        in_specs=[row(D_MODEL), gate(0), gate(1), gate(2), row(SSM_WIDTH), row(ATT_WIDTH),
                  row(RET_HEADS * RET_DIM)] + [full(a) for a in ws],
        out_specs=row(D_MODEL),
        compiler_params=_params(("arbitrary",)),
        name="merge",
    )(x, h, h, h, y_ssm, y_att, y_ret, *ws)


CAND_ROWS = 80


def _cand_flat_index():
    f = [b for b in range(16)]
    for a in range(1, 8):
        f += [a * 16 + b for b in range(8)]
    f += [a * 16 for a in range(8, 16)]
    return np.asarray(f, np.float32)


def _split_bf16(x):
    hi = x.astype(BF)
    return hi, (x - hi.astype(F32)).astype(BF)


def _take16(work, order):
    t = work.shape[1]
    row16 = lax.broadcasted_iota(jnp.int32, (PEER_TOPK, t), 0)
    rank = jnp.full(work.shape, float(PEER_TOPK), F32)
    vals = jnp.zeros((PEER_TOPK, t), F32)
    for r in range(PEER_TOPK):
        m = jnp.max(work, axis=0, keepdims=True)
        first = jnp.min(jnp.where(work == m, order, 1e9), axis=0, keepdims=True)
        sel = order == first
        rank = jnp.where(sel, float(r), rank)
        work = jnp.where(sel, -jnp.inf, work)
        vals = jnp.where(row16 == r, m, vals)
    return rank, vals


def _route_head_exact(s1, s2, fidx, outs, h):
    r2_ref, e2_ref, c1_ref, e1_ref = outs
    key_idx = lax.broadcasted_iota(jnp.int32, s1.shape, 0).astype(F32)
    rank1, v1 = _take16(s1, key_idx)
    rank2, v2 = _take16(s2, key_idx)
    blocks = [v1[0:1] + v2]
    for a in range(1, 8):
        blocks.append(v1[a:a + 1] + v2[0:8])
    blocks.append(v1[8:16] + v2[0:1])
    cand = jnp.concatenate(blocks, axis=0)
    rank_c, _ = _take16(cand, fidx)
    chosen = jnp.where(rank_c < float(PEER_TOPK), 1.0, 0.0)
    counts = [jnp.sum(chosen[0:16], axis=0, keepdims=True)]
    for a in range(1, 8):
        counts.append(jnp.sum(chosen[8 + 8 * a:16 + 8 * a], axis=0, keepdims=True))
    for a in range(8, 16):
        counts.append(chosen[64 + a:65 + a])
    c1 = jnp.zeros(s1.shape, F32)
    for a in range(PEER_TOPK):
        c1 = jnp.where(rank1 == float(a), counts[a], c1)
    z = jnp.sum(chosen * jnp.exp(cand - cand[0:1]), axis=0, keepdims=True)
    r2_ref[h] = rank2.astype(BF)
    e2_ref[h] = jnp.exp(s2 - v2[0:1]).astype(BF)
    c1_ref[h] = c1
    e1_ref[h] = jnp.exp(s1 - v1[0:1]) / z


def _batcher_pairs(n):
    pairs = []

    def merge(lo, m, r):
        step = r * 2
        if step < m:
            merge(lo, m, step)
            merge(lo + r, m, step)
            pairs.extend((i, i + r) for i in range(lo + r, lo + m - r, step))
        else:
            pairs.append((lo, lo + r))

    def sort(lo, m):
        if m > 1:
            sort(lo, m // 2)
            sort(lo + m // 2, m // 2)
            merge(lo, m, 1)

    sort(0, n)
    return pairs


SORT16_PAIRS = _batcher_pairs(PEER_TOPK)


def _exchange(x, i, j):
    a, b = x[i], x[j]
    if b is None:
        return
    if a is None:
        x[i], x[j] = b, None
    else:
        x[i], x[j] = jnp.maximum(a, b), jnp.minimum(a, b)


def _top16_values(tiles):
    x = list(tiles) + [None] * (PEER_TOPK - len(tiles))
    for i, j in SORT16_PAIRS:
        _exchange(x, i, j)
    for shift in (4, 2, 1):
        y = [None if v is None else pltpu.roll(v, shift, 0) for v in x]
        z = []
        for i in range(PEER_TOPK):
            a, b = x[i], y[PEER_TOPK - 1 - i]
            z.append(b if a is None else a if b is None else jnp.maximum(a, b))
        for stride in (8, 4, 2, 1):
            for i in range(PEER_TOPK):
                if i & stride == 0:
                    _exchange(z, i, i + stride)
        x = z
    return x


def _count_greater(v, x):
    b3 = v[7] > x
    b2 = jnp.where(b3, v[11], v[3]) > x
    b1 = jnp.where(b3, jnp.where(b2, v[13], v[9]), jnp.where(b2, v[5], v[1])) > x
    b0 = jnp.where(b3, jnp.where(b2, jnp.where(b1, v[14], v[12]), jnp.where(b1, v[10], v[8])),
                   jnp.where(b2, jnp.where(b1, v[6], v[4]), jnp.where(b1, v[2], v[0]))) > x
    lo = (jnp.where(b3, 8.0, 0.0) + jnp.where(b2, 4.0, 0.0)) + (jnp.where(b1, 2.0, 0.0) + jnp.where(b0, 1.0, 0.0))
    return jnp.where(v[15] > x, float(PEER_TOPK), lo)


def _column_sum(tiles):
    parts = list(tiles)
    while len(parts) > 1:
        parts = [parts[i] + parts[i + 1] for i in range(0, len(parts) - 1, 2)] + parts[len(parts) & ~1:]
    total = parts[0]
    for shift in (4, 2, 1):
        total = total + pltpu.roll(total, shift, 0)
    return total


def _route_head_fast(s1, s2, outs, h, cols):
    r2_ref, e2_ref, c1_ref, e1_ref = outs
    n_tiles = N_KEYS // SUBLANES
    t1 = [s1[i * SUBLANES:(i + 1) * SUBLANES] for i in range(n_tiles)]
    t2 = [s2[i * SUBLANES:(i + 1) * SUBLANES] for i in range(n_tiles)]
    v1 = _top16_values(t1)
    v2 = _top16_values(t2)
    sub = lax.broadcasted_iota(jnp.int32, t1[0].shape, 0)

    def by_sublane(vals):
        out = vals[0]
        for b in range(1, SUBLANES):
            out = jnp.where(sub == b, vals[b], out)
        return out

    v2_lo, v2_hi, v1_hi = by_sublane(v2[0:8]), by_sublane(v2[8:16]), by_sublane(v1[8:16])
    cand = [v1[0] + v2_lo, v1[0] + v2_hi] + [v1[a] + v2_lo for a in range(1, 8)] + [v1_hi + v2[0]]
    tau = _top16_values(cand)[PEER_TOPK - 1]
    chosen = [jnp.where(c >= tau, 1.0, 0.0) for c in cand]
    counts = [_column_sum(chosen[0:2])] + [_column_sum([chosen[1 + a]]) for a in range(1, 8)]
    counts += [jnp.broadcast_to(chosen[9][a:a + 1], sub.shape) for a in range(SUBLANES)]
    z = _column_sum([ch * jnp.exp(c - cand[0][0:1]) for ch, c in zip(chosen, cand)])
    inv_z = 1.0 / z

    rank2 = [_count_greater(v2, x) for x in t2]
    c1 = []
    for x in t1:
        c = jnp.zeros_like(x)
        for a in range(PEER_TOPK):
            c = jnp.where(x == v1[a], counts[a], c)
        c1.append(c)
    r2_ref[h, :, cols] = jnp.concatenate(rank2, axis=0).astype(BF)
    e2_ref[h, :, cols] = jnp.concatenate([jnp.exp(x - v2[0]) for x in t2], axis=0).astype(BF)
    c1_ref[h, :, cols] = jnp.concatenate(c1, axis=0)
    e1_ref[h, :, cols] = jnp.concatenate([jnp.exp(x - v1[0]) * inv_z for x in t1], axis=0)

    no_tie_rank_sum = float(sum(range(PEER_TOPK)) + (N_KEYS - PEER_TOPK) * PEER_TOPK)
    bad = jnp.where(_column_sum(rank2) != no_tie_rank_sum, 1.0, 0.0)
    bad = bad + jnp.where(_column_sum([jnp.where(x >= v1[PEER_TOPK - 1], 1.0, 0.0) for x in t1]) != PEER_TOPK, 1.0, 0.0)
    for a in range(PEER_TOPK - 1):
        bad = bad + jnp.where(v1[a] == v1[a + 1], 1.0, 0.0)
    return bad + jnp.where(_column_sum(chosen) != PEER_TOPK, 1.0, 0.0)


def _route_kernel(x_ref, wh_ref, wl_ref, kh_ref, kl_ref, fidx_ref, r2_ref, e2_ref, c1_ref, e1_ref):
    x = x_ref[...]
    t = x.shape[0]
    xh, xl = _split_bf16(x)
    q = _dot(xh, wh_ref[...]) + (_dot(xl, wh_ref[...]) + _dot(xh, wl_ref[...]))
    outs = (r2_ref, e2_ref, c1_ref, e1_ref)

    def scores(hs):
        qh, ql = _split_bf16(q[:, hs * PEER_HALF:(hs + 1) * PEER_HALF])
        return _dot_nt(kh_ref[hs], qh) + (_dot_nt(kl_ref[hs], qh) + _dot_nt(kh_ref[hs], ql))

    all_scores = [scores(hs) for hs in range(2 * PEER_HEADS)]
    bad = []
    for h in range(PEER_HEADS):
        flags = [_route_head_fast(all_scores[2 * h][:, c * LANES:(c + 1) * LANES],
                                  all_scores[2 * h + 1][:, c * LANES:(c + 1) * LANES], outs, h,
                                  slice(c * LANES, (c + 1) * LANES)) for c in range(t // LANES)]
        bad.append(sum(flags[1:], flags[0]))

    for h in range(PEER_HEADS):
        @pl.when(jnp.max(bad[h]) > 0.0)
        def _(h=h):
            _route_head_exact(all_scores[2 * h], all_scores[2 * h + 1], fidx_ref[...], outs, h)


def _route(x1, lw, tb):
    n = x1.shape[0]
    fidx = jnp.broadcast_to(jnp.asarray(_cand_flat_index())[:, None], (CAND_ROWS, tb))
    out = jax.ShapeDtypeStruct((PEER_HEADS, N_KEYS, n), F32)
    out_bf = jax.ShapeDtypeStruct((PEER_HEADS, N_KEYS, n), BF)
    ospec = pl.BlockSpec((PEER_HEADS, N_KEYS, tb), lambda i: (0, 0, i))
    full2 = lambda a: pl.BlockSpec(a.shape, lambda i: (0, 0))
    full3 = lambda a: pl.BlockSpec(a.shape, lambda i: (0, 0, 0))
    return pl.pallas_call(
        _route_kernel,
        out_shape=(out_bf, out_bf, out, out),
        grid=(n // tb,),
        in_specs=[pl.BlockSpec((tb, D_MODEL), lambda i: (i, 0)), full2(lw["wq_hi"]), full2(lw["wq_lo"]),
                  full3(lw["keys_hi"]), full3(lw["keys_lo"]), full2(fidx)],
        out_specs=(ospec, ospec, ospec, ospec),
        compiler_params=_params(("arbitrary",)),
        name="peer_route",
    )(x1, lw["wq_hi"], lw["wq_lo"], lw["keys_hi"], lw["keys_lo"], fidx)


PEER_EXPERT_BLOCK = 1024


def _peer_kernel(x_ref, u_ref, vt_ref, r2_ref, e2_ref, c1_ref, e1_ref, lg_ref, lb_ref, o_ref, xb, acc, wt, rowbuf, act):
    s = pl.program_id(1)
    last = pl.num_programs(1) - 1
    tb = x_ref.shape[0]

    @pl.when(s == 0)
    def _():
        xb[...] = x_ref[...].T.astype(BF)
        acc[...] = jnp.zeros_like(acc)
        act[1] = jnp.zeros(act.shape[1:], BF)

    keys_per_step = PEER_EXPERT_BLOCK // N_KEYS
    first_key = pl.multiple_of(jnp.maximum(s - 1, 0) * keys_per_step, keys_per_step)

    for h in range(PEER_HEADS):
        for src, ref in enumerate((c1_ref, e1_ref)):
            blk = ref[h, pl.ds(first_key, keys_per_step), :]
            for j in range(keys_per_step):
                rowbuf[src, h, j] = jnp.broadcast_to(blk[j:j + 1], (SUBLANES, tb)).astype(BF)

    half = N_KEYS // 2
    group = keys_per_step // 2
    tiles = half // SUBLANES

    def spread(row):
        return jnp.broadcast_to(row[None], (tiles, SUBLANES, BF_CHUNK)).reshape(half, BF_CHUNK)

    for tc in range(tb // BF_CHUNK):
        cols = slice(tc * BF_CHUNK, (tc + 1) * BF_CHUNK)
        for jg in range(keys_per_step // group):
            for kh in range(2):
                keys = slice(kh * half, (kh + 1) * half)
                w = [jnp.zeros((half, BF_CHUNK), BF) for _ in range(group)]
                for h in range(PEER_HEADS):
                    r2 = r2_ref[h, keys, cols]
                    e2 = e2_ref[h, keys, cols]
                    for jj in range(group):
                        j = jg * group + jj
                        c1 = spread(rowbuf[0, h, j, :, cols])
                        e1 = spread(rowbuf[1, h, j, :, cols])
                        w[jj] = w[jj] + jnp.where(r2 < c1, e2 * e1, 0)
                for jj in range(group):
                    rows = slice((jg * group + jj) * N_KEYS + kh * half, (jg * group + jj) * N_KEYS + (kh + 1) * half)
                    wt[rows, cols] = w[jj]
    wt[...] = wt[...] * act[(s + 1) % 2]
    acc[...] += _dot_tn(vt_ref[...], wt[...])
    act[s % 2] = jax.nn.gelu(_dot(u_ref[...], xb[...]).astype(BF))

    @pl.when(s == last)
    def _():
        r = DN_ALPHA * x_ref[...] + acc[...].T
        o_ref[...] = _layer_norm(r, lg_ref[...], lb_ref[...])


def _peer(x1, route, lw, tb):
    n = x1.shape[0]
    eb = PEER_EXPERT_BLOCK
    nblk = N_EXPERTS // eb
    rspec = pl.BlockSpec((PEER_HEADS, N_KEYS, tb), lambda i, e: (0, 0, i))
    vec = pl.BlockSpec((1, D_MODEL), lambda i, e: (0, 0))
    return pl.pallas_call(
        _peer_kernel,
        out_shape=jax.ShapeDtypeStruct((n, D_MODEL), F32),
        grid=(n // tb, nblk + 1),
        in_specs=[pl.BlockSpec((tb, D_MODEL), lambda i, s: (i, 0)),
                  pl.BlockSpec((eb, D_MODEL), lambda i, s: (jnp.minimum(s, nblk - 1), 0)),
                  pl.BlockSpec((eb, D_MODEL), lambda i, s: (jnp.maximum(s - 1, 0), 0)),
                  rspec, rspec, rspec, rspec, vec, vec],
        out_specs=pl.BlockSpec((tb, D_MODEL), lambda i, s: (i, 0)),
        scratch_shapes=[pltpu.VMEM((D_MODEL, tb), BF), pltpu.VMEM((D_MODEL, tb), F32), pltpu.VMEM((eb, tb), BF),
                        pltpu.VMEM((2, PEER_HEADS, eb // N_KEYS, SUBLANES, tb), BF),
                        pltpu.VMEM((2, eb, tb), BF)],
        compiler_params=_params(("arbitrary", "arbitrary")),
        name="peer_experts",
    )(x1, lw["peer_u"], lw["peer_vt"], *route, lw["ln2_g"], lw["ln2_b"])


def _ple_kernel(x_ref, pe_ref, wg_ref, wp_ref, lg_ref, lb_ref, o_ref):
    x = x_ref[...]
    ple = jax.nn.sigmoid(_dot(x.astype(BF), wg_ref[...])) * _dot(pe_ref[...].astype(BF), wp_ref[...])
    o_ref[...] = _layer_norm(DN_ALPHA * x + ple, lg_ref[...], lb_ref[...])


def _ple(x2, pe, lw, tm):
    n = x2.shape[0]
    full = lambda a: pl.BlockSpec(a.shape, lambda i: (0, 0))
    ws = (lw["ple_w_g"], lw["ple_w_p"], lw["ln3_g"], lw["ln3_b"])
    return pl.pallas_call(
        _ple_kernel,
        out_shape=jax.ShapeDtypeStruct((n, D_MODEL), F32),
        grid=(n // tm,),
        in_specs=[pl.BlockSpec((tm, D_MODEL), lambda i: (i, 0)), pl.BlockSpec((tm, PLE_DIM), lambda i: (i, 0))]
        + [full(a) for a in ws],
        out_specs=pl.BlockSpec((tm, D_MODEL), lambda i: (i, 0)),
        compiler_params=_params(("arbitrary",)),
        name="ple",
    )(x2, pe, *ws)


def _time_major(u, bsz, t, bp):
    u = jnp.transpose(u.reshape(bsz, t, -1), (1, 0, 2))
    return jnp.pad(u, ((0, 0), (0, bp - bsz), (0, 0))).reshape(t * bp, -1)


def _batch_major(y, bsz, t, bp):
    return jnp.transpose(y.reshape(t, bp, -1)[:, :bsz], (1, 0, 2)).reshape(bsz * t, -1)


def _layer(x, pe, h0_re, h0_im, cache_k, cache_v, s0, pos, lw, bsz, t, prompt, n_rows):
    n = bsz * t
    h = _in_projection(x, lw["w_in"], 512, IN_COLS // 2)

    bp = max(8, bsz)
    u_tb = _time_major(h[:, OFF_U_SSM:OFF_U_SSM + SSM_WIDTH], bsz, t, bp)
    h0 = jnp.concatenate([h0_re.reshape(bsz, SSM_FLAT), h0_im.reshape(bsz, SSM_FLAT)], axis=1).astype(F32)
    h0 = jnp.pad(h0, ((0, bp - bsz), (0, 0)))
    y_tb, h_last = _s5_mixer(u_tb, h0, lw["s5"], t, bp, CHUNK)
    y_ssm = _batch_major(y_tb, bsz, t, bp)
    new_re = h_last[:bsz, :SSM_FLAT].reshape(bsz, SSM_GROUPS, SSM_STATE)
    new_im = h_last[:bsz, SSM_FLAT:].reshape(bsz, SSM_GROUPS, SSM_STATE)

    k_att = h[:, OFF_K_ATT:OFF_K_ATT + ATT_WIDTH].reshape(bsz, t, ATT_HEADS, ATT_HEAD_DIM)
    v_att = h[:, OFF_V_ATT:OFF_V_ATT + ATT_WIDTH].reshape(bsz, t, ATT_HEADS, ATT_HEAD_DIM)
    if prompt:
        y_att = _attn_prompt(h, lw["rel_bias"], bsz, t)
        new_k, new_v = k_att[:, -n_rows:], v_att[:, -n_rows:]
    else:
        r = cache_k.shape[1]
        y_att = _attn_sample(h, cache_k.reshape(bsz, r, ATT_WIDTH), cache_v.reshape(bsz, r, ATT_WIDTH),
                             lw["rel_bias"], bsz)
        new_k, new_v = k_att, v_att

    y_ret, s_pair = _retention(h, _pair_states(s0), _retention_tables(pos), lw["ret_gn_g"], bsz, t,
                               4 if prompt else 1)
    s_fin = _unpair_states(s_pair)

    tok = 512 if n % 512 == 0 else 256
    x1 = _merge(x, h, y_ssm, y_att, y_ret, lw, tok)
    route = _route(x1, lw, 256)
    x2 = _peer(x1, route, lw, tok)
    x3 = _ple(x2, pe, lw, tok)
    return x3, (new_re, new_im, new_k, new_v, s_fin)


def _layer_weights(li, w_in, ssm_lam_re, ssm_lam_im, ssm_b_re, ssm_b_im, ssm_c_re, ssm_c_im, ssm_log_dt, ssm_d,
                   ssm_w_glu, ssm_b_glu, att_rel_bias, ret_gn_g, w_br_ssm, w_br_att, w_br_ret, w_o, ln1_g, ln1_b,
                   peer_w_q, peer_sub_keys, peer_u, peer_v, ln2_g, ln2_b, ple_w_g, ple_w_p, ln3_g, ln3_b):
    vec = lambda a: a[li].astype(F32).reshape(1, -1)
    w = w_in[li]
    wq_hi, wq_lo = _split_bf16(peer_w_q[li].astype(F32))
    keys_hi, keys_lo = _split_bf16(peer_sub_keys[li].astype(F32).reshape(2 * PEER_HEADS, N_KEYS, PEER_HALF))
    return {
        "w_in": jnp.concatenate([w[:, OLD_GATE_START:], w[:, :OLD_GATE_START]], axis=1).astype(BF),
        "s5": _s5_tables(ssm_lam_re[li], ssm_lam_im[li], ssm_b_re[li], ssm_b_im[li], ssm_c_re[li], ssm_c_im[li],
                         ssm_log_dt[li], ssm_d[li], ssm_w_glu[li], ssm_b_glu[li]),
        "rel_bias": _rel_bias_tile(att_rel_bias[li]),
        "ret_gn_g": vec(ret_gn_g),
        "w_br_ssm": w_br_ssm[li].astype(BF), "w_br_att": w_br_att[li].astype(BF), "w_br_ret": w_br_ret[li].astype(BF),
        "w_o": w_o[li].astype(BF), "ln1_g": vec(ln1_g), "ln1_b": vec(ln1_b),
        "wq_hi": wq_hi, "wq_lo": wq_lo, "keys_hi": keys_hi, "keys_lo": keys_lo,
        "peer_u": peer_u[li].astype(BF), "peer_vt": peer_v[li].astype(BF),
        "ln2_g": vec(ln2_g), "ln2_b": vec(ln2_b),
        "ple_w_g": ple_w_g[li].astype(BF), "ple_w_p": ple_w_p[li].astype(BF), "ln3_g": vec(ln3_g), "ln3_b": vec(ln3_b),
    }


def kernel(x_prompt, x_sample, p_prompt, p_sample, state_ssm_re, state_ssm_im, cache_attn_k, cache_attn_v, state_ret, w_in, ssm_lam_re, ssm_lam_im, ssm_b_re, ssm_b_im, ssm_c_re, ssm_c_im, ssm_log_dt, ssm_d, ssm_w_glu, ssm_b_glu, att_rel_bias, ret_gn_g, w_br_ssm, w_br_att, w_br_ret, w_o, ln1_g, ln1_b, peer_w_q, peer_sub_keys, peer_u, peer_v, ln2_g, ln2_b, ple_w_g, ple_w_p, ln3_g, ln3_b):
    bp, tp, _ = x_prompt.shape
    bs, ts, _ = x_sample.shape
    n_rows = cache_attn_k.shape[2]
    pos_p = np.arange(tp)
    pos_s = PAST_LEN + np.arange(ts)
    y_p = x_prompt.reshape(bp * tp, D_MODEL).astype(F32)
    y_s = x_sample.reshape(bs * ts, D_MODEL).astype(F32)
    new_p, new_s = [], []
    for li in range(DEPTH):
        lw = _layer_weights(li, w_in, ssm_lam_re, ssm_lam_im, ssm_b_re, ssm_b_im, ssm_c_re, ssm_c_im, ssm_log_dt,
                            ssm_d, ssm_w_glu, ssm_b_glu, att_rel_bias, ret_gn_g, w_br_ssm, w_br_att, w_br_ret, w_o,
                            ln1_g, ln1_b, peer_w_q, peer_sub_keys, peer_u, peer_v, ln2_g, ln2_b, ple_w_g, ple_w_p,
                            ln3_g, ln3_b)
        zero_ssm = jnp.zeros((bp, SSM_GROUPS, SSM_STATE), F32)
        zero_ret = jnp.zeros((bp, RET_HEADS, RET_DIM, RET_DIM), F32)
        y_p, st = _layer(y_p, p_prompt[li].reshape(bp * tp, PLE_DIM), zero_ssm, zero_ssm, None, None, zero_ret,
                         pos_p, lw, bp, tp, True, n_rows)
        new_p.append(st)
        y_s, st = _layer(y_s, p_sample[li].reshape(bs * ts, PLE_DIM), state_ssm_re[li], state_ssm_im[li],
                         cache_attn_k[li], cache_attn_v[li], state_ret[li], pos_s, lw, bs, ts, False, n_rows)
        new_s.append(st)
    stack = lambda states, k: jnp.stack([s[k] for s in states])
    return (y_p.reshape(bp, tp, D_MODEL), y_s.reshape(bs, ts, D_MODEL),
            stack(new_p, 0), stack(new_p, 1), stack(new_p, 2), stack(new_p, 3), stack(new_p, 4),
            stack(new_s, 0), stack(new_s, 1), stack(new_s, 2), stack(new_s, 3), stack(new_s, 4))
```

```python
import functools

import jax
import jax.numpy as jnp
import numpy as np
from jax import lax
from jax.experimental import pallas as pl
from jax.experimental.pallas import tpu as pltpu

F32 = jnp.float32
BF = jnp.bfloat16

D_MODEL = 1024
DEPTH = 2
PAST_LEN = 2048
CHUNK = 64
PLE_DIM = 256
SSM_WIDTH = 256
SSM_GROUP = 16
SSM_GROUPS = 16
SSM_STATE = 64
SSM_FLAT = SSM_GROUPS * SSM_STATE
ATT_HEADS = 8
ATT_HEAD_DIM = 64
ATT_WIDTH = 512
ATT_PAST = 8 * CHUNK
ATT_BAND = ATT_PAST + CHUNK
REL_CLIP = 256
RET_HEADS = 4
RET_DIM = 64
ROPE_BASE = 10000.0
PEER_HEADS = 8
PEER_HALF = 128
N_KEYS = 128
N_EXPERTS = N_KEYS * N_KEYS
PEER_TOPK = 16
SUBLANES = 8
BF_CHUNK = 256
DN_ALPHA = (2 * DEPTH) ** 0.25
LN_EPS = 1e-5
LANES = 128
NEG_MASK = float(jnp.finfo(jnp.float32).min)

IN_COLS = 5888
OFF_GATES = 0
OFF_U_SSM = 3072
OFF_Q_ATT = 3328
OFF_K_ATT = 3840
OFF_V_ATT = 4352
OFF_Q_RET = 4864
OFF_K_RET = 5120
OFF_V_RET = 5376
OFF_G_RET = 5632
OLD_GATE_START = 2816

VMEM_LIMIT = 48 * 1024 * 1024


def _params(sem):
    return pltpu.CompilerParams(dimension_semantics=sem, vmem_limit_bytes=VMEM_LIMIT)


def _dot(a, b):
    return jnp.dot(a, b, preferred_element_type=F32)


def _dot_nt(a, b):
    return lax.dot_general(a, b, (((1,), (1,)), ((), ())), preferred_element_type=F32)


def _dot_tn(a, b):
    return lax.dot_general(a, b, (((0,), (0,)), ((), ())), preferred_element_type=F32)


def _layer_norm(r, g, b):
    mu = jnp.mean(r, axis=-1, keepdims=True)
    d = r - mu
    var = jnp.mean(d * d, axis=-1, keepdims=True)
    return d * lax.rsqrt(var + LN_EPS) * g + b


def _matmul_kernel(x_ref, w_ref, o_ref):
    o_ref[...] = _dot(x_ref[...].astype(BF), w_ref[...])


def _in_projection(x, w, tm, tn):
    m, k = x.shape
    n = w.shape[1]
    return pl.pallas_call(
        _matmul_kernel,
        out_shape=jax.ShapeDtypeStruct((m, n), F32),
        grid=(n // tn, m // tm),
        in_specs=[pl.BlockSpec((tm, k), lambda j, i: (i, 0)),
                  pl.BlockSpec((k, tn), lambda j, i: (0, j))],
        out_specs=pl.BlockSpec((tm, tn), lambda j, i: (i, j)),
        compiler_params=_params(("arbitrary", "arbitrary")),
        name="in_projection",
    )(x, w)


def _s5_kernel(u_ref, bm_ref, a_ref, cm_ref, d_ref, wg_ref, bg_ref, h0_ref, y_ref, hl_ref, hbuf, hst, *, steps, bp):
    @pl.when(pl.program_id(0) == 0)
    def _():
        hst[...] = h0_ref[...]

    u = u_ref[...]
    hbuf[...] = _dot(u.astype(BF), bm_ref[...])
    a_re = jnp.broadcast_to(a_ref[0:1, :], (bp, SSM_FLAT))
    a_im = jnp.broadcast_to(a_ref[1:2, :], (bp, SSM_FLAT))

    def step(t, carry):
        h_re, h_im = carry
        r0 = pl.multiple_of(t * bp, bp)
        n_re = a_re * h_re - a_im * h_im + hbuf[pl.ds(r0, bp), 0:SSM_FLAT]
        n_im = a_re * h_im + a_im * h_re + hbuf[pl.ds(r0, bp), SSM_FLAT:2 * SSM_FLAT]
        hbuf[pl.ds(r0, bp), 0:SSM_FLAT] = n_re
        hbuf[pl.ds(r0, bp), SSM_FLAT:2 * SSM_FLAT] = n_im
        return n_re, n_im

    h_re, h_im = lax.fori_loop(0, steps, step, (hst[:, 0:SSM_FLAT], hst[:, SSM_FLAT:2 * SSM_FLAT]))
    hst[:, 0:SSM_FLAT] = h_re
    hst[:, SSM_FLAT:2 * SSM_FLAT] = h_im
    hl_ref[...] = hst[...]
    y = _dot(hbuf[...].astype(BF), cm_ref[...]) + d_ref[...] * u
    z = jax.nn.gelu(y)
    y_ref[...] = z * jax.nn.sigmoid(_dot(z.astype(BF), wg_ref[...]) + bg_ref[...])


def _s5_mixer(u_tb, h0, sp, n_steps, bp, steps):
    rows = steps * bp
    kern = functools.partial(_s5_kernel, steps=steps, bp=bp)
    const = lambda c: (0, 0)
    return pl.pallas_call(
        kern,
        out_shape=(jax.ShapeDtypeStruct((n_steps * bp, SSM_WIDTH), F32),
                   jax.ShapeDtypeStruct((bp, 2 * SSM_FLAT), F32)),
        grid=(n_steps // steps,),
        in_specs=[pl.BlockSpec((rows, SSM_WIDTH), lambda c: (c, 0)),
                  pl.BlockSpec((SSM_WIDTH, 2 * SSM_FLAT), const),
                  pl.BlockSpec((2, SSM_FLAT), const),
                  pl.BlockSpec((2 * SSM_FLAT, SSM_WIDTH), const),
                  pl.BlockSpec((1, SSM_WIDTH), const),
                  pl.BlockSpec((SSM_WIDTH, SSM_WIDTH), const),
                  pl.BlockSpec((1, SSM_WIDTH), const),
                  pl.BlockSpec((bp, 2 * SSM_FLAT), const)],
        out_specs=(pl.BlockSpec((rows, SSM_WIDTH), lambda c: (c, 0)),
                   pl.BlockSpec((bp, 2 * SSM_FLAT), const)),
        scratch_shapes=[pltpu.VMEM((rows, 2 * SSM_FLAT), F32), pltpu.VMEM((bp, 2 * SSM_FLAT), F32)],
        compiler_params=_params(("arbitrary",)),
        name="s5_mixer",
    )(u_tb, sp["bmat"], sp["a"], sp["cmat"], sp["d"], sp["w_glu"], sp["b_glu"], h0)


def _s5_tables(lam_re, lam_im, b_re, b_im, c_re, c_im, log_dt, d_skip, w_glu, b_glu):
    lr, li = lam_re.astype(F32), lam_im.astype(F32)
    dt = jnp.exp(log_dt.astype(F32))[:, None]
    mag = jnp.exp(lr * dt)
    a_re, a_im = mag * jnp.cos(li * dt), mag * jnp.sin(li * dt)
    den = lr * lr + li * li
    k_re = ((a_re - 1.0) * lr + a_im * li) / den
    k_im = (a_im * lr - (a_re - 1.0) * li) / den
    br, bi = b_re.astype(F32), b_im.astype(F32)
    bbar_re = k_re[..., None] * br - k_im[..., None] * bi
    bbar_im = k_re[..., None] * bi + k_im[..., None] * br
    eye = jnp.eye(SSM_GROUPS, dtype=F32)

    def in_blockdiag(m):
        m = jnp.transpose(m, (0, 2, 1))
        return (m[:, :, None, :] * eye[:, None, :, None]).reshape(SSM_WIDTH, SSM_FLAT)

    def out_blockdiag(m):
        m = jnp.transpose(m, (0, 2, 1))
        return (m[:, :, None, :] * eye[:, None, :, None]).reshape(SSM_FLAT, SSM_WIDTH)

    bmat = jnp.concatenate([in_blockdiag(bbar_re), in_blockdiag(bbar_im)], axis=1)
    cmat = jnp.concatenate([out_blockdiag(c_re.astype(F32)), -out_blockdiag(c_im.astype(F32))], axis=0)
    a = jnp.stack([a_re.reshape(SSM_FLAT), a_im.reshape(SSM_FLAT)])
    return {"bmat": bmat.astype(BF), "cmat": cmat.astype(BF), "a": a,
            "d": d_skip.astype(F32).reshape(1, SSM_WIDTH), "w_glu": w_glu.astype(BF),
            "b_glu": b_glu.astype(F32).reshape(1, SSM_WIDTH)}


ATT_QROWS = 4 * CHUNK
ATT_WINDOW = ATT_PAST + ATT_QROWS


def _attn_prompt_kernel(q_ref, k0_ref, k1_ref, k2_ref, v0_ref, v1_ref, v2_ref, bias_ref, o_ref):
    tb = pl.program_id(2)
    kwin = jnp.concatenate([k0_ref[...], k1_ref[...], k2_ref[...]], axis=0).astype(BF)
    vwin = jnp.concatenate([v0_ref[...], v1_ref[...], v2_ref[...]], axis=0).astype(BF)
    first = lax.broadcasted_iota(jnp.int32, (1, LANES), 1) < ATT_HEAD_DIM
    q = q_ref[...] * (ATT_HEAD_DIM ** -0.5)
    q2 = jnp.concatenate([jnp.where(first, q, 0.0), jnp.where(first, 0.0, q)], axis=0).astype(BF)
    s = _dot_nt(q2, kwin) + jnp.concatenate([bias_ref[0], bias_ref[1]], axis=0)
    col = lax.broadcasted_iota(jnp.int32, (1, ATT_WINDOW), 1)
    s = jnp.where(col + (tb * ATT_QROWS - ATT_PAST) >= 0, s, NEG_MASK)
    p = jnp.exp(s - jnp.max(s, axis=-1, keepdims=True))
    o = _dot(p.astype(BF), vwin) / jnp.sum(p, axis=-1, keepdims=True)
    o_ref[...] = jnp.where(first, o[:ATT_QROWS], o[ATT_QROWS:])


def _attn_prompt(h, bias, bsz, t):
    rows = ATT_QROWS
    tbn = t // rows
    qc, kc, vc = OFF_Q_ATT // LANES, OFF_K_ATT // LANES, OFF_V_ATT // LANES
    band = jnp.concatenate(
        [jnp.pad(bias, ((0, 0), (0, 0), (c * CHUNK, ATT_WINDOW - ATT_BAND - c * CHUNK)), constant_values=NEG_MASK)
         for c in range(rows // CHUNK)], axis=1)

    def past(col0, back):
        return pl.BlockSpec((rows, LANES), lambda b, hp, tb: (b * tbn + jnp.maximum(tb - back, 0), col0 + hp))

    return pl.pallas_call(
        _attn_prompt_kernel,
        out_shape=jax.ShapeDtypeStruct((bsz * t, ATT_WIDTH), F32),
        grid=(bsz, ATT_HEADS // 2, tbn),
        in_specs=[past(qc, 0), past(kc, 2), past(kc, 1), past(kc, 0), past(vc, 2), past(vc, 1), past(vc, 0),
                  pl.BlockSpec((2, rows, ATT_WINDOW), lambda b, hp, tb: (hp, 0, 0))],
        out_specs=pl.BlockSpec((rows, LANES), lambda b, hp, tb: (b * tbn + tb, hp)),
        compiler_params=_params(("arbitrary", "arbitrary", "arbitrary")),
        name="attn_prompt",
    )(h, h, h, h, h, h, h, band)


def _attn_sample_kernel(q_ref, k_ref, v_ref, ck_ref, cv_ref, bias_ref, o_ref):
    kwin = jnp.concatenate([ck_ref[0], k_ref[...]], axis=0).astype(BF)
    vwin = jnp.concatenate([cv_ref[0], v_ref[...]], axis=0).astype(BF)
    first = lax.broadcasted_iota(jnp.int32, (1, LANES), 1) < ATT_HEAD_DIM
    q = q_ref[...] * (ATT_HEAD_DIM ** -0.5)
    q2 = jnp.concatenate([jnp.where(first, q, 0.0), jnp.where(first, 0.0, q)], axis=0).astype(BF)
    s = _dot_nt(q2, kwin) + jnp.concatenate([bias_ref[0], bias_ref[1]], axis=0)
    p = jnp.exp(s - jnp.max(s, axis=-1, keepdims=True))
    o = _dot(p.astype(BF), vwin) / jnp.sum(p, axis=-1, keepdims=True)
    o_ref[...] = jnp.where(first, o[:CHUNK], o[CHUNK:])


def _attn_sample(h, cache_k, cache_v, bias, bsz):
    qc, kc, vc = OFF_Q_ATT // LANES, OFF_K_ATT // LANES, OFF_V_ATT // LANES
    new = lambda col0: pl.BlockSpec((CHUNK, LANES), lambda b, hp: (b, col0 + hp))
    old = pl.BlockSpec((1, ATT_PAST, LANES), lambda b, hp: (b, 0, hp))
    return pl.pallas_call(
        _attn_sample_kernel,
        out_shape=jax.ShapeDtypeStruct((bsz * CHUNK, ATT_WIDTH), F32),
        grid=(bsz, ATT_HEADS // 2),
        in_specs=[new(qc), new(kc), new(vc), old, old,
                  pl.BlockSpec((2, CHUNK, ATT_BAND), lambda b, hp: (hp, 0, 0))],
        out_specs=pl.BlockSpec((CHUNK, LANES), lambda b, hp: (b, hp)),
        compiler_params=_params(("arbitrary", "arbitrary")),
        name="attn_sample",
    )(h, h, h, cache_k, cache_v, bias)


def _rel_bias_tile(table):
    dist = jnp.arange(-(CHUNK - 1), ATT_BAND)
    per_dist = table[jnp.clip(dist, -REL_CLIP, REL_CLIP) + REL_CLIP].astype(F32).T
    n = dist.shape[0]
    rev = jnp.roll(per_dist[:, ::-1], -(CHUNK - 1), axis=1)
    skew = jnp.tile(rev, (1, CHUNK))[:, :CHUNK * (n - 1)].reshape(ATT_HEADS, CHUNK, n - 1)
    return skew[:, :, :ATT_BAND]


def _ret_kernel(q_ref, k_ref, v_ref, g_ref, cos_ref, sin_ref, intra_ref, qw_ref, kw_ref, dec_ref, bd_ref, gn_ref,
                s0_ref, y_ref, sf_ref, state, *, cpb):
    @pl.when(pl.program_id(1) == 0)
    def _():
        state[...] = s0_ref[0]

    lane = lax.broadcasted_iota(jnp.int32, (1, LANES), 1)
    first = lane < RET_DIM
    low_half = (lane % RET_DIM) < (RET_DIM // 2)

    for c in range(cpb):
        rows = slice(c * CHUNK, (c + 1) * CHUNK)
        cos = cos_ref[rows, :]
        sin = sin_ref[rows, :]

        def rope(x):
            swapped = jnp.where(low_half, pltpu.roll(x, LANES - RET_DIM // 2, 1), pltpu.roll(x, RET_DIM // 2, 1))
            return x * cos + swapped * sin

        for hp in range(RET_HEADS // 2):
            cols = slice(hp * LANES, (hp + 1) * LANES)
            q = rope(q_ref[rows, cols]) * (RET_DIM ** -0.5)
            k = rope(k_ref[rows, cols])
            kb = k.astype(BF)
            vb = v_ref[rows, cols].astype(BF)
            intra = []
            for hh in range(2):
                qm = jnp.where(first if hh == 0 else jnp.logical_not(first), q, 0.0).astype(BF)
                s = _dot_nt(qm, kb) * intra_ref[2 * hp + hh]
                intra.append(_dot(s.astype(BF), vb))
            s_prev = state[hp]
            o = jnp.where(first, intra[0], intra[1]) + _dot((q * qw_ref[hp]).astype(BF), s_prev.astype(BF))
            kv = _dot_tn((k * kw_ref[hp]).astype(BF), vb)
            state[hp] = dec_ref[hp] * s_prev + bd_ref[...] * kv
            sum_a = jnp.sum(jnp.where(first, o, 0.0), axis=-1, keepdims=True)
            sum_b = jnp.sum(o, axis=-1, keepdims=True) - sum_a
            d = o - jnp.where(first, sum_a, sum_b) * (1.0 / RET_DIM)
            sq = d * d
            var_a = jnp.sum(jnp.where(first, sq, 0.0), axis=-1, keepdims=True)
            var_b = jnp.sum(sq, axis=-1, keepdims=True) - var_a
            var = jnp.where(first, var_a, var_b) * (1.0 / RET_DIM)
            g = g_ref[rows, cols]
            y_ref[rows, cols] = d * lax.rsqrt(var + LN_EPS) * gn_ref[:, cols] * (g * jax.nn.sigmoid(g))
    sf_ref[0] = state[...]


def _retention(h, s0, rt, gn, bsz, t, cpb):
    rows = cpb * CHUNK
    tbn = t // rows
    width = RET_HEADS * RET_DIM
    col = lambda off: pl.BlockSpec((rows, width), lambda b, tb: (b * tbn + tb, off // width))
    tab = pl.BlockSpec((rows, LANES), lambda b, tb: (tb, 0))
    const3 = lambda shape: pl.BlockSpec(shape, lambda b, tb: (0, 0, 0))
    st = pl.BlockSpec((1, 2, LANES, LANES), lambda b, tb: (b, 0, 0, 0))
    return pl.pallas_call(
        functools.partial(_ret_kernel, cpb=cpb),
        out_shape=(jax.ShapeDtypeStruct((bsz * t, width), F32),
                   jax.ShapeDtypeStruct((bsz, 2, LANES, LANES), F32)),
        grid=(bsz, tbn),
        in_specs=[col(OFF_Q_RET), col(OFF_K_RET), col(OFF_V_RET), col(OFF_G_RET), tab, tab,
                  const3((RET_HEADS, CHUNK, CHUNK)), const3((2, CHUNK, LANES)), const3((2, CHUNK, LANES)),
                  const3((2, LANES, LANES)), pl.BlockSpec((LANES, LANES), lambda b, tb: (0, 0)),
                  pl.BlockSpec((1, width), lambda b, tb: (0, 0)), st],
        out_specs=(pl.BlockSpec((rows, width), lambda b, tb: (b * tbn + tb, 0)), st),
        scratch_shapes=[pltpu.VMEM((2, LANES, LANES), F32)],
        compiler_params=_params(("arbitrary", "arbitrary")),
        name="retention",
    )(h, h, h, h, rt["cos"], rt["sin"], rt["intra"], rt["qw"], rt["kw"], rt["dec"], rt["bd"], gn, s0)


def _retention_tables(pos):
    f32 = np.float32
    lg = np.log1p(-(f32(2.0) ** (f32(-5.0) - np.arange(RET_HEADS, dtype=f32)))).astype(f32)
    i = np.arange(CHUNK, dtype=f32)
    intra = np.exp(lg[:, None, None] * np.abs(i[:, None] - i[None, :])).astype(f32)
    lane_head = np.arange(2 * LANES) // RET_DIM
    lg_lane = lg[lane_head]
    qw = np.exp(lg_lane[None, :] * (i[:, None] + f32(1.0))).astype(f32)
    kw = np.exp(lg_lane[None, :] * (f32(CHUNK - 1.0) - i)[:, None]).astype(f32)
    pair = lambda m: np.stack([m[:, :LANES], m[:, LANES:]])
    same_head = (np.arange(LANES)[:, None] // RET_DIM) == (np.arange(LANES)[None, :] // RET_DIM)
    bd = same_head.astype(f32)
    dec_lane = np.exp(lg_lane * f32(CHUNK)).astype(f32)
    dec = np.stack([bd * dec_lane[:LANES][:, None], bd * dec_lane[LANES:][:, None]])
    half = RET_DIM // 2
    freqs = (f32(ROPE_BASE) ** (-np.arange(half, dtype=f32) / f32(half))).astype(f32)
    ang = (np.asarray(pos).astype(f32)[:, None] * freqs[None]).astype(f32)
    cos = np.tile(np.cos(ang).astype(f32), (1, LANES // half))
    sin = np.tile(np.concatenate([-np.sin(ang), np.sin(ang)], axis=1).astype(f32), (1, LANES // RET_DIM))
    return {"intra": intra, "qw": pair(qw), "kw": pair(kw), "dec": dec, "bd": bd, "cos": cos, "sin": sin}


def _pair_states(s):
    b = s.shape[0]
    s = s.astype(F32).reshape(b, 2, 2, RET_DIM, RET_DIM)
    out = jnp.zeros((b, 2, 2, RET_DIM, 2, RET_DIM), F32)
    out = out.at[:, :, 0, :, 0, :].set(s[:, :, 0]).at[:, :, 1, :, 1, :].set(s[:, :, 1])
    return out.reshape(b, 2, LANES, LANES)


def _unpair_states(s):
    b = s.shape[0]
    s = s.reshape(b, 2, 2, RET_DIM, 2, RET_DIM)
    return jnp.stack([s[:, :, 0, :, 0, :], s[:, :, 1, :, 1, :]], axis=2).reshape(b, RET_HEADS, RET_DIM, RET_DIM)


def _merge_kernel(x_ref, g1_ref, g2_ref, g3_ref, ys_ref, ya_ref, yr_ref, ws_ref, wa_ref, wr_ref, wo_ref,
                  lg_ref, lb_ref, o_ref):
    merged = (jax.nn.sigmoid(g1_ref[...]) * _dot(ys_ref[...].astype(BF), ws_ref[...])
              + jax.nn.sigmoid(g2_ref[...]) * _dot(ya_ref[...].astype(BF), wa_ref[...])
              + jax.nn.sigmoid(g3_ref[...]) * _dot(yr_ref[...].astype(BF), wr_ref[...]))
    r = DN_ALPHA * x_ref[...] + _dot(merged.astype(BF), wo_ref[...])
    o_ref[...] = _layer_norm(r, lg_ref[...], lb_ref[...])


def _merge(x, h, y_ssm, y_att, y_ret, lw, tm):
    n = x.shape[0]
    row = lambda w: pl.BlockSpec((tm, w), lambda i: (i, 0))
    gate = lambda j: pl.BlockSpec((tm, D_MODEL), lambda i: (i, j))
    full = lambda a: pl.BlockSpec(a.shape, lambda i: (0, 0))
    ws = (lw["w_br_ssm"], lw["w_br_att"], lw["w_br_ret"], lw["w_o"], lw["ln1_g"], lw["ln1_b"])
    return pl.pallas_call(
        _merge_kernel,
        out_shape=jax.ShapeDtypeStruct((n, D_MODEL), F32),
        grid=(n // tm,),
        in_specs=[row(D_MODEL), gate(0), gate(1), gate(2), row(SSM_WIDTH), row(ATT_WIDTH),
                  row(RET_HEADS * RET_DIM)] + [full(a) for a in ws],
        out_specs=row(D_MODEL),
        compiler_params=_params(("arbitrary",)),
        name="merge",
    )(x, h, h, h, y_ssm, y_att, y_ret, *ws)


CAND_ROWS = 80


def _cand_flat_index():
    f = [b for b in range(16)]
    for a in range(1, 8):
        f += [a * 16 + b for b in range(8)]
    f += [a * 16 for a in range(8, 16)]
    return np.asarray(f, np.float32)


def _split_bf16(x):
    hi = x.astype(BF)
    return hi, (x - hi.astype(F32)).astype(BF)


def _take16(work, order):
    t = work.shape[1]
    row16 = lax.broadcasted_iota(jnp.int32, (PEER_TOPK, t), 0)
    rank = jnp.full(work.shape, float(PEER_TOPK), F32)
    vals = jnp.zeros((PEER_TOPK, t), F32)
    for r in range(PEER_TOPK):
        m = jnp.max(work, axis=0, keepdims=True)
        first = jnp.min(jnp.where(work == m, order, 1e9), axis=0, keepdims=True)
        sel = order == first
        rank = jnp.where(sel, float(r), rank)
        work = jnp.where(sel, -jnp.inf, work)
        vals = jnp.where(row16 == r, m, vals)
    return rank, vals


def _route_head_exact(s1, s2, fidx, outs, h):
    r2_ref, e2_ref, c1_ref, e1_ref = outs
    key_idx = lax.broadcasted_iota(jnp.int32, s1.shape, 0).astype(F32)
    rank1, v1 = _take16(s1, key_idx)
    rank2, v2 = _take16(s2, key_idx)
    blocks = [v1[0:1] + v2]
    for a in range(1, 8):
        blocks.append(v1[a:a + 1] + v2[0:8])
    blocks.append(v1[8:16] + v2[0:1])
    cand = jnp.concatenate(blocks, axis=0)
    rank_c, _ = _take16(cand, fidx)
    chosen = jnp.where(rank_c < float(PEER_TOPK), 1.0, 0.0)
    counts = [jnp.sum(chosen[0:16], axis=0, keepdims=True)]
    for a in range(1, 8):
        counts.append(jnp.sum(chosen[8 + 8 * a:16 + 8 * a], axis=0, keepdims=True))
    for a in range(8, 16):
        counts.append(chosen[64 + a:65 + a])
    c1 = jnp.zeros(s1.shape, F32)
    for a in range(PEER_TOPK):
        c1 = jnp.where(rank1 == float(a), counts[a], c1)
    z = jnp.sum(chosen * jnp.exp(cand - cand[0:1]), axis=0, keepdims=True)
    r2_ref[h] = rank2.astype(BF)
    e2_ref[h] = jnp.exp(s2 - v2[0:1]).astype(BF)
    c1_ref[h] = c1
    e1_ref[h] = jnp.exp(s1 - v1[0:1]) / z


def _batcher_pairs(n):
    pairs = []

    def merge(lo, m, r):
        step = r * 2
        if step < m:
            merge(lo, m, step)
            merge(lo + r, m, step)
            pairs.extend((i, i + r) for i in range(lo + r, lo + m - r, step))
        else:
            pairs.append((lo, lo + r))

    def sort(lo, m):
        if m > 1:
            sort(lo, m // 2)
            sort(lo + m // 2, m // 2)
            merge(lo, m, 1)

    sort(0, n)
    return pairs


SORT16_PAIRS = _batcher_pairs(PEER_TOPK)


def _exchange(x, i, j):
    a, b = x[i], x[j]
    if b is None:
        return
    if a is None:
        x[i], x[j] = b, None
    else:
        x[i], x[j] = jnp.maximum(a, b), jnp.minimum(a, b)


def _top16_values(tiles):
    x = list(tiles) + [None] * (PEER_TOPK - len(tiles))
    for i, j in SORT16_PAIRS:
        _exchange(x, i, j)
    for shift in (4, 2, 1):
        y = [None if v is None else pltpu.roll(v, shift, 0) for v in x]
        z = []
        for i in range(PEER_TOPK):
            a, b = x[i], y[PEER_TOPK - 1 - i]
            z.append(b if a is None else a if b is None else jnp.maximum(a, b))
        for stride in (8, 4, 2, 1):
            for i in range(PEER_TOPK):
                if i & stride == 0:
                    _exchange(z, i, i + stride)
        x = z
    return x


def _count_greater(v, x):
    b3 = v[7] > x
    b2 = jnp.where(b3, v[11], v[3]) > x
    b1 = jnp.where(b3, jnp.where(b2, v[13], v[9]), jnp.where(b2, v[5], v[1])) > x
    b0 = jnp.where(b3, jnp.where(b2, jnp.where(b1, v[14], v[12]), jnp.where(b1, v[10], v[8])),
                   jnp.where(b2, jnp.where(b1, v[6], v[4]), jnp.where(b1, v[2], v[0]))) > x
    lo = (jnp.where(b3, 8.0, 0.0) + jnp.where(b2, 4.0, 0.0)) + (jnp.where(b1, 2.0, 0.0) + jnp.where(b0, 1.0, 0.0))
    return jnp.where(v[15] > x, float(PEER_TOPK), lo)


def _column_sum(tiles):
    parts = list(tiles)
    while len(parts) > 1:
        parts = [parts[i] + parts[i + 1] for i in range(0, len(parts) - 1, 2)] + parts[len(parts) & ~1:]
    total = parts[0]
    for shift in (4, 2, 1):
        total = total + pltpu.roll(total, shift, 0)
    return total


def _route_head_fast(s1, s2, outs, h, cols):
    r2_ref, e2_ref, c1_ref, e1_ref = outs
    n_tiles = N_KEYS // SUBLANES
    t1 = [s1[i * SUBLANES:(i + 1) * SUBLANES] for i in range(n_tiles)]
    t2 = [s2[i * SUBLANES:(i + 1) * SUBLANES] for i in range(n_tiles)]
    v1 = _top16_values(t1)
    v2 = _top16_values(t2)
    sub = lax.broadcasted_iota(jnp.int32, t1[0].shape, 0)

    def by_sublane(vals):
        out = vals[0]
        for b in range(1, SUBLANES):
            out = jnp.where(sub == b, vals[b], out)
        return out

    v2_lo, v2_hi, v1_hi = by_sublane(v2[0:8]), by_sublane(v2[8:16]), by_sublane(v1[8:16])
    cand = [v1[0] + v2_lo, v1[0] + v2_hi] + [v1[a] + v2_lo for a in range(1, 8)] + [v1_hi + v2[0]]
    tau = _top16_values(cand)[PEER_TOPK - 1]
    chosen = [jnp.where(c >= tau, 1.0, 0.0) for c in cand]
    counts = [_column_sum(chosen[0:2])] + [_column_sum([chosen[1 + a]]) for a in range(1, 8)]
    counts += [jnp.broadcast_to(chosen[9][a:a + 1], sub.shape) for a in range(SUBLANES)]
    z = _column_sum([ch * jnp.exp(c - cand[0][0:1]) for ch, c in zip(chosen, cand)])
    inv_z = 1.0 / z

    rank2 = [_count_greater(v2, x) for x in t2]
    c1 = []
    for x in t1:
        c = jnp.zeros_like(x)
        for a in range(PEER_TOPK):
            c = jnp.where(x == v1[a], counts[a], c)
        c1.append(c)
    r2_ref[h, :, cols] = jnp.concatenate(rank2, axis=0).astype(BF)
    e2_ref[h, :, cols] = jnp.concatenate([jnp.exp(x - v2[0]) for x in t2], axis=0).astype(BF)
    c1_ref[h, :, cols] = jnp.concatenate(c1, axis=0)
    e1_ref[h, :, cols] = jnp.concatenate([jnp.exp(x - v1[0]) * inv_z for x in t1], axis=0)

    no_tie_rank_sum = float(sum(range(PEER_TOPK)) + (N_KEYS - PEER_TOPK) * PEER_TOPK)
    bad = jnp.where(_column_sum(rank2) != no_tie_rank_sum, 1.0, 0.0)
    bad = bad + jnp.where(_column_sum([jnp.where(x >= v1[PEER_TOPK - 1], 1.0, 0.0) for x in t1]) != PEER_TOPK, 1.0, 0.0)
    for a in range(PEER_TOPK - 1):
        bad = bad + jnp.where(v1[a] == v1[a + 1], 1.0, 0.0)
    return bad + jnp.where(_column_sum(chosen) != PEER_TOPK, 1.0, 0.0)


def _route_kernel(x_ref, wh_ref, wl_ref, kh_ref, kl_ref, fidx_ref, r2_ref, e2_ref, c1_ref, e1_ref):
    x = x_ref[...]
    t = x.shape[0]
    xh, xl = _split_bf16(x)
    q = _dot(xh, wh_ref[...]) + (_dot(xl, wh_ref[...]) + _dot(xh, wl_ref[...]))
    outs = (r2_ref, e2_ref, c1_ref, e1_ref)

    def scores(hs):
        qh, ql = _split_bf16(q[:, hs * PEER_HALF:(hs + 1) * PEER_HALF])
        return _dot_nt(kh_ref[hs], qh) + (_dot_nt(kl_ref[hs], qh) + _dot_nt(kh_ref[hs], ql))

    all_scores = [scores(hs) for hs in range(2 * PEER_HEADS)]
    bad = []
    for h in range(PEER_HEADS):
        flags = [_route_head_fast(all_scores[2 * h][:, c * LANES:(c + 1) * LANES],
                                  all_scores[2 * h + 1][:, c * LANES:(c + 1) * LANES], outs, h,
                                  slice(c * LANES, (c + 1) * LANES)) for c in range(t // LANES)]
        bad.append(sum(flags[1:], flags[0]))

    for h in range(PEER_HEADS):
        @pl.when(jnp.max(bad[h]) > 0.0)
        def _(h=h):
            _route_head_exact(all_scores[2 * h], all_scores[2 * h + 1], fidx_ref[...], outs, h)


def _route(x1, lw, tb):
    n = x1.shape[0]
    fidx = jnp.broadcast_to(jnp.asarray(_cand_flat_index())[:, None], (CAND_ROWS, tb))
    out = jax.ShapeDtypeStruct((PEER_HEADS, N_KEYS, n), F32)
    out_bf = jax.ShapeDtypeStruct((PEER_HEADS, N_KEYS, n), BF)
    ospec = pl.BlockSpec((PEER_HEADS, N_KEYS, tb), lambda i: (0, 0, i))
    full2 = lambda a: pl.BlockSpec(a.shape, lambda i: (0, 0))
    full3 = lambda a: pl.BlockSpec(a.shape, lambda i: (0, 0, 0))
    return pl.pallas_call(
        _route_kernel,
        out_shape=(out_bf, out_bf, out, out),
        grid=(n // tb,),
        in_specs=[pl.BlockSpec((tb, D_MODEL), lambda i: (i, 0)), full2(lw["wq_hi"]), full2(lw["wq_lo"]),
                  full3(lw["keys_hi"]), full3(lw["keys_lo"]), full2(fidx)],
        out_specs=(ospec, ospec, ospec, ospec),
        compiler_params=_params(("arbitrary",)),
        name="peer_route",
    )(x1, lw["wq_hi"], lw["wq_lo"], lw["keys_hi"], lw["keys_lo"], fidx)


PEER_EXPERT_BLOCK = 1024


def _peer_kernel(x_ref, u_ref, vt_ref, r2_ref, e2_ref, c1_ref, e1_ref, lg_ref, lb_ref, o_ref, xb, acc, wt, rowbuf, act):
    s = pl.program_id(1)
    last = pl.num_programs(1) - 1
    tb = x_ref.shape[0]

    @pl.when(s == 0)
    def _():
        xb[...] = x_ref[...].T.astype(BF)
        acc[...] = jnp.zeros_like(acc)
        act[1] = jnp.zeros(act.shape[1:], BF)

    keys_per_step = PEER_EXPERT_BLOCK // N_KEYS
    first_key = pl.multiple_of(jnp.maximum(s - 1, 0) * keys_per_step, keys_per_step)

    for h in range(PEER_HEADS):
        for src, ref in enumerate((c1_ref, e1_ref)):
            blk = ref[h, pl.ds(first_key, keys_per_step), :]
            for j in range(keys_per_step):
                rowbuf[src, h, j] = jnp.broadcast_to(blk[j:j + 1], (SUBLANES, tb)).astype(BF)

    half = N_KEYS // 2
    group = keys_per_step // 2
    tiles = half // SUBLANES

    def spread(row):
        return jnp.broadcast_to(row[None], (tiles, SUBLANES, BF_CHUNK)).reshape(half, BF_CHUNK)

    for tc in range(tb // BF_CHUNK):
        cols = slice(tc * BF_CHUNK, (tc + 1) * BF_CHUNK)
        for jg in range(keys_per_step // group):
            for kh in range(2):
                keys = slice(kh * half, (kh + 1) * half)
                w = [jnp.zeros((half, BF_CHUNK), BF) for _ in range(group)]
                for h in range(PEER_HEADS):
                    r2 = r2_ref[h, keys, cols]
                    e2 = e2_ref[h, keys, cols]
                    for jj in range(group):
                        j = jg * group + jj
                        c1 = spread(rowbuf[0, h, j, :, cols])
                        e1 = spread(rowbuf[1, h, j, :, cols])
                        w[jj] = w[jj] + jnp.where(r2 < c1, e2 * e1, 0)
                for jj in range(group):
                    rows = slice((jg * group + jj) * N_KEYS + kh * half, (jg * group + jj) * N_KEYS + (kh + 1) * half)
                    wt[rows, cols] = w[jj]
    wt[...] = wt[...] * jax.nn.gelu(act[(s + 1) % 2])
    acc[...] += _dot(vt_ref[...], wt[...])
    act[s % 2] = _dot(u_ref[...], xb[...]).astype(BF)

    @pl.when(s == last)
    def _():
        r = DN_ALPHA * x_ref[...] + acc[...].T
        o_ref[...] = _layer_norm(r, lg_ref[...], lb_ref[...])


def _peer(x1, route, lw, tb):
    n = x1.shape[0]
    eb = PEER_EXPERT_BLOCK
    nblk = N_EXPERTS // eb
    rspec = pl.BlockSpec((PEER_HEADS, N_KEYS, tb), lambda i, e: (0, 0, i))
    vec = pl.BlockSpec((1, D_MODEL), lambda i, e: (0, 0))
    return pl.pallas_call(
        _peer_kernel,
        out_shape=jax.ShapeDtypeStruct((n, D_MODEL), F32),
        grid=(n // tb, nblk + 1),
        in_specs=[pl.BlockSpec((tb, D_MODEL), lambda i, s: (i, 0)),
                  pl.BlockSpec((eb, D_MODEL), lambda i, s: (jnp.minimum(s, nblk - 1), 0)),
                  pl.BlockSpec((D_MODEL, eb), lambda i, s: (0, jnp.maximum(s - 1, 0))),
                  rspec, rspec, rspec, rspec, vec, vec],
        out_specs=pl.BlockSpec((tb, D_MODEL), lambda i, s: (i, 0)),
        scratch_shapes=[pltpu.VMEM((D_MODEL, tb), BF), pltpu.VMEM((D_MODEL, tb), F32), pltpu.VMEM((eb, tb), BF),
                        pltpu.VMEM((2, PEER_HEADS, eb // N_KEYS, SUBLANES, tb), BF),
                        pltpu.VMEM((2, eb, tb), BF)],
        compiler_params=_params(("arbitrary", "arbitrary")),
        name="peer_experts",
    )(x1, lw["peer_u"], lw["peer_vt"], *route, lw["ln2_g"], lw["ln2_b"])


def _ple_kernel(x_ref, pe_ref, wg_ref, wp_ref, lg_ref, lb_ref, o_ref):
    x = x_ref[...]
    ple = jax.nn.sigmoid(_dot(x.astype(BF), wg_ref[...])) * _dot(pe_ref[...].astype(BF), wp_ref[...])
    o_ref[...] = _layer_norm(DN_ALPHA * x + ple, lg_ref[...], lb_ref[...])


def _ple(x2, pe, lw, tm):
    n = x2.shape[0]
    full = lambda a: pl.BlockSpec(a.shape, lambda i: (0, 0))
    ws = (lw["ple_w_g"], lw["ple_w_p"], lw["ln3_g"], lw["ln3_b"])
    return pl.pallas_call(
        _ple_kernel,
        out_shape=jax.ShapeDtypeStruct((n, D_MODEL), F32),
        grid=(n // tm,),
        in_specs=[pl.BlockSpec((tm, D_MODEL), lambda i: (i, 0)), pl.BlockSpec((tm, PLE_DIM), lambda i: (i, 0))]
        + [full(a) for a in ws],
        out_specs=pl.BlockSpec((tm, D_MODEL), lambda i: (i, 0)),
        compiler_params=_params(("arbitrary",)),
        name="ple",
    )(x2, pe, *ws)


def _time_major(u, bsz, t, bp):
    u = jnp.transpose(u.reshape(bsz, t, -1), (1, 0, 2))
    return jnp.pad(u, ((0, 0), (0, bp - bsz), (0, 0))).reshape(t * bp, -1)


def _batch_major(y, bsz, t, bp):
    return jnp.transpose(y.reshape(t, bp, -1)[:, :bsz], (1, 0, 2)).reshape(bsz * t, -1)


def _layer(x, pe, h0_re, h0_im, cache_k, cache_v, s0, pos, lw, bsz, t, prompt, n_rows):
    n = bsz * t
    h = _in_projection(x, lw["w_in"], 512, IN_COLS // 2)

    bp = max(8, bsz)
    u_tb = _time_major(h[:, OFF_U_SSM:OFF_U_SSM + SSM_WIDTH], bsz, t, bp)
    h0 = jnp.concatenate([h0_re.reshape(bsz, SSM_FLAT), h0_im.reshape(bsz, SSM_FLAT)], axis=1).astype(F32)
    h0 = jnp.pad(h0, ((0, bp - bsz), (0, 0)))
    y_tb, h_last = _s5_mixer(u_tb, h0, lw["s5"], t, bp, CHUNK)
    y_ssm = _batch_major(y_tb, bsz, t, bp)
    new_re = h_last[:bsz, :SSM_FLAT].reshape(bsz, SSM_GROUPS, SSM_STATE)
    new_im = h_last[:bsz, SSM_FLAT:].reshape(bsz, SSM_GROUPS, SSM_STATE)

    k_att = h[:, OFF_K_ATT:OFF_K_ATT + ATT_WIDTH].reshape(bsz, t, ATT_HEADS, ATT_HEAD_DIM)
    v_att = h[:, OFF_V_ATT:OFF_V_ATT + ATT_WIDTH].reshape(bsz, t, ATT_HEADS, ATT_HEAD_DIM)
    if prompt:
        y_att = _attn_prompt(h, lw["rel_bias"], bsz, t)
        new_k, new_v = k_att[:, -n_rows:], v_att[:, -n_rows:]
    else:
        r = cache_k.shape[1]
        y_att = _attn_sample(h, cache_k.reshape(bsz, r, ATT_WIDTH), cache_v.reshape(bsz, r, ATT_WIDTH),
                             lw["rel_bias"], bsz)
        new_k, new_v = k_att, v_att

    y_ret, s_pair = _retention(h, _pair_states(s0), _retention_tables(pos), lw["ret_gn_g"], bsz, t,
                               4 if prompt else 1)
    s_fin = _unpair_states(s_pair)

    tok = 512 if n % 512 == 0 else 256
    x1 = _merge(x, h, y_ssm, y_att, y_ret, lw, tok)
    route = _route(x1, lw, 256)
    x2 = _peer(x1, route, lw, tok)
    x3 = _ple(x2, pe, lw, tok)
    return x3, (new_re, new_im, new_k, new_v, s_fin)


def _layer_weights(li, w_in, ssm_lam_re, ssm_lam_im, ssm_b_re, ssm_b_im, ssm_c_re, ssm_c_im, ssm_log_dt, ssm_d,
                   ssm_w_glu, ssm_b_glu, att_rel_bias, ret_gn_g, w_br_ssm, w_br_att, w_br_ret, w_o, ln1_g, ln1_b,
                   peer_w_q, peer_sub_keys, peer_u, peer_v, ln2_g, ln2_b, ple_w_g, ple_w_p, ln3_g, ln3_b):
    vec = lambda a: a[li].astype(F32).reshape(1, -1)
    w = w_in[li]
    wq_hi, wq_lo = _split_bf16(peer_w_q[li].astype(F32))
    keys_hi, keys_lo = _split_bf16(peer_sub_keys[li].astype(F32).reshape(2 * PEER_HEADS, N_KEYS, PEER_HALF))
    return {
        "w_in": jnp.concatenate([w[:, OLD_GATE_START:], w[:, :OLD_GATE_START]], axis=1).astype(BF),
        "s5": _s5_tables(ssm_lam_re[li], ssm_lam_im[li], ssm_b_re[li], ssm_b_im[li], ssm_c_re[li], ssm_c_im[li],
                         ssm_log_dt[li], ssm_d[li], ssm_w_glu[li], ssm_b_glu[li]),
        "rel_bias": _rel_bias_tile(att_rel_bias[li]),
        "ret_gn_g": vec(ret_gn_g),
        "w_br_ssm": w_br_ssm[li].astype(BF), "w_br_att": w_br_att[li].astype(BF), "w_br_ret": w_br_ret[li].astype(BF),
        "w_o": w_o[li].astype(BF), "ln1_g": vec(ln1_g), "ln1_b": vec(ln1_b),
        "wq_hi": wq_hi, "wq_lo": wq_lo, "keys_hi": keys_hi, "keys_lo": keys_lo,
        "peer_u": peer_u[li].astype(BF), "peer_vt": peer_v[li].astype(BF).T,
        "ln2_g": vec(ln2_g), "ln2_b": vec(ln2_b),
        "ple_w_g": ple_w_g[li].astype(BF), "ple_w_p": ple_w_p[li].astype(BF), "ln3_g": vec(ln3_g), "ln3_b": vec(ln3_b),
    }


def kernel(x_prompt, x_sample, p_prompt, p_sample, state_ssm_re, state_ssm_im, cache_attn_k, cache_attn_v, state_ret, w_in, ssm_lam_re, ssm_lam_im, ssm_b_re, ssm_b_im, ssm_c_re, ssm_c_im, ssm_log_dt, ssm_d, ssm_w_glu, ssm_b_glu, att_rel_bias, ret_gn_g, w_br_ssm, w_br_att, w_br_ret, w_o, ln1_g, ln1_b, peer_w_q, peer_sub_keys, peer_u, peer_v, ln2_g, ln2_b, ple_w_g, ple_w_p, ln3_g, ln3_b):
    bp, tp, _ = x_prompt.shape
    bs, ts, _ = x_sample.shape
    n_rows = cache_attn_k.shape[2]
    pos_p = np.arange(tp)
    pos_s = PAST_LEN + np.arange(ts)
    y_p = x_prompt.reshape(bp * tp, D_MODEL).astype(F32)
    y_s = x_sample.reshape(bs * ts, D_MODEL).astype(F32)
    new_p, new_s = [], []
    for li in range(DEPTH):
        lw = _layer_weights(li, w_in, ssm_lam_re, ssm_lam_im, ssm_b_re, ssm_b_im, ssm_c_re, ssm_c_im, ssm_log_dt,
                            ssm_d, ssm_w_glu, ssm_b_glu, att_rel_bias, ret_gn_g, w_br_ssm, w_br_att, w_br_ret, w_o,
                            ln1_g, ln1_b, peer_w_q, peer_sub_keys, peer_u, peer_v, ln2_g, ln2_b, ple_w_g, ple_w_p,
                            ln3_g, ln3_b)
        zero_ssm = jnp.zeros((bp, SSM_GROUPS, SSM_STATE), F32)
        zero_ret = jnp.zeros((bp, RET_HEADS, RET_DIM, RET_DIM), F32)
        y_p, st = _layer(y_p, p_prompt[li].reshape(bp * tp, PLE_DIM), zero_ssm, zero_ssm, None, None, zero_ret,
                         pos_p, lw, bp, tp, True, n_rows)
        new_p.append(st)
        y_s, st = _layer(y_s, p_sample[li].reshape(bs * ts, PLE_DIM), state_ssm_re[li], state_ssm_im[li],
                         cache_attn_k[li], cache_attn_v[li], state_ret[li], pos_s, lw, bs, ts, False, n_rows)
        new_s.append(st)
    stack = lambda states, k: jnp.stack([s[k] for s in states])
    return (y_p.reshape(bp, tp, D_MODEL), y_s.reshape(bs, ts, D_MODEL),
            stack(new_p, 0), stack(new_p, 1), stack(new_p, 2), stack(new_p, 3), stack(new_p, 4),
            stack(new_s, 0), stack(new_s, 1), stack(new_s, 2), stack(new_s, 3), stack(new_s, 4))
```
